```python
import jax, jax.numpy as jnp
from jax import lax
import numpy as np

D_MODEL = 1024
BATCH = 32
SEQ = 2048
DEPTH = 4

N_EVEN = (DEPTH + 1) // 2
N_ODD = DEPTH // 2
CHUNK = 64
EPS = 1e-6
RET_HEADS = 4
RET_DV = D_MODEL // 2 // RET_HEADS
RET_DK = RET_DV
ROPE_BASE = 10000.0
GLA_HEADS = 4
GLA_DV = D_MODEL // 2 // GLA_HEADS
GLA_DK = GLA_DV // 2
GLA_RANK = 16
GLA_TAU = 16.0
AB_SIZES = (RET_HEADS * RET_DK, RET_HEADS * RET_DK, RET_HEADS * RET_DV, RET_HEADS * RET_DV,
            GLA_HEADS * GLA_DK, GLA_HEADS * GLA_DK, GLA_HEADS * GLA_DV, GLA_HEADS * GLA_DV,
            GLA_RANK)
AB_IN = sum(AB_SIZES)
HGRN_DK = 128
HGRN_HEADS = D_MODEL // HGRN_DK
HGRN_DV = D_MODEL // HGRN_HEADS
C_IN = 4 * D_MODEL
D_FF = ((8 * D_MODEL // 3 + 255) // 256) * 256
N_EXPERTS = 8
TOP_K = 2
EXPERT_FF = 7 * D_MODEL // 2
OUT_SCALE = (2 * DEPTH) ** -0.5

kernel_name = "hybrid_retnet_gla_hgrn2_moe_trunk"


def rms_norm(x, g):
    xf = x.astype(jnp.float32)
    y = xf * lax.rsqrt(jnp.mean(xf * xf, axis=-1, keepdims=True) + EPS)
    return (y * g.astype(jnp.float32)).astype(x.dtype)


def rotary(x, positions):
    d = x.shape[-1]
    inv_freq = ROPE_BASE ** (-jnp.arange(0, d, 2, dtype=jnp.float32) / d)
    ang = positions.astype(jnp.float32)[..., None, None] * inv_freq
    cos, sin = jnp.cos(ang), jnp.sin(ang)
    xf = x.astype(jnp.float32)
    x1, x2 = xf[..., : d // 2], xf[..., d // 2:]
    return jnp.concatenate([x1 * cos - x2 * sin, x2 * cos + x1 * sin], axis=-1)


def _to_chunks(a):
    b, t, h, d = a.shape
    return a.reshape(b, t // CHUNK, CHUNK, h, d).transpose(1, 0, 3, 2, 4)


def _from_chunks(a):
    nc, b, h, c, d = a.shape
    return a.transpose(1, 0, 3, 2, 4).reshape(b, nc * c, h, d)


def _causal_mask():
    t = jnp.arange(CHUNK)
    return t[:, None] >= t[None, :]


def retention_chunkwise(q, k, v, log_gamma):
    b, _, h, dk = q.shape
    dv = v.shape[-1]
    t = jnp.arange(CHUNK, dtype=jnp.float32)
    lg = log_gamma.astype(jnp.float32)
    rel = t[:, None] - t[None, :]
    intra_decay = jnp.exp(jnp.where(_causal_mask()[None], rel[None] * lg[:, None, None], -jnp.inf))
    q_decay = jnp.exp((t[None, :] + 1.0) * lg[:, None])[..., None]
    k_decay = jnp.exp((CHUNK - 1.0 - t[None, :]) * lg[:, None])[..., None]
    chunk_decay = jnp.exp(CHUNK * lg)[:, None, None]

    def step(state, inp):
        qc, kc, vc = inp
        scores = jnp.einsum('bhid,bhjd->bhij', qc, kc) * intra_decay
        o = (jnp.einsum('bhij,bhjv->bhiv', scores, vc)
             + jnp.einsum('bhid,bhdv->bhiv', qc * q_decay, state))
        state = chunk_decay * state + jnp.einsum('bhjd,bhjv->bhdv', kc * k_decay, vc)
        return state, o

    s0 = jnp.zeros((b, h, dk, dv), jnp.float32)
    _, o = lax.scan(step, s0, (_to_chunks(q), _to_chunks(k), _to_chunks(v)))
    return _from_chunks(o)


def gated_chunkwise(q, k, v, log_f):
    b, _, h, dk = q.shape
    dv = v.shape[-1]
    mask = _causal_mask()[:, :, None]
    g_cum = jnp.cumsum(_to_chunks(log_f), axis=-2)

    def step(state, inp):
        qc, kc, vc, gc = inp
        diff = gc[:, :, :, None, :] - gc[:, :, None, :, :]
        decay = jnp.exp(jnp.where(mask, diff, -jnp.inf))
        scores = jnp.einsum('bhid,bhjd,bhijd->bhij', qc, kc, decay)
        g_last = gc[:, :, -1:, :]
        o = (jnp.einsum('bhij,bhjv->bhiv', scores, vc)
             + jnp.einsum('bhid,bhdv->bhiv', qc * jnp.exp(gc), state))
        state = (jnp.exp(g_last[:, :, 0, :])[..., None] * state
                 + jnp.einsum('bhjd,bhjv->bhdv', kc * jnp.exp(g_last - gc), vc))
        return state, o

    s0 = jnp.zeros((b, h, dk, dv), jnp.float32)
    _, o = lax.scan(step, s0, (_to_chunks(q), _to_chunks(k), _to_chunks(v), g_cum))
    return _from_chunks(o)


def retnet_gla_mixer(h, positions, w_in, w_a2, b_a2, ret_g, gla_g, w_out):
    b, t, _ = h.shape
    z = h @ w_in
    aq, ak, av, ag, bq, bk, bv, bg, ba = jnp.split(z, np.cumsum(AB_SIZES)[:-1], axis=-1)
    f32 = jnp.float32

    def heads(a, n):
        return a.reshape(b, t, n, -1).astype(f32)

    qa = rotary(heads(aq, RET_HEADS), positions)
    ka = rotary(heads(ak, RET_HEADS), positions) * (RET_DK ** -0.5)
    log_gamma = jnp.log1p(-jnp.exp2(-5.0 - jnp.arange(RET_HEADS, dtype=f32)))
    oa = retention_chunkwise(qa, ka, heads(av, RET_HEADS), log_gamma)
    oa = rms_norm(oa, ret_g) * jax.nn.silu(heads(ag, RET_HEADS))
    log_alpha = jax.nn.log_sigmoid((ba @ w_a2 + b_a2).astype(f32)) / GLA_TAU
    ob = gated_chunkwise(heads(bq, GLA_HEADS) * (GLA_DK ** -0.5), heads(bk, GLA_HEADS),
                         heads(bv, GLA_HEADS), heads(log_alpha, GLA_HEADS))
    ob = rms_norm(ob, gla_g) * jax.nn.silu(heads(bg, GLA_HEADS))
    o = jnp.concatenate([oa.reshape(b, t, -1), ob.reshape(b, t, -1)], axis=-1)
    return o.astype(h.dtype) @ w_out


def hgrn2_mixer(h, lb, w_in, norm_g, w_out):
    b, t, d = h.shape
    f32 = jnp.float32
    q, f, i, g = jnp.split(h @ w_in, 4, axis=-1)

    def heads(a):
        return a.reshape(b, t, HGRN_HEADS, -1).astype(f32)

    f = heads(f)
    lb = lb.astype(f32)
    log_f = jnp.logaddexp(jnp.log(lb), jnp.log1p(-lb) + jax.nn.log_sigmoid(f))
    k = (1.0 - lb) * jax.nn.sigmoid(-f)
    qh = jax.nn.silu(heads(q)) * (HGRN_DK ** -0.5)
    o = gated_chunkwise(qh, k, heads(i), log_f)
    o = rms_norm(o, norm_g) * jax.nn.silu(heads(g))
    return o.reshape(b, t, d).astype(h.dtype) @ w_out


def swiglu(h, w_gate, w_up, w_down):
    return (jax.nn.silu(h @ w_gate) * (h @ w_up)) @ w_down


def moe_swiglu(h, w_router, w_gate, w_up, w_down):
    b, t, d = h.shape
    xt = h.reshape(b * t, d)
    logits = (xt @ w_router).astype(jnp.float32)
    top_logits, top_idx = lax.top_k(logits, TOP_K)
    gates = jax.nn.softmax(top_logits, axis=-1)
    flat_e = top_idx.reshape(-1)
    order = jnp.argsort(flat_e)
    group_sizes = jnp.bincount(flat_e, length=N_EXPERTS).astype(jnp.int32)
    xs = xt[order // TOP_K]
    hid = (jax.nn.silu(lax.ragged_dot(xs, w_gate, group_sizes))
           * lax.ragged_dot(xs, w_up, group_sizes))
    ys = lax.ragged_dot(hid, w_down, group_sizes)
    y = jnp.zeros_like(ys).at[order].set(ys).reshape(b * t, TOP_K, d)
    out = jnp.einsum('nk,nkd->nd', gates.astype(y.dtype), y)
    return out.reshape(b, t, d)


def setup_inputs(seed: int = 0) -> dict:
    key = jax.random.key(seed)
    ks = jax.random.split(key, 24)
    f32 = jnp.float32

    def nrm(k, shape, scale):
        return jax.random.normal(k, shape, f32) * scale

    def gain(k, shape):
        return 1.0 + 0.02 * jax.random.normal(k, shape, f32)

    ds = D_MODEL ** -0.5
    return {
        "x": nrm(ks[0], (BATCH, SEQ, D_MODEL), 1.0),
        "positions": jnp.broadcast_to(jnp.arange(SEQ, dtype=jnp.int32), (BATCH, SEQ)),
        "norm_mix_g": gain(ks[1], (DEPTH, D_MODEL)),
        "norm_ffn_g": gain(ks[2], (DEPTH, D_MODEL)),
        "final_norm_g": gain(ks[3], (D_MODEL,)),
        "ab_w_in": nrm(ks[4], (N_EVEN, D_MODEL, AB_IN), ds),
        "gla_w_a2": nrm(ks[5], (N_EVEN, GLA_RANK, GLA_HEADS * GLA_DK), GLA_RANK ** -0.5),
        "gla_b_a2": nrm(ks[6], (N_EVEN, GLA_HEADS * GLA_DK), 0.1),
        "ret_norm_g": gain(ks[7], (N_EVEN, RET_DV)),
        "gla_norm_g": gain(ks[8], (N_EVEN, GLA_DV)),
        "ab_w_out": nrm(ks[9], (N_EVEN, D_MODEL, D_MODEL), ds * OUT_SCALE),
        "ffn_w_gate": nrm(ks[10], (N_EVEN, D_MODEL, D_FF), ds),
        "ffn_w_up": nrm(ks[11], (N_EVEN, D_MODEL, D_FF), ds),
        "ffn_w_down": nrm(ks[12], (N_EVEN, D_FF, D_MODEL), D_FF ** -0.5 * OUT_SCALE),
        "hgrn_lb_logits": nrm(ks[13], (DEPTH, HGRN_HEADS * HGRN_DK), 0.1),
        "c_w_in": nrm(ks[14], (N_ODD, D_MODEL, C_IN), ds),
        "hgrn_norm_g": gain(ks[15], (N_ODD, HGRN_DV)),
        "c_w_out": nrm(ks[16], (N_ODD, D_MODEL, D_MODEL), ds * OUT_SCALE),
        "moe_router": nrm(ks[17], (N_ODD, D_MODEL, N_EXPERTS), ds),
        "moe_w_gate": nrm(ks[18], (N_ODD, N_EXPERTS, D_MODEL, EXPERT_FF), ds),
        "moe_w_up": nrm(ks[19], (N_ODD, N_EXPERTS, D_MODEL, EXPERT_FF), ds),
        "moe_w_down": nrm(ks[20], (N_ODD, N_EXPERTS, EXPERT_FF, D_MODEL), EXPERT_FF ** -0.5 * OUT_SCALE),
    }


def reference(x, positions, norm_mix_g, norm_ffn_g, final_norm_g, ab_w_in, gla_w_a2, gla_b_a2,
              ret_norm_g, gla_norm_g, ab_w_out, ffn_w_gate, ffn_w_up, ffn_w_down,
              hgrn_lb_logits, c_w_in, hgrn_norm_g, c_w_out, moe_router, moe_w_gate,
              moe_w_up, moe_w_down):
    lb_cum = jnp.cumsum(jax.nn.softmax(hgrn_lb_logits.astype(jnp.float32), axis=0), axis=0)
    lower_bounds = lb_cum - lb_cum[0:1]
    h = x
    for l in range(DEPTH):
        e = l // 2
        hn = rms_norm(h, norm_mix_g[l])
        if l % 2 == 0:
            h = h + retnet_gla_mixer(hn, positions, ab_w_in[e], gla_w_a2[e], gla_b_a2[e],
                                     ret_norm_g[e], gla_norm_g[e], ab_w_out[e])
            h = h + swiglu(rms_norm(h, norm_ffn_g[l]), ffn_w_gate[e], ffn_w_up[e], ffn_w_down[e])
        else:
            lb = lower_bounds[l].reshape(HGRN_HEADS, HGRN_DK)
            h = h + hgrn2_mixer(hn, lb, c_w_in[e], hgrn_norm_g[e], c_w_out[e])
            h = h + moe_swiglu(rms_norm(h, norm_ffn_g[l]), moe_router[e], moe_w_gate[e],
                               moe_w_up[e], moe_w_down[e])
    return rms_norm(h, final_norm_g)
```

```python
import functools
import math

import numpy as np
import jax
import jax.numpy as jnp
from jax import lax
from jax.experimental import pallas as pl
from jax.experimental.pallas import tpu as pltpu

F32 = jnp.float32
BF16 = jnp.bfloat16
I32 = jnp.int32

EPS = 1e-6
LANES = 128
CHUNK = 64
N_LEVELS = 6
HEAD_DV = 128
ROPE_BASE = 10000.0
GLA_TAU = 16.0
N_EXPERTS = 8
VMEM_LIMIT = 56 * 1024 * 1024

_NT = (((1,), (1,)), ((), ()))
_TN = (((0,), (0,)), ((), ()))


def _cparams(n_axes):
    return pltpu.CompilerParams(dimension_semantics=("arbitrary",) * n_axes,
                                vmem_limit_bytes=VMEM_LIMIT)


def _rms(x, g):
    ms = jnp.mean(x * x, axis=-1, keepdims=True)
    return x * lax.rsqrt(ms + EPS) * g


def _silu(x):
    return x * _sigmoid(x)


def _sigmoid(x):
    e = jnp.exp(-jnp.abs(x))
    r = 1.0 / (1.0 + e)
    return jnp.where(x >= 0, r, e * r)


def _bdot(a, b):
    return jnp.dot(a.astype(BF16), b.astype(BF16), preferred_element_type=F32)


def _bdot_nt(a, b):
    return lax.dot_general(a.astype(BF16), b.astype(BF16), _NT, preferred_element_type=F32)


def _bdot_tn(a, b):
    return lax.dot_general(a.astype(BF16), b.astype(BF16), _TN, preferred_element_type=F32)


def _norm_proj_body(h_ref, g_ref, w_ref, z_ref, *, col_chunk):
    n = _rms(h_ref[...], g_ref[...]).astype(BF16)
    m = w_ref.shape[1]
    for c in range(0, m, col_chunk):
        ce = min(c + col_chunk, m)
        z_ref[:, c:ce] = jnp.dot(n, w_ref[:, c:ce], preferred_element_type=F32).astype(z_ref.dtype)


def _norm_proj(h, g, w, *, tm=512, col_chunk=512):
    n_tok, d = h.shape
    m = w.shape[1]
    assert n_tok % tm == 0
    return pl.pallas_call(
        functools.partial(_norm_proj_body, col_chunk=col_chunk),
        grid=(n_tok // tm,),
        in_specs=[pl.BlockSpec((tm, d), lambda i: (i, 0)),
                  pl.BlockSpec((1, d), lambda i: (0, 0)),
                  pl.BlockSpec((d, m), lambda i: (0, 0), pipeline_mode=pl.Buffered(1))],
        out_specs=pl.BlockSpec((tm, m), lambda i: (i, 0)),
        out_shape=jax.ShapeDtypeStruct((n_tok, m), BF16),
        compiler_params=_cparams(1),
        name="norm_proj",
    )(h, g.reshape(1, d), w)


def _proj_res_body(*refs, n_in):
    h_ref = refs[0]
    x_refs = refs[1:1 + n_in]
    w_refs = refs[1 + n_in:1 + 2 * n_in]
    o_ref = refs[1 + 2 * n_in]
    acc = h_ref[...]
    for x_ref, w_ref in zip(x_refs, w_refs):
        acc = acc + jnp.dot(x_ref[...], w_ref[...], preferred_element_type=F32)
    o_ref[...] = acc


def _proj_residual(h, xs, ws, *, tm=512):
    n_tok, d = h.shape
    n_in = len(xs)
    in_specs = [pl.BlockSpec((tm, d), lambda i: (i, 0))]
    in_specs += [pl.BlockSpec((tm, x.shape[1]), lambda i: (i, 0)) for x in xs]
    in_specs += [pl.BlockSpec(w.shape, lambda i: (0, 0), pipeline_mode=pl.Buffered(1)) for w in ws]
    return pl.pallas_call(
        functools.partial(_proj_res_body, n_in=n_in),
        grid=(n_tok // tm,),
        in_specs=in_specs,
        out_specs=pl.BlockSpec((tm, d), lambda i: (i, 0)),
        out_shape=jax.ShapeDtypeStruct((n_tok, d), F32),
        compiler_params=_cparams(1),
        name="proj_residual",
    )(h, *xs, *ws)


def _ffn_body(h_ref, g_ref, wg_ref, wu_ref, wd_ref, o_ref, *, ff_chunk):
    h = h_ref[...]
    n = _rms(h, g_ref[...]).astype(BF16)
    acc = h
    for c in range(0, wg_ref.shape[1], ff_chunk):
        a = jnp.dot(n, wg_ref[:, c:c + ff_chunk], preferred_element_type=F32)
        b = jnp.dot(n, wu_ref[:, c:c + ff_chunk], preferred_element_type=F32)
        hid = (_silu(a) * b).astype(BF16)
        acc = acc + jnp.dot(hid, wd_ref[c:c + ff_chunk, :], preferred_element_type=F32)
    o_ref[...] = acc


def _ffn(h, g, wg, wu, wd, *, tm=512, ff_chunk=1408):
    n_tok, d = h.shape
    ff = wg.shape[1]
    assert ff % ff_chunk == 0
    const = lambda shape: pl.BlockSpec(shape, lambda i: (0, 0), pipeline_mode=pl.Buffered(1))
    return pl.pallas_call(
        functools.partial(_ffn_body, ff_chunk=ff_chunk),
        grid=(n_tok // tm,),
        in_specs=[pl.BlockSpec((tm, d), lambda i: (i, 0)),
                  pl.BlockSpec((1, d), lambda i: (0, 0)),
                  const((d, ff)), const((d, ff)), const((ff, d))],
        out_specs=pl.BlockSpec((tm, d), lambda i: (i, 0)),
        out_shape=jax.ShapeDtypeStruct((n_tok, d), F32),
        compiler_params=_cparams(1),
        name="ffn_swiglu",
    )(h, g.reshape(1, d), wg, wu, wd)


def _decay_sum_matrix():
    c = CHUNK
    r = np.arange(c)[:, None]
    t = np.arange(c)[None, :]
    blocks = [(t <= r), (t > r)]
    for lvl in range(N_LEVELS):
        s = c >> (lvl + 1)
        m = (r // (2 * s)) * (2 * s) + s - 1
        upper = r > m
        blocks.append(np.where(upper, (t > m) & (t <= r), (t > r) & (t <= m)))
    return np.concatenate(blocks, axis=0).astype(np.float32)


def _level_masks():
    c = CHUNK
    row = lax.broadcasted_iota(I32, (c, LANES), 0)
    ri = lax.broadcasted_iota(I32, (c, c), 0)
    ci = lax.broadcasted_iota(I32, (c, c), 1)
    uppers, blocks = [], []
    for lvl in range(N_LEVELS):
        s = c >> (lvl + 1)
        uppers.append((row // s) % 2 == 1)
        blocks.append((ri // (2 * s)) == (ci // (2 * s)))
    return uppers, blocks, ri == ci


def _gated_group(q, k, lf, vs, lane_masks, st, wsum, masks):
    c = CHUNK
    uppers, blocks, eye = masks
    hi = lf.astype(BF16)
    lo = (lf - hi.astype(F32)).astype(BF16)
    ex = jnp.dot(wsum, jnp.concatenate([hi, lo], axis=1), preferred_element_type=F32)
    fac = jnp.exp(ex[:, :LANES] + ex[:, LANES:])
    f_q = fac[0:c]
    f_k = fac[c:2 * c]
    f_last = f_q[c - 1:c]
    kb = []
    for lvl in range(N_LEVELS):
        f_l = fac[(2 + lvl) * c:(3 + lvl) * c]
        kb.append(jnp.where(uppers[lvl], 0.0, k * f_l).astype(BF16))
    k_state = k * f_k
    outs = []
    new_st = st * f_last
    st_b = st.astype(BF16)
    for v, lm in zip(vs, lane_masks):
        qh = q if lm is None else jnp.where(lm, q, 0.0)
        diag = jnp.sum(qh * k, axis=-1, keepdims=True)
        scores = jnp.where(eye, diag, 0.0)
        for lvl in range(N_LEVELS):
            f_l = fac[(2 + lvl) * c:(3 + lvl) * c]
            qa = jnp.where(uppers[lvl], qh * f_l, 0.0)
            s_l = lax.dot_general(qa.astype(BF16), kb[lvl], _NT, preferred_element_type=F32)
            scores = scores + (s_l if lvl == 0 else jnp.where(blocks[lvl], s_l, 0.0))
        vb = v.astype(BF16)
        o = jnp.dot(scores.astype(BF16), vb, preferred_element_type=F32)
        o = o + lax.dot_general((qh * f_q).astype(BF16), st_b, _NT, preferred_element_type=F32)
        outs.append(o)
        kh = k_state if lm is None else jnp.where(lm, k_state, 0.0)
        new_st = new_st + lax.dot_general(vb, kh.astype(BF16), _TN, preferred_element_type=F32)
    return outs, new_st


def _head_out(o, norm_g, gate):
    return (_rms(o, norm_g) * _silu(gate)).astype(BF16)


def _retention_body(pos_ref, invf_ref, ng_ref, q_ref, k_ref, v_ref, gate_ref, o_ref,
                    st_ref, cos_ref, sin_ref, *, n_heads):
    c = CHUNK
    tb = q_ref.shape[0]

    @pl.when(pl.program_id(1) == 0)
    def _():
        st_ref[...] = jnp.zeros_like(st_ref)

    ang = pos_ref[0].astype(F32) * invf_ref[...]
    lane = lax.broadcasted_iota(I32, (1, LANES), 1)
    cos_ref[...] = jnp.cos(ang)
    sin_ref[...] = jnp.sin(ang) * jnp.where(lane < LANES // 2, -1.0, 1.0)

    ri = lax.broadcasted_iota(I32, (c, c), 0)
    ci = lax.broadcasted_iota(I32, (c, c), 1)
    rel = (ri - ci).astype(F32)
    trow = lax.broadcasted_iota(I32, (c, LANES), 0).astype(F32)
    k_scale = float(LANES) ** -0.5
    ng = ng_ref[...]
    log_gammas = [math.log1p(-(2.0 ** (-5 - h))) for h in range(n_heads)]
    dmats = [jnp.where(rel >= 0, jnp.exp(rel * lg), 0.0) for lg in log_gammas]
    q_decays = [jnp.exp((trow + 1.0) * lg) for lg in log_gammas]
    k_decays = [jnp.exp((c - 1.0 - trow) * lg) for lg in log_gammas]

    def chunk(ic, carry):
        r0 = pl.multiple_of(ic * c, c)
        rows = pl.ds(r0, c)
        cosv = cos_ref[rows, :]
        sinv = sin_ref[rows, :]
        for h in range(n_heads):
            cols = slice(h * LANES, (h + 1) * LANES)
            q = q_ref[rows, cols].astype(F32)
            k = k_ref[rows, cols].astype(F32)
            v = v_ref[rows, cols]
            qr = q * cosv + pltpu.roll(q, LANES // 2, 1) * sinv
            kr = (k * cosv + pltpu.roll(k, LANES // 2, 1) * sinv) * k_scale
            scores = _bdot_nt(qr, kr) * dmats[h]
            st = st_ref[h]
            o = jnp.dot(scores.astype(BF16), v, preferred_element_type=F32)
            o = o + _bdot_nt(qr * q_decays[h], st)
            st_ref[h] = math.exp(c * log_gammas[h]) * st + lax.dot_general(
                v, (kr * k_decays[h]).astype(BF16), _TN, preferred_element_type=F32)
            o_ref[rows, cols] = _head_out(o, ng, gate_ref[rows, cols].astype(F32))
        return carry

    lax.fori_loop(0, tb // c, chunk, 0)


def _retention(z, pos3, inv_freq2, norm_g, *, batch, seq, n_heads, col0, tb=512):
    n_tok = z.shape[0]
    w = n_heads * LANES
    assert seq % tb == 0 and tb % CHUNK == 0
    nt = seq // tb
    cb = col0 // w
    zspec = lambda j: pl.BlockSpec((tb, w), lambda b, t, j=j: (b * nt + t, cb + j))
    return pl.pallas_call(
        functools.partial(_retention_body, n_heads=n_heads),
        grid=(batch, nt),
        in_specs=[pl.BlockSpec((1, tb, 1), lambda b, t: (b, t, 0)),
                  pl.BlockSpec((1, LANES), lambda b, t: (0, 0)),
                  pl.BlockSpec((1, LANES), lambda b, t: (0, 0)),
                  zspec(0), zspec(1), zspec(2), zspec(3)],
        out_specs=pl.BlockSpec((tb, w), lambda b, t: (b * nt + t, 0)),
        out_shape=jax.ShapeDtypeStruct((n_tok, w), BF16),
        scratch_shapes=[pltpu.VMEM((n_heads, HEAD_DV, LANES), F32),
                        pltpu.VMEM((tb, LANES), F32),
                        pltpu.VMEM((tb, LANES), F32)],
        compiler_params=_cparams(2),
        name="retention",
    )(pos3, inv_freq2, norm_g.reshape(1, LANES), z, z, z, z)


def _gla_body(wsum_ref, wa_ref, ba2_ref, ng_ref, a_ref, q_ref, k_ref, v_ref, gate_ref, o_ref,
              st_ref, lf_ref, *, n_groups):
    c = CHUNK
    tb = q_ref.shape[0]

    @pl.when(pl.program_id(1) == 0)
    def _():
        st_ref[...] = jnp.zeros_like(st_ref)

    x = jnp.dot(a_ref[...], wa_ref[...], preferred_element_type=F32) + ba2_ref[...]
    lf_ref[...] = (jnp.minimum(x, 0.0) - jnp.log1p(jnp.exp(-jnp.abs(x)))) * (1.0 / GLA_TAU)

    masks = _level_masks()
    lane = lax.broadcasted_iota(I32, (1, LANES), 1)
    lane_masks = [lane < LANES // 2, lane >= LANES // 2]
    wsum = wsum_ref[...]
    ng = ng_ref[...]
    q_scale = float(LANES // 2) ** -0.5

    def chunk(ic, carry):
        r0 = pl.multiple_of(ic * c, c)
        rows = pl.ds(r0, c)
        for g in range(n_groups):
            cols = slice(g * LANES, (g + 1) * LANES)
            q = q_ref[rows, cols].astype(F32) * q_scale
            k = k_ref[rows, cols].astype(F32)
            vcols = [slice((2 * g + j) * HEAD_DV, (2 * g + j + 1) * HEAD_DV) for j in range(2)]
            vs = [v_ref[rows, vc] for vc in vcols]
            outs, new_st = _gated_group(q, k, lf_ref[rows, cols], vs, lane_masks, st_ref[g],
                                        wsum, masks)
            st_ref[g] = new_st
            for o, vc in zip(outs, vcols):
                o_ref[rows, vc] = _head_out(o, ng, gate_ref[rows, vc].astype(F32))
        return carry

    lax.fori_loop(0, tb // c, chunk, 0)


def _gla(z, wsum, wa_pad, b_a2, norm_g, *, batch, seq, n_heads, col0, col_a, tb=512):
    n_tok = z.shape[0]
    wk = n_heads * (LANES // 2)
    wv = n_heads * HEAD_DV
    assert seq % tb == 0 and tb % CHUNK == 0
    nt = seq // tb
    n_groups = n_heads // 2
    row = lambda b, t: b * nt + t
    const = lambda shape: pl.BlockSpec(shape, lambda b, t: (0, 0))
    return pl.pallas_call(
        functools.partial(_gla_body, n_groups=n_groups),
        grid=(batch, nt),
        in_specs=[const(wsum.shape), const(wa_pad.shape), const((1, wk)), const((1, LANES)),
                  pl.BlockSpec((tb, LANES), lambda b, t: (row(b, t), col_a // LANES)),
                  pl.BlockSpec((tb, wk), lambda b, t: (row(b, t), col0 // wk)),
                  pl.BlockSpec((tb, wk), lambda b, t: (row(b, t), col0 // wk + 1)),
                  pl.BlockSpec((tb, wv), lambda b, t: (row(b, t), (col0 + 2 * wk) // wv)),
                  pl.BlockSpec((tb, wv), lambda b, t: (row(b, t), (col0 + 2 * wk) // wv + 1))],
        out_specs=pl.BlockSpec((tb, wv), lambda b, t: (row(b, t), 0)),
        out_shape=jax.ShapeDtypeStruct((n_tok, wv), BF16),
        scratch_shapes=[pltpu.VMEM((n_groups, HEAD_DV, LANES), F32),
                        pltpu.VMEM((tb, wk), F32)],
        compiler_params=_cparams(2),
        name="gla",
    )(wsum, wa_pad, b_a2.reshape(1, wk), norm_g.reshape(1, LANES), z, z, z, z, z)


def _hgrn_body(wsum_ref, lb_ref, ng_ref, q_ref, f_ref, i_ref, gate_ref, o_ref, st_ref, *, n_heads):
    c = CHUNK
    tb = q_ref.shape[0]

    @pl.when(pl.program_id(2) == 0)
    def _():
        st_ref[...] = jnp.zeros_like(st_ref)

    masks = _level_masks()
    wsum = wsum_ref[...]
    ng = ng_ref[...]
    q_scale = float(LANES) ** -0.5

    def chunk(ic, carry):
        r0 = pl.multiple_of(ic * c, c)
        rows = pl.ds(r0, c)
        for h in range(n_heads):
            cols = slice(h * LANES, (h + 1) * LANES)
            lb = lb_ref[:, cols]
            f = f_ref[rows, cols].astype(F32)
            e = jnp.exp(-jnp.abs(f))
            r = 1.0 / (1.0 + e)
            sig_pos = jnp.where(f >= 0, r, e * r)
            sig_neg = jnp.where(f >= 0, e * r, r)
            lf = jnp.log(lb + (1.0 - lb) * sig_pos)
            k = (1.0 - lb) * sig_neg
            q = _silu(q_ref[rows, cols].astype(F32)) * q_scale
            outs, new_st = _gated_group(q, k, lf, [i_ref[rows, cols]], [None], st_ref[h],
                                        wsum, masks)
            st_ref[h] = new_st
            o_ref[rows, cols] = _head_out(outs[0], ng, gate_ref[rows, cols].astype(F32))
        return carry

    lax.fori_loop(0, tb // c, chunk, 0)


def _hgrn(z, wsum, lb, norm_g, *, batch, seq, n_heads, heads_per_step=4, tb=512):
    n_tok = z.shape[0]
    w = heads_per_step * LANES
    ng_ = n_heads // heads_per_step
    assert seq % tb == 0 and tb % CHUNK == 0
    nt = seq // tb
    zspec = lambda j: pl.BlockSpec((tb, w), lambda b, g, t, j=j: (b * nt + t, j * ng_ + g))
    return pl.pallas_call(
        functools.partial(_hgrn_body, n_heads=heads_per_step),
        grid=(batch, ng_, nt),
        in_specs=[pl.BlockSpec(wsum.shape, lambda b, g, t: (0, 0)),
                  pl.BlockSpec((1, w), lambda b, g, t: (0, g)),
                  pl.BlockSpec((1, LANES), lambda b, g, t: (0, 0)),
                  zspec(0), zspec(1), zspec(2), zspec(3)],
        out_specs=pl.BlockSpec((tb, w), lambda b, g, t: (b * nt + t, g)),
        out_shape=jax.ShapeDtypeStruct((n_tok, n_heads * LANES), BF16),
        scratch_shapes=[pltpu.VMEM((heads_per_step, HEAD_DV, LANES), F32)],
        compiler_params=_cparams(3),
        name="hgrn2",
    )(wsum, lb.reshape(1, n_heads * LANES), norm_g.reshape(1, LANES), z, z, z, z)


def _router_body(h_ref, g_ref, w_ref, route_ref, cnt_ref, run_ref):
    tm = h_ref.shape[0]

    @pl.when(pl.program_id(0) == 0)
    def _():
        run_ref[...] = jnp.zeros_like(run_ref)

    n = _rms(h_ref[...], g_ref[...])
    w = w_ref[...]
    n_hi = n.astype(BF16)
    n_lo = (n - n_hi.astype(F32)).astype(BF16)
    w_hi = w.astype(BF16)
    w_lo = (w - w_hi.astype(F32)).astype(BF16)
    logits = (jnp.dot(n_hi, w_hi, preferred_element_type=F32)
              + jnp.dot(n_hi, w_lo, preferred_element_type=F32)
              + jnp.dot(n_lo, w_hi, preferred_element_type=F32))

    lane = lax.broadcasted_iota(I32, (tm, LANES), 1)
    lane_f = lane.astype(F32)
    neg = -jnp.inf
    lg1 = jnp.where(lane < N_EXPERTS, logits, neg)
    m1 = jnp.max(lg1, axis=-1, keepdims=True)
    i1 = jnp.min(jnp.where(lg1 == m1, lane_f, float(LANES)), axis=-1, keepdims=True)
    oh1 = lane_f == i1
    lg2 = jnp.where(oh1, neg, lg1)
    m2 = jnp.max(lg2, axis=-1, keepdims=True)
    i2 = jnp.min(jnp.where(lg2 == m2, lane_f, float(LANES)), axis=-1, keepdims=True)
    oh2 = lane_f == i2
    e2 = jnp.exp(m2 - m1)
    g1 = 1.0 / (1.0 + e2)
    g2 = e2 * g1

    both = jnp.where(oh1, 1.0, 0.0) + jnp.where(oh2, 1.0, 0.0)
    ri = lax.broadcasted_iota(I32, (tm, tm), 0)
    ci = lax.broadcasted_iota(I32, (tm, tm), 1)
    lstrict = jnp.where(ri > ci, 1.0, 0.0).astype(BF16)
    before = jnp.dot(lstrict, both.astype(BF16), preferred_element_type=F32) + run_ref[...]
    r1 = jnp.sum(jnp.where(oh1, before, 0.0), axis=-1, keepdims=True)
    r2 = jnp.sum(jnp.where(oh2, before, 0.0), axis=-1, keepdims=True)
    run_new = run_ref[...] + jnp.sum(both, axis=0, keepdims=True)
    run_ref[...] = run_new
    cnt_ref[...] = run_new

    out = jnp.where(lane == 0, i1, 0.0)
    out = jnp.where(lane == 1, i2, out)
    out = jnp.where(lane == 2, g1, out)
    out = jnp.where(lane == 3, g2, out)
    out = jnp.where(lane == 4, r1, out)
    out = jnp.where(lane == 5, r2, out)
    route_ref[...] = out


def _router(h, g, w_pad, *, tm=512):
    n_tok, d = h.shape
    return pl.pallas_call(
        _router_body,
        grid=(n_tok // tm,),
        in_specs=[pl.BlockSpec((tm, d), lambda i: (i, 0)),
                  pl.BlockSpec((1, d), lambda i: (0, 0)),
                  pl.BlockSpec((d, LANES), lambda i: (0, 0))],
        out_specs=[pl.BlockSpec((tm, LANES), lambda i: (i, 0)),
                   pl.BlockSpec((1, LANES), lambda i: (0, 0))],
        out_shape=[jax.ShapeDtypeStruct((n_tok, LANES), F32),
                   jax.ShapeDtypeStruct((1, LANES), F32)],
        scratch_shapes=[pltpu.VMEM((1, LANES), F32)],
        compiler_params=_cparams(1),
        name="moe_router",
    )(h, g.reshape(1, d), w_pad)


def _dispatch_body(info_ref, h_ref, g_ref, pos_ref, xs_ref, nbuf, zbuf, idx, sem, isem, *, tm):
    i = pl.program_id(0)
    tg = zbuf.shape[0]

    def zero_copy(e):
        return pltpu.make_async_copy(zbuf, xs_ref.at[pl.ds(pl.multiple_of(info_ref[e], tg), tg)], sem)

    @pl.when(i == 0)
    def _():
        zbuf[...] = jnp.zeros_like(zbuf)
        for e in range(2 * N_EXPERTS):
            @pl.when(info_ref[e] >= 0)
            def _():
                zero_copy(e).start()
        for e in range(2 * N_EXPERTS):
            @pl.when(info_ref[e] >= 0)
            def _():
                zero_copy(e).wait()

    idx_copy = pltpu.make_async_copy(pos_ref.at[i], idx, isem)
    idx_copy.start()
    nbuf[...] = _rms(h_ref[...], g_ref[...])
    idx_copy.wait()

    def row_copy(t, p):
        return pltpu.make_async_copy(nbuf.at[pl.ds(t, 1)], xs_ref.at[pl.ds(p, 1)], sem)

    def issue(t, carry):
        row_copy(t, idx[t]).start()
        row_copy(t, idx[tm + t]).start()
        return carry

    lax.fori_loop(0, tm, issue, 0)
    for _ in range(2):
        pltpu.make_async_copy(nbuf, xs_ref.at[pl.ds(0, tm)], sem).wait()


def _dispatch(h, g, pos_tiles, info, rows_pad, *, tm, tg):
    n_tok, d = h.shape
    return pl.pallas_call(
        functools.partial(_dispatch_body, tm=tm),
        grid_spec=pltpu.PrefetchScalarGridSpec(
            num_scalar_prefetch=1,
            grid=(n_tok // tm,),
            in_specs=[pl.BlockSpec((tm, d), lambda i, info: (i, 0)),
                      pl.BlockSpec((1, d), lambda i, info: (0, 0)),
                      pl.BlockSpec(memory_space=pl.ANY)],
            out_specs=pl.BlockSpec(memory_space=pl.ANY),
            scratch_shapes=[pltpu.VMEM((tm, d), F32),
                            pltpu.VMEM((tg, d), F32),
                            pltpu.SMEM((2 * tm,), I32),
                            pltpu.SemaphoreType.DMA,
                            pltpu.SemaphoreType.DMA]),
        out_shape=jax.ShapeDtypeStruct((rows_pad, d), F32),
        compiler_params=_cparams(1),
        name="moe_dispatch",
    )(info, h, g.reshape(1, d), pos_tiles)


def _gmm_body(te_ref, nu_ref, x_ref, wg_ref, wu_ref, wd_ref, y_ref, *, ff_chunk):
    @pl.when(pl.program_id(0) >= nu_ref[0])
    def _():
        y_ref[...] = jnp.zeros_like(y_ref)

    @pl.when(pl.program_id(0) < nu_ref[0])
    def _():
        x = x_ref[...].astype(BF16)
        acc = None
        for c in range(0, wg_ref.shape[1], ff_chunk):
            a = jnp.dot(x, wg_ref[:, c:c + ff_chunk], preferred_element_type=F32)
            b = jnp.dot(x, wu_ref[:, c:c + ff_chunk], preferred_element_type=F32)
            hid = (_silu(a) * b).astype(BF16)
            part = jnp.dot(hid, wd_ref[c:c + ff_chunk, :], preferred_element_type=F32)
            acc = part if acc is None else acc + part
        y_ref[...] = acc


def _gmm(xs, tile_expert, n_used, wg, wu, wd, *, tm, ff_chunk=896):
    rows_pad, d = xs.shape
    ff = wg.shape[2]
    assert ff % ff_chunk == 0
    tile = lambda i, te, nu: jnp.minimum(i, nu[0] - 1)
    wspec = lambda shape: pl.BlockSpec((None,) + shape, lambda i, te, nu: (te[tile(i, te, nu)], 0, 0),
                                       pipeline_mode=pl.Buffered(1))
    return pl.pallas_call(
        functools.partial(_gmm_body, ff_chunk=ff_chunk),
        grid_spec=pltpu.PrefetchScalarGridSpec(
            num_scalar_prefetch=2,
            grid=(rows_pad // tm,),
            in_specs=[pl.BlockSpec((tm, d), lambda i, te, nu: (tile(i, te, nu), 0)),
                      wspec((d, ff)), wspec((d, ff)), wspec((ff, d))],
            out_specs=pl.BlockSpec((tm, d), lambda i, te, nu: (i, 0))),
        out_shape=jax.ShapeDtypeStruct((rows_pad, d), F32),
        compiler_params=_cparams(1),
        name="moe_experts",
    )(tile_expert, n_used, xs, wg, wu, wd)


def _combine_body(h_ref, route_ref, fg_ref, pos_ref, ys_ref, o_ref, gbuf, idx, sem, isem, *,
                  tm, final_norm):
    i = pl.program_id(0)
    idx_copy = pltpu.make_async_copy(pos_ref.at[i], idx, isem)
    idx_copy.start()
    idx_copy.wait()

    def row_copy(s, t, p):
        return pltpu.make_async_copy(ys_ref.at[pl.ds(p, 1)], gbuf.at[s, pl.ds(t, 1)], sem)

    def issue(t, carry):
        row_copy(0, t, idx[t]).start()
        row_copy(1, t, idx[tm + t]).start()
        return carry

    lax.fori_loop(0, tm, issue, 0)
    for s in range(2):
        pltpu.make_async_copy(ys_ref.at[pl.ds(0, tm)], gbuf.at[s], sem).wait()

    route = route_ref[...]
    out = h_ref[...] + route[:, 2:3] * gbuf[0] + route[:, 3:4] * gbuf[1]
    if final_norm:
        out = _rms(out, fg_ref[...])
    o_ref[...] = out


def _combine(h, route, final_g, pos_tiles, ys, *, tm, final_norm):
    n_tok, d = h.shape
    return pl.pallas_call(
        functools.partial(_combine_body, tm=tm, final_norm=final_norm),
        grid=(n_tok // tm,),
        in_specs=[pl.BlockSpec((tm, d), lambda i: (i, 0)),
                  pl.BlockSpec((tm, LANES), lambda i: (i, 0)),
                  pl.BlockSpec((1, d), lambda i: (0, 0)),
                  pl.BlockSpec(memory_space=pl.ANY),
                  pl.BlockSpec(memory_space=pl.ANY)],
        out_specs=pl.BlockSpec((tm, d), lambda i: (i, 0)),
        out_shape=jax.ShapeDtypeStruct((n_tok, d), F32),
        scratch_shapes=[pltpu.VMEM((2, tm, d), F32),
                        pltpu.SMEM((2 * tm,), I32),
                        pltpu.SemaphoreType.DMA,
                        pltpu.SemaphoreType.DMA],
        compiler_params=_cparams(1),
        name="moe_combine",
    )(h, route, final_g.reshape(1, d), pos_tiles, ys)


def _moe(h, norm_g, w_router, wg, wu, wd, final_g, *, final_norm, tm_route=512, tm_gmm=512,
         tm_rows=256):
    n_tok, d = h.shape
    w_pad = jnp.zeros((d, LANES), F32).at[:, :N_EXPERTS].set(w_router.astype(F32))
    route, counts = _router(h, norm_g, w_pad, tm=tm_route)

    cnt = counts[0, :N_EXPERTS].astype(I32)
    gsz = ((cnt + tm_gmm - 1) // tm_gmm) * tm_gmm
    ends = jnp.cumsum(gsz)
    offs = ends - gsz
    rows_pad = (n_tok * 2 // tm_gmm + N_EXPERTS) * tm_gmm
    n_tiles = rows_pad // tm_gmm
    tile_expert = jnp.minimum(
        jnp.searchsorted(ends, jnp.arange(n_tiles, dtype=I32) * tm_gmm, side="right"),
        N_EXPERTS - 1).astype(I32)
    n_used = (ends[-1:] // tm_gmm).astype(I32)
    tail = ends[-1] + jnp.arange(N_EXPERTS, dtype=I32) * tm_gmm
    info = jnp.concatenate([jnp.where(gsz > 0, ends - tm_gmm, -1),
                            jnp.where(tail < rows_pad, tail, -1)]).astype(I32)

    e01 = route[:, 0:2].astype(I32)
    pos = offs[e01] + route[:, 4:6].astype(I32)
    pos_tiles = pos.reshape(n_tok // tm_rows, tm_rows, 2).transpose(0, 2, 1).reshape(
        n_tok // tm_rows, 2 * tm_rows)

    xs = _dispatch(h, norm_g, pos_tiles, info, rows_pad, tm=tm_rows, tg=tm_gmm)
    ys = _gmm(xs, tile_expert, n_used, wg, wu, wd, tm=tm_gmm)
    return _combine(h, route, final_g, pos_tiles, ys, tm=tm_rows, final_norm=final_norm)


def kernel(x, positions, norm_mix_g, norm_ffn_g, final_norm_g, ab_w_in, gla_w_a2, gla_b_a2,
           ret_norm_g, gla_norm_g, ab_w_out, ffn_w_gate, ffn_w_up, ffn_w_down,
           hgrn_lb_logits, c_w_in, hgrn_norm_g, c_w_out, moe_router, moe_w_gate,
           moe_w_up, moe_w_down):
    batch, seq, d = x.shape
    depth = norm_mix_g.shape[0]
    n_tok = batch * seq
    ret_heads = 4
    gla_heads = 4
    gla_rank = gla_w_a2.shape[1]
    hgrn_heads = d // LANES
    ab_cols = ab_w_in.shape[2] - gla_rank

    lb_cum = jnp.cumsum(jax.nn.softmax(hgrn_lb_logits.astype(F32), axis=0), axis=0)
    lower_bounds = lb_cum - lb_cum[0:1]

    wsum = jnp.asarray(_decay_sum_matrix(), BF16)
    inv_freq = ROPE_BASE ** (-jnp.arange(0, LANES, 2, dtype=F32) / LANES)
    inv_freq2 = jnp.concatenate([inv_freq, inv_freq]).reshape(1, LANES)
    pos3 = positions.reshape(batch, seq, 1)

    h = x.reshape(n_tok, d)
    for l in range(depth):
        e = l // 2
        if l % 2 == 0:
            w_in = jnp.pad(ab_w_in[e], ((0, 0), (0, LANES - gla_rank))).astype(BF16)
            z = _norm_proj(h, norm_mix_g[l], w_in)
            oa = _retention(z, pos3, inv_freq2, ret_norm_g[e], batch=batch, seq=seq,
                            n_heads=ret_heads, col0=0)
            wa_pad = jnp.pad(gla_w_a2[e], ((0, LANES - gla_rank), (0, 0))).astype(BF16)
            ob = _gla(z, wsum, wa_pad, gla_b_a2[e], gla_norm_g[e], batch=batch, seq=seq,
                      n_heads=gla_heads, col0=4 * ret_heads * LANES, col_a=ab_cols)
            w_out = ab_w_out[e].astype(BF16)
            half = ret_heads * HEAD_DV
            h = _proj_residual(h, [oa, ob], [w_out[:half], w_out[half:]])
            h = _ffn(h, norm_ffn_g[l], ffn_w_gate[e].astype(BF16), ffn_w_up[e].astype(BF16),
                     ffn_w_down[e].astype(BF16))
        else:
            z = _norm_proj(h, norm_mix_g[l], c_w_in[e].astype(BF16))
            o = _hgrn(z, wsum, lower_bounds[l], hgrn_norm_g[e], batch=batch, seq=seq,
                      n_heads=hgrn_heads)
            h = _proj_residual(h, [o], [c_w_out[e].astype(BF16)])
            h = _moe(h, norm_ffn_g[l], moe_router[e], moe_w_gate[e].astype(BF16),
                     moe_w_up[e].astype(BF16), moe_w_down[e].astype(BF16), final_norm_g,
                     final_norm=(l == depth - 1))
    if depth % 2 == 1:
        raise NotImplementedError("final norm is fused into the last (odd) layer")
    return h.reshape(batch, seq, d)
```

```python
import functools
import math

import numpy as np
import jax
import jax.numpy as jnp
from jax import lax
from jax.experimental import pallas as pl
from jax.experimental.pallas import tpu as pltpu

F32 = jnp.float32
BF16 = jnp.bfloat16
I32 = jnp.int32

EPS = 1e-6
LANES = 128
CHUNK = 64
N_LEVELS = 6
HEAD_DV = 128
ROPE_BASE = 10000.0
GLA_TAU = 16.0
LOG2_E = 1.4426950408889634
N_EXPERTS = 8
VMEM_LIMIT = 56 * 1024 * 1024

_NT = (((1,), (1,)), ((), ()))
_TN = (((0,), (0,)), ((), ()))


def _cparams(n_axes):
    return pltpu.CompilerParams(dimension_semantics=("arbitrary",) * n_axes,
                                vmem_limit_bytes=VMEM_LIMIT)


def _rms(x, g):
    ms = jnp.mean(x * x, axis=-1, keepdims=True)
    return x * lax.rsqrt(ms + EPS) * g


def _silu(x):
    return x * _sigmoid(x)


def _sigmoid(x):
    e = jnp.exp(-jnp.abs(x))
    r = 1.0 / (1.0 + e)
    return jnp.where(x >= 0, r, e * r)


def _bdot(a, b):
    return jnp.dot(a.astype(BF16), b.astype(BF16), preferred_element_type=F32)


def _bdot_nt(a, b):
    return lax.dot_general(a.astype(BF16), b.astype(BF16), _NT, preferred_element_type=F32)


def _bdot_tn(a, b):
    return lax.dot_general(a.astype(BF16), b.astype(BF16), _TN, preferred_element_type=F32)


def _norm_proj_body(h_ref, g_ref, w_ref, z_ref, *, col_chunk):
    n = _rms(h_ref[...], g_ref[...]).astype(BF16)
    m = w_ref.shape[1]
    for c in range(0, m, col_chunk):
        ce = min(c + col_chunk, m)
        z_ref[:, c:ce] = jnp.dot(n, w_ref[:, c:ce], preferred_element_type=F32).astype(z_ref.dtype)


def _norm_proj(h, g, w, *, tm=512, col_chunk=512):
    n_tok, d = h.shape
    m = w.shape[1]
    assert n_tok % tm == 0
    return pl.pallas_call(
        functools.partial(_norm_proj_body, col_chunk=col_chunk),
        grid=(n_tok // tm,),
        in_specs=[pl.BlockSpec((tm, d), lambda i: (i, 0)),
                  pl.BlockSpec((1, d), lambda i: (0, 0)),
                  pl.BlockSpec((d, m), lambda i: (0, 0), pipeline_mode=pl.Buffered(1))],
        out_specs=pl.BlockSpec((tm, m), lambda i: (i, 0)),
        out_shape=jax.ShapeDtypeStruct((n_tok, m), BF16),
        compiler_params=_cparams(1),
        name="norm_proj",
    )(h, g.reshape(1, d), w)


def _proj_res_body(*refs, n_in):
    h_ref = refs[0]
    x_refs = refs[1:1 + n_in]
    w_refs = refs[1 + n_in:1 + 2 * n_in]
    o_ref = refs[1 + 2 * n_in]
    acc = h_ref[...]
    for x_ref, w_ref in zip(x_refs, w_refs):
        acc = acc + jnp.dot(x_ref[...], w_ref[...], preferred_element_type=F32)
    o_ref[...] = acc


def _proj_residual(h, xs, ws, *, tm=512):
    n_tok, d = h.shape
    n_in = len(xs)
    in_specs = [pl.BlockSpec((tm, d), lambda i: (i, 0))]
    in_specs += [pl.BlockSpec((tm, x.shape[1]), lambda i: (i, 0)) for x in xs]
    in_specs += [pl.BlockSpec(w.shape, lambda i: (0, 0), pipeline_mode=pl.Buffered(1)) for w in ws]
    return pl.pallas_call(
        functools.partial(_proj_res_body, n_in=n_in),
        grid=(n_tok // tm,),
        in_specs=in_specs,
        out_specs=pl.BlockSpec((tm, d), lambda i: (i, 0)),
        out_shape=jax.ShapeDtypeStruct((n_tok, d), F32),
        compiler_params=_cparams(1),
        name="proj_residual",
    )(h, *xs, *ws)


def _ffn_body(h_ref, g_ref, wg_ref, wu_ref, wd_ref, o_ref, *, ff_chunk):
    h = h_ref[...]
    n = _rms(h, g_ref[...]).astype(BF16)
    acc = h
    for c in range(0, wg_ref.shape[1], ff_chunk):
        a = jnp.dot(n, wg_ref[:, c:c + ff_chunk], preferred_element_type=F32)
        b = jnp.dot(n, wu_ref[:, c:c + ff_chunk], preferred_element_type=F32)
        hid = (_silu(a) * b).astype(BF16)
        acc = acc + jnp.dot(hid, wd_ref[c:c + ff_chunk, :], preferred_element_type=F32)
    o_ref[...] = acc


def _ffn(h, g, wg, wu, wd, *, tm=512, ff_chunk=1408):
    n_tok, d = h.shape
    ff = wg.shape[1]
    assert ff % ff_chunk == 0
    const = lambda shape: pl.BlockSpec(shape, lambda i: (0, 0), pipeline_mode=pl.Buffered(1))
    return pl.pallas_call(
        functools.partial(_ffn_body, ff_chunk=ff_chunk),
        grid=(n_tok // tm,),
        in_specs=[pl.BlockSpec((tm, d), lambda i: (i, 0)),
                  pl.BlockSpec((1, d), lambda i: (0, 0)),
                  const((d, ff)), const((d, ff)), const((ff, d))],
        out_specs=pl.BlockSpec((tm, d), lambda i: (i, 0)),
        out_shape=jax.ShapeDtypeStruct((n_tok, d), F32),
        compiler_params=_cparams(1),
        name="ffn_swiglu",
    )(h, g.reshape(1, d), wg, wu, wd)


def _decay_sum_matrix():
    c = CHUNK
    r = np.arange(c)[:, None]
    t = np.arange(c)[None, :]
    blocks = [(t <= r), (t > r)]
    for lvl in range(N_LEVELS):
        s = c >> (lvl + 1)
        m = (r // (2 * s)) * (2 * s) + s - 1
        upper = r > m
        blocks.append(np.where(upper, (t > m) & (t <= r), (t > r) & (t <= m)))
    w = np.concatenate(blocks, axis=0).astype(np.float32)
    return np.concatenate([w, w], axis=1)


def _level_masks():
    c = CHUNK
    row = lax.broadcasted_iota(I32, (c, LANES), 0)
    ri = lax.broadcasted_iota(I32, (c, c), 0)
    ci = lax.broadcasted_iota(I32, (c, c), 1)
    uppers, pairs = [], []
    for lvl in range(N_LEVELS):
        s = c >> (lvl + 1)
        uppers.append((row // s) % 2 == 1)
        pairs.append(((ri // (2 * s)) == (ci // (2 * s)))
                     & ((ri // s) % 2 == 1) & ((ci // s) % 2 == 0))
    return uppers, pairs, ri == ci


def _gated_chunk(qs, ks, lfs, vss, lmss, sts, wsum2, masks):
    c = CHUNK
    uppers, pairs, eye = masks
    n_g = len(qs)
    pieces = []
    for lf in lfs:
        lf2 = lf * LOG2_E
        hi = lf2.astype(BF16)
        lo = (lf2 - hi.astype(F32)).astype(BF16)
        pieces.append(jnp.concatenate([hi, lo], axis=0))
    ex = jnp.dot(wsum2, jnp.concatenate(pieces, axis=1), preferred_element_type=F32)
    fac = jnp.exp2(ex)
    fac_b = fac.astype(BF16)

    grams, inters, updates, diags = [], [], [], []
    for g in range(n_g):
        cols = slice(g * LANES, (g + 1) * LANES)
        q_b = qs[g].astype(BF16)
        k_b = ks[g].astype(BF16)
        w = [jnp.where(uppers[l], q_b, k_b) * fac_b[(2 + l) * c:(3 + l) * c, cols]
             for l in range(N_LEVELS)]
        qf = q_b * fac_b[0:c, cols]
        kf = k_b * fac_b[c:2 * c, cols]
        st_b = sts[g].astype(BF16)
        qk = qs[g] * ks[g]
        for v, lm in zip(vss[g], lmss[g]):
            if lm is None:
                sel = lambda a: a
            else:
                sel = lambda a, lm=lm: jnp.where(lm, a, jnp.zeros_like(a))
            grams.append([lax.dot_general(sel(w[l]), w[l], _NT, preferred_element_type=F32)
                          for l in range(N_LEVELS)])
            inters.append(lax.dot_general(sel(qf), st_b, _NT, preferred_element_type=F32))
            updates.append(lax.dot_general(v, sel(kf), _TN, preferred_element_type=F32))
            diags.append(jnp.sum(sel(qk), axis=-1, keepdims=True))

    outs, new_sts = [], []
    h = 0
    for g in range(n_g):
        cols = slice(g * LANES, (g + 1) * LANES)
        new_st = sts[g] * fac[c - 1:c, cols]
        outs_g = []
        for v in vss[g]:
            scores = jnp.where(eye, diags[h], 0.0)
            for l in range(N_LEVELS):
                scores = jnp.where(pairs[l], grams[h][l], scores)
            outs_g.append(jnp.dot(scores.astype(BF16), v, preferred_element_type=F32) + inters[h])
            new_st = new_st + updates[h]
            h += 1
        outs.append(outs_g)
        new_sts.append(new_st)
    return outs, new_sts


def _head_out(o, norm_g, gate):
    return (_rms(o, norm_g) * _silu(gate)).astype(BF16)


def _retention_body(pos_ref, invf_ref, ng_ref, q_ref, k_ref, v_ref, gate_ref, o_ref,
                    st_ref, cos_ref, sin_ref, *, n_heads):
    c = CHUNK
    tb = q_ref.shape[0]

    @pl.when(pl.program_id(1) == 0)
    def _():
        st_ref[...] = jnp.zeros_like(st_ref)

    ang = pos_ref[0].astype(F32) * invf_ref[...]
    lane = lax.broadcasted_iota(I32, (1, LANES), 1)
    cos_ref[...] = jnp.cos(ang)
    sin_ref[...] = jnp.sin(ang) * jnp.where(lane < LANES // 2, -1.0, 1.0)

    ri = lax.broadcasted_iota(I32, (c, c), 0)
    ci = lax.broadcasted_iota(I32, (c, c), 1)
    rel = (ri - ci).astype(F32)
    trow = lax.broadcasted_iota(I32, (c, LANES), 0).astype(F32)
    k_scale = float(LANES) ** -0.5
    ng = ng_ref[...]
    log_gammas = [math.log1p(-(2.0 ** (-5 - h))) for h in range(n_heads)]
    dmats = [jnp.where(rel >= 0, jnp.exp(rel * lg), 0.0) for lg in log_gammas]
    q_decays = [jnp.exp((trow + 1.0) * lg) for lg in log_gammas]
    k_decays = [jnp.exp((c - 1.0 - trow) * lg) for lg in log_gammas]

    def chunk(ic, carry):
        r0 = pl.multiple_of(ic * c, c)
        rows = pl.ds(r0, c)
        cosv = cos_ref[rows, :]
        sinv = sin_ref[rows, :]
        for h in range(n_heads):
            cols = slice(h * LANES, (h + 1) * LANES)
            q = q_ref[rows, cols].astype(F32)
            k = k_ref[rows, cols].astype(F32)
            v = v_ref[rows, cols]
            qr = q * cosv + pltpu.roll(q, LANES // 2, 1) * sinv
            kr = (k * cosv + pltpu.roll(k, LANES // 2, 1) * sinv) * k_scale
            scores = _bdot_nt(qr, kr) * dmats[h]
            st = st_ref[h]
            o = jnp.dot(scores.astype(BF16), v, preferred_element_type=F32)
            o = o + _bdot_nt(qr * q_decays[h], st)
            st_ref[h] = math.exp(c * log_gammas[h]) * st + lax.dot_general(
                v, (kr * k_decays[h]).astype(BF16), _TN, preferred_element_type=F32)
            o_ref[rows, cols] = _head_out(o, ng, gate_ref[rows, cols].astype(F32))
        return carry

    lax.fori_loop(0, tb // c, chunk, 0, unroll=2)


def _retention(z, pos3, inv_freq2, norm_g, *, batch, seq, n_heads, col0, tb=512):
    n_tok = z.shape[0]
    w = n_heads * LANES
    assert seq % tb == 0 and tb % CHUNK == 0
    nt = seq // tb
    cb = col0 // w
    zspec = lambda j: pl.BlockSpec((tb, w), lambda b, t, j=j: (b * nt + t, cb + j))
    return pl.pallas_call(
        functools.partial(_retention_body, n_heads=n_heads),
        grid=(batch, nt),
        in_specs=[pl.BlockSpec((1, tb, 1), lambda b, t: (b, t, 0)),
                  pl.BlockSpec((1, LANES), lambda b, t: (0, 0)),
                  pl.BlockSpec((1, LANES), lambda b, t: (0, 0)),
                  zspec(0), zspec(1), zspec(2), zspec(3)],
        out_specs=pl.BlockSpec((tb, w), lambda b, t: (b * nt + t, 0)),
        out_shape=jax.ShapeDtypeStruct((n_tok, w), BF16),
        scratch_shapes=[pltpu.VMEM((n_heads, HEAD_DV, LANES), F32),
                        pltpu.VMEM((tb, LANES), F32),
                        pltpu.VMEM((tb, LANES), F32)],
        compiler_params=_cparams(2),
        name="retention",
    )(pos3, inv_freq2, norm_g.reshape(1, LANES), z, z, z, z)


def _gla_body(wsum_ref, wa_ref, ba2_ref, ng_ref, a_ref, q_ref, k_ref, v_ref, gate_ref, o_ref,
              st_ref, lf_ref, *, n_groups):
    c = CHUNK
    tb = q_ref.shape[0]

    @pl.when(pl.program_id(1) == 0)
    def _():
        st_ref[...] = jnp.zeros_like(st_ref)

    x = jnp.dot(a_ref[...], wa_ref[...], preferred_element_type=F32) + ba2_ref[...]
    lf_ref[...] = (jnp.minimum(x, 0.0) - jnp.log1p(jnp.exp(-jnp.abs(x)))) * (1.0 / GLA_TAU)

    masks = _level_masks()
    lane = lax.broadcasted_iota(I32, (1, LANES), 1)
    lane_masks = [lane < LANES // 2, lane >= LANES // 2]
    wsum = wsum_ref[...]
    ng = ng_ref[...]
    q_scale = float(LANES // 2) ** -0.5

    def chunk(ic, carry):
        r0 = pl.multiple_of(ic * c, c)
        rows = pl.ds(r0, c)
        kcols = [slice(g * LANES, (g + 1) * LANES) for g in range(n_groups)]
        vcols = [[slice((2 * g + j) * HEAD_DV, (2 * g + j + 1) * HEAD_DV) for j in range(2)]
                 for g in range(n_groups)]
        outs, new_sts = _gated_chunk(
            [q_ref[rows, kc].astype(F32) * q_scale for kc in kcols],
            [k_ref[rows, kc].astype(F32) for kc in kcols],
            [lf_ref[rows, kc] for kc in kcols],
            [[v_ref[rows, vc] for vc in vcs] for vcs in vcols],
            [lane_masks] * n_groups,
            [st_ref[g] for g in range(n_groups)],
            wsum, masks)
        for g in range(n_groups):
            st_ref[g] = new_sts[g]
            for o, vc in zip(outs[g], vcols[g]):
                o_ref[rows, vc] = _head_out(o, ng, gate_ref[rows, vc].astype(F32))
        return carry

    lax.fori_loop(0, tb // c, chunk, 0, unroll=2)


def _gla(z, wsum, wa_pad, b_a2, norm_g, *, batch, seq, n_heads, col0, col_a, tb=512):
    n_tok = z.shape[0]
    wk = n_heads * (LANES // 2)
    wv = n_heads * HEAD_DV
    assert seq % tb == 0 and tb % CHUNK == 0
    nt = seq // tb
    n_groups = n_heads // 2
    row = lambda b, t: b * nt + t
    const = lambda shape: pl.BlockSpec(shape, lambda b, t: (0, 0))
    return pl.pallas_call(
        functools.partial(_gla_body, n_groups=n_groups),
        grid=(batch, nt),
        in_specs=[const(wsum.shape), const(wa_pad.shape), const((1, wk)), const((1, LANES)),
                  pl.BlockSpec((tb, LANES), lambda b, t: (row(b, t), col_a // LANES)),
                  pl.BlockSpec((tb, wk), lambda b, t: (row(b, t), col0 // wk)),
                  pl.BlockSpec((tb, wk), lambda b, t: (row(b, t), col0 // wk + 1)),
                  pl.BlockSpec((tb, wv), lambda b, t: (row(b, t), (col0 + 2 * wk) // wv)),
                  pl.BlockSpec((tb, wv), lambda b, t: (row(b, t), (col0 + 2 * wk) // wv + 1))],
        out_specs=pl.BlockSpec((tb, wv), lambda b, t: (row(b, t), 0)),
        out_shape=jax.ShapeDtypeStruct((n_tok, wv), BF16),
        scratch_shapes=[pltpu.VMEM((n_groups, HEAD_DV, LANES), F32),
                        pltpu.VMEM((tb, wk), F32)],
        compiler_params=_cparams(2),
        name="gla",
    )(wsum, wa_pad, b_a2.reshape(1, wk), norm_g.reshape(1, LANES), z, z, z, z, z)


def _hgrn_body(wsum_ref, lb_ref, ng_ref, q_ref, f_ref, i_ref, gate_ref, o_ref, st_ref, *, n_heads):
    c = CHUNK
    tb = q_ref.shape[0]

    @pl.when(pl.program_id(2) == 0)
    def _():
        st_ref[...] = jnp.zeros_like(st_ref)

    masks = _level_masks()
    wsum = wsum_ref[...]
    ng = ng_ref[...]
    q_scale = float(LANES) ** -0.5

    def chunk(ic, carry):
        r0 = pl.multiple_of(ic * c, c)
        rows = pl.ds(r0, c)
        hcols = [slice(h * LANES, (h + 1) * LANES) for h in range(n_heads)]
        qs, ks, lfs = [], [], []
        for cols in hcols:
            lb = lb_ref[:, cols]
            f = f_ref[rows, cols].astype(F32)
            e = jnp.exp(-jnp.abs(f))
            r = 1.0 / (1.0 + e)
            sig_pos = jnp.where(f >= 0, r, e * r)
            sig_neg = jnp.where(f >= 0, e * r, r)
            lfs.append(jnp.log(lb + (1.0 - lb) * sig_pos))
            ks.append((1.0 - lb) * sig_neg)
            qs.append(_silu(q_ref[rows, cols].astype(F32)) * q_scale)
        outs, new_sts = _gated_chunk(
            qs, ks, lfs, [[i_ref[rows, cols]] for cols in hcols], [[None]] * n_heads,
            [st_ref[h] for h in range(n_heads)], wsum, masks)
        for h, cols in enumerate(hcols):
            st_ref[h] = new_sts[h]
            o_ref[rows, cols] = _head_out(outs[h][0], ng, gate_ref[rows, cols].astype(F32))
        return carry

    lax.fori_loop(0, tb // c, chunk, 0, unroll=2)


def _hgrn(z, wsum, lb, norm_g, *, batch, seq, n_heads, heads_per_step=8, tb=512):
    n_tok = z.shape[0]
    w = heads_per_step * LANES
    ng_ = n_heads // heads_per_step
    assert seq % tb == 0 and tb % CHUNK == 0
    nt = seq // tb
    zspec = lambda j: pl.BlockSpec((tb, w), lambda b, g, t, j=j: (b * nt + t, j * ng_ + g))
    return pl.pallas_call(
        functools.partial(_hgrn_body, n_heads=heads_per_step),
        grid=(batch, ng_, nt),
        in_specs=[pl.BlockSpec(wsum.shape, lambda b, g, t: (0, 0)),
                  pl.BlockSpec((1, w), lambda b, g, t: (0, g)),
                  pl.BlockSpec((1, LANES), lambda b, g, t: (0, 0)),
                  zspec(0), zspec(1), zspec(2), zspec(3)],
        out_specs=pl.BlockSpec((tb, w), lambda b, g, t: (b * nt + t, g)),
        out_shape=jax.ShapeDtypeStruct((n_tok, n_heads * LANES), BF16),
        scratch_shapes=[pltpu.VMEM((heads_per_step, HEAD_DV, LANES), F32)],
        compiler_params=_cparams(3),
        name="hgrn2",
    )(wsum, lb.reshape(1, n_heads * LANES), norm_g.reshape(1, LANES), z, z, z, z)


def _router_body(h_ref, g_ref, w_ref, route_ref, cnt_ref, run_ref):
    tm = h_ref.shape[0]

    @pl.when(pl.program_id(0) == 0)
    def _():
        run_ref[...] = jnp.zeros_like(run_ref)

    n = _rms(h_ref[...], g_ref[...])
    w = w_ref[...]
    n_hi = n.astype(BF16)
    n_lo = (n - n_hi.astype(F32)).astype(BF16)
    w_hi = w.astype(BF16)
    w_lo = (w - w_hi.astype(F32)).astype(BF16)
    logits = (jnp.dot(n_hi, w_hi, preferred_element_type=F32)
              + jnp.dot(n_hi, w_lo, preferred_element_type=F32)
              + jnp.dot(n_lo, w_hi, preferred_element_type=F32))

    lane = lax.broadcasted_iota(I32, (tm, LANES), 1)
    lane_f = lane.astype(F32)
    neg = -jnp.inf
    lg1 = jnp.where(lane < N_EXPERTS, logits, neg)
    m1 = jnp.max(lg1, axis=-1, keepdims=True)
    i1 = jnp.min(jnp.where(lg1 == m1, lane_f, float(LANES)), axis=-1, keepdims=True)
    oh1 = lane_f == i1
    lg2 = jnp.where(oh1, neg, lg1)
    m2 = jnp.max(lg2, axis=-1, keepdims=True)
    i2 = jnp.min(jnp.where(lg2 == m2, lane_f, float(LANES)), axis=-1, keepdims=True)
    oh2 = lane_f == i2
    e2 = jnp.exp(m2 - m1)
    g1 = 1.0 / (1.0 + e2)
    g2 = e2 * g1

    both = jnp.where(oh1, 1.0, 0.0) + jnp.where(oh2, 1.0, 0.0)
    ri = lax.broadcasted_iota(I32, (tm, tm), 0)
    ci = lax.broadcasted_iota(I32, (tm, tm), 1)
    lstrict = jnp.where(ri > ci, 1.0, 0.0).astype(BF16)
    before = jnp.dot(lstrict, both.astype(BF16), preferred_element_type=F32) + run_ref[...]
    r1 = jnp.sum(jnp.where(oh1, before, 0.0), axis=-1, keepdims=True)
    r2 = jnp.sum(jnp.where(oh2, before, 0.0), axis=-1, keepdims=True)
    run_new = run_ref[...] + jnp.sum(both, axis=0, keepdims=True)
    run_ref[...] = run_new
    cnt_ref[...] = run_new

    out = jnp.where(lane == 0, i1, 0.0)
    out = jnp.where(lane == 1, i2, out)
    out = jnp.where(lane == 2, g1, out)
    out = jnp.where(lane == 3, g2, out)
    out = jnp.where(lane == 4, r1, out)
    out = jnp.where(lane == 5, r2, out)
    route_ref[...] = out


def _router(h, g, w_pad, *, tm=512):
    n_tok, d = h.shape
    return pl.pallas_call(
        _router_body,
        grid=(n_tok // tm,),
        in_specs=[pl.BlockSpec((tm, d), lambda i: (i, 0)),
                  pl.BlockSpec((1, d), lambda i: (0, 0)),
                  pl.BlockSpec((d, LANES), lambda i: (0, 0))],
        out_specs=[pl.BlockSpec((tm, LANES), lambda i: (i, 0)),
                   pl.BlockSpec((1, LANES), lambda i: (0, 0))],
        out_shape=[jax.ShapeDtypeStruct((n_tok, LANES), F32),
                   jax.ShapeDtypeStruct((1, LANES), F32)],
        scratch_shapes=[pltpu.VMEM((1, LANES), F32)],
        compiler_params=_cparams(1),
        name="moe_router",
    )(h, g.reshape(1, d), w_pad)


def _dispatch_body(info_ref, h_ref, g_ref, pos_ref, xs_ref, nbuf, zbuf, idx, sem, isem, *, tm):
    i = pl.program_id(0)
    tg = zbuf.shape[0]

    def zero_copy(e):
        return pltpu.make_async_copy(zbuf, xs_ref.at[pl.ds(pl.multiple_of(info_ref[e], tg), tg)], sem)

    @pl.when(i == 0)
    def _():
        zbuf[...] = jnp.zeros_like(zbuf)
        for e in range(2 * N_EXPERTS):
            @pl.when(info_ref[e] >= 0)
            def _():
                zero_copy(e).start()
        for e in range(2 * N_EXPERTS):
            @pl.when(info_ref[e] >= 0)
            def _():
                zero_copy(e).wait()

    idx_copy = pltpu.make_async_copy(pos_ref.at[i], idx, isem)
    idx_copy.start()
    nbuf[...] = _rms(h_ref[...], g_ref[...])
    idx_copy.wait()

    def row_copy(t, p):
        return pltpu.make_async_copy(nbuf.at[pl.ds(t, 1)], xs_ref.at[pl.ds(p, 1)], sem)

    def issue(t, carry):
        row_copy(t, idx[t]).start()
        row_copy(t, idx[tm + t]).start()
        return carry

    lax.fori_loop(0, tm, issue, 0)
    for _ in range(2):
        pltpu.make_async_copy(nbuf, xs_ref.at[pl.ds(0, tm)], sem).wait()


def _dispatch(h, g, pos_tiles, info, rows_pad, *, tm, tg):
    n_tok, d = h.shape
    return pl.pallas_call(
        functools.partial(_dispatch_body, tm=tm),
        grid_spec=pltpu.PrefetchScalarGridSpec(
            num_scalar_prefetch=1,
            grid=(n_tok // tm,),
            in_specs=[pl.BlockSpec((tm, d), lambda i, info: (i, 0)),
                      pl.BlockSpec((1, d), lambda i, info: (0, 0)),
                      pl.BlockSpec(memory_space=pl.ANY)],
            out_specs=pl.BlockSpec(memory_space=pl.ANY),
            scratch_shapes=[pltpu.VMEM((tm, d), F32),
                            pltpu.VMEM((tg, d), F32),
                            pltpu.SMEM((2 * tm,), I32),
                            pltpu.SemaphoreType.DMA,
                            pltpu.SemaphoreType.DMA]),
        out_shape=jax.ShapeDtypeStruct((rows_pad, d), F32),
        compiler_params=_cparams(1),
        name="moe_dispatch",
    )(info, h, g.reshape(1, d), pos_tiles)


def _gmm_body(te_ref, nu_ref, x_ref, wg_ref, wu_ref, wd_ref, y_ref, *, ff_chunk):
    @pl.when(pl.program_id(0) >= nu_ref[0])
    def _():
        y_ref[...] = jnp.zeros_like(y_ref)

    @pl.when(pl.program_id(0) < nu_ref[0])
    def _():
        x = x_ref[...].astype(BF16)
        acc = None
        for c in range(0, wg_ref.shape[1], ff_chunk):
            a = jnp.dot(x, wg_ref[:, c:c + ff_chunk], preferred_element_type=F32)
            b = jnp.dot(x, wu_ref[:, c:c + ff_chunk], preferred_element_type=F32)
            hid = (_silu(a) * b).astype(BF16)
            part = jnp.dot(hid, wd_ref[c:c + ff_chunk, :], preferred_element_type=F32)
            acc = part if acc is None else acc + part
        y_ref[...] = acc


def _gmm(xs, tile_expert, n_used, wg, wu, wd, *, tm, ff_chunk=896):
    rows_pad, d = xs.shape
    ff = wg.shape[2]
    assert ff % ff_chunk == 0
    tile = lambda i, te, nu: jnp.minimum(i, nu[0] - 1)
    wspec = lambda shape: pl.BlockSpec((None,) + shape, lambda i, te, nu: (te[tile(i, te, nu)], 0, 0),
                                       pipeline_mode=pl.Buffered(1))
    return pl.pallas_call(
        functools.partial(_gmm_body, ff_chunk=ff_chunk),
        grid_spec=pltpu.PrefetchScalarGridSpec(
            num_scalar_prefetch=2,
            grid=(rows_pad // tm,),
            in_specs=[pl.BlockSpec((tm, d), lambda i, te, nu: (tile(i, te, nu), 0)),
                      wspec((d, ff)), wspec((d, ff)), wspec((ff, d))],
            out_specs=pl.BlockSpec((tm, d), lambda i, te, nu: (i, 0))),
        out_shape=jax.ShapeDtypeStruct((rows_pad, d), F32),
        compiler_params=_cparams(1),
        name="moe_experts",
    )(tile_expert, n_used, xs, wg, wu, wd)


def _combine_body(h_ref, route_ref, fg_ref, pos_ref, ys_ref, o_ref, gbuf, idx, sem, isem, *,
                  tm, final_norm):
    i = pl.program_id(0)
    idx_copy = pltpu.make_async_copy(pos_ref.at[i], idx, isem)
    idx_copy.start()
    idx_copy.wait()

    def row_copy(s, t, p):
        return pltpu.make_async_copy(ys_ref.at[pl.ds(p, 1)], gbuf.at[s, pl.ds(t, 1)], sem)

    def issue(t, carry):
        row_copy(0, t, idx[t]).start()
        row_copy(1, t, idx[tm + t]).start()
        return carry

    lax.fori_loop(0, tm, issue, 0)
    for s in range(2):
        pltpu.make_async_copy(ys_ref.at[pl.ds(0, tm)], gbuf.at[s], sem).wait()

    route = route_ref[...]
    out = h_ref[...] + route[:, 2:3] * gbuf[0] + route[:, 3:4] * gbuf[1]
    if final_norm:
        out = _rms(out, fg_ref[...])
    o_ref[...] = out


def _combine(h, route, final_g, pos_tiles, ys, *, tm, final_norm):
    n_tok, d = h.shape
    return pl.pallas_call(
        functools.partial(_combine_body, tm=tm, final_norm=final_norm),
        grid=(n_tok // tm,),
        in_specs=[pl.BlockSpec((tm, d), lambda i: (i, 0)),
                  pl.BlockSpec((tm, LANES), lambda i: (i, 0)),
                  pl.BlockSpec((1, d), lambda i: (0, 0)),
                  pl.BlockSpec(memory_space=pl.ANY),
                  pl.BlockSpec(memory_space=pl.ANY)],
        out_specs=pl.BlockSpec((tm, d), lambda i: (i, 0)),
        out_shape=jax.ShapeDtypeStruct((n_tok, d), F32),
        scratch_shapes=[pltpu.VMEM((2, tm, d), F32),
                        pltpu.SMEM((2 * tm,), I32),
                        pltpu.SemaphoreType.DMA,
                        pltpu.SemaphoreType.DMA],
        compiler_params=_cparams(1),
        name="moe_combine",
    )(h, route, final_g.reshape(1, d), pos_tiles, ys)


def _moe(h, norm_g, w_router, wg, wu, wd, final_g, *, final_norm, tm_route=512, tm_gmm=512,
         tm_rows=256):
    n_tok, d = h.shape
    w_pad = jnp.zeros((d, LANES), F32).at[:, :N_EXPERTS].set(w_router.astype(F32))
    route, counts = _router(h, norm_g, w_pad, tm=tm_route)

    cnt = counts[0, :N_EXPERTS].astype(I32)
    gsz = ((cnt + tm_gmm - 1) // tm_gmm) * tm_gmm
    ends = jnp.cumsum(gsz)
    offs = ends - gsz
    rows_pad = (n_tok * 2 // tm_gmm + N_EXPERTS) * tm_gmm
    n_tiles = rows_pad // tm_gmm
    tile_expert = jnp.minimum(
        jnp.searchsorted(ends, jnp.arange(n_tiles, dtype=I32) * tm_gmm, side="right"),
        N_EXPERTS - 1).astype(I32)
    n_used = (ends[-1:] // tm_gmm).astype(I32)
    tail = ends[-1] + jnp.arange(N_EXPERTS, dtype=I32) * tm_gmm
    info = jnp.concatenate([jnp.where(gsz > 0, ends - tm_gmm, -1),
                            jnp.where(tail < rows_pad, tail, -1)]).astype(I32)

    e01 = route[:, 0:2].astype(I32)
    pos = offs[e01] + route[:, 4:6].astype(I32)
    pos_tiles = pos.reshape(n_tok // tm_rows, tm_rows, 2).transpose(0, 2, 1).reshape(
        n_tok // tm_rows, 2 * tm_rows)

    xs = _dispatch(h, norm_g, pos_tiles, info, rows_pad, tm=tm_rows, tg=tm_gmm)
    ys = _gmm(xs, tile_expert, n_used, wg, wu, wd, tm=tm_gmm)
    return _combine(h, route, final_g, pos_tiles, ys, tm=tm_rows, final_norm=final_norm)


def kernel(x, positions, norm_mix_g, norm_ffn_g, final_norm_g, ab_w_in, gla_w_a2, gla_b_a2,
           ret_norm_g, gla_norm_g, ab_w_out, ffn_w_gate, ffn_w_up, ffn_w_down,
           hgrn_lb_logits, c_w_in, hgrn_norm_g, c_w_out, moe_router, moe_w_gate,
           moe_w_up, moe_w_down):
    batch, seq, d = x.shape
    depth = norm_mix_g.shape[0]
    n_tok = batch * seq
    ret_heads = 4
    gla_heads = 4
    gla_rank = gla_w_a2.shape[1]
    hgrn_heads = d // LANES
    ab_cols = ab_w_in.shape[2] - gla_rank

    lb_cum = jnp.cumsum(jax.nn.softmax(hgrn_lb_logits.astype(F32), axis=0), axis=0)
    lower_bounds = lb_cum - lb_cum[0:1]

    wsum = jnp.asarray(_decay_sum_matrix(), BF16)
    inv_freq = ROPE_BASE ** (-jnp.arange(0, LANES, 2, dtype=F32) / LANES)
    inv_freq2 = jnp.concatenate([inv_freq, inv_freq]).reshape(1, LANES)
    pos3 = positions.reshape(batch, seq, 1)

    h = x.reshape(n_tok, d)
    for l in range(depth):
        e = l // 2
        if l % 2 == 0:
            w_in = jnp.pad(ab_w_in[e], ((0, 0), (0, LANES - gla_rank))).astype(BF16)
            z = _norm_proj(h, norm_mix_g[l], w_in)
            oa = _retention(z, pos3, inv_freq2, ret_norm_g[e], batch=batch, seq=seq,
                            n_heads=ret_heads, col0=0)
            wa_pad = jnp.pad(gla_w_a2[e], ((0, LANES - gla_rank), (0, 0))).astype(BF16)
            ob = _gla(z, wsum, wa_pad, gla_b_a2[e], gla_norm_g[e], batch=batch, seq=seq,
                      n_heads=gla_heads, col0=4 * ret_heads * LANES, col_a=ab_cols)
            w_out = ab_w_out[e].astype(BF16)
            half = ret_heads * HEAD_DV
            h = _proj_residual(h, [oa, ob], [w_out[:half], w_out[half:]])
            h = _ffn(h, norm_ffn_g[l], ffn_w_gate[e].astype(BF16), ffn_w_up[e].astype(BF16),
                     ffn_w_down[e].astype(BF16))
        else:
            z = _norm_proj(h, norm_mix_g[l], c_w_in[e].astype(BF16))
            o = _hgrn(z, wsum, lower_bounds[l], hgrn_norm_g[e], batch=batch, seq=seq,
                      n_heads=hgrn_heads)
            h = _proj_residual(h, [o], [c_w_out[e].astype(BF16)])
            h = _moe(h, norm_ffn_g[l], moe_router[e], moe_w_gate[e].astype(BF16),
                     moe_w_up[e].astype(BF16), moe_w_down[e].astype(BF16), final_norm_g,
                     final_norm=(l == depth - 1))
    if depth % 2 == 1:
        raise NotImplementedError("final norm is fused into the last (odd) layer")
    return h.reshape(batch, seq, d)
```

```python
import functools
import math

import numpy as np
import jax
import jax.numpy as jnp
from jax import lax
from jax.experimental import pallas as pl
from jax.experimental.pallas import tpu as pltpu

F32 = jnp.float32
BF16 = jnp.bfloat16
I32 = jnp.int32

EPS = 1e-6
LANES = 128
MXU_DIM = 256
CHUNK = 64
N_LEVELS = 6
HEAD_DV = 128
ROPE_BASE = 10000.0
GLA_TAU = 16.0
LOG2_E = 1.4426950408889634
N_EXPERTS = 8
ROUTE_SUB = 128
VMEM_LIMIT = 56 * 1024 * 1024

_NT = (((1,), (1,)), ((), ()))
_TN = (((0,), (0,)), ((), ()))


def _cparams(n_axes):
    return pltpu.CompilerParams(dimension_semantics=("arbitrary",) * n_axes,
                                vmem_limit_bytes=VMEM_LIMIT)


def _rms(x, g):
    ms = jnp.mean(x * x, axis=-1, keepdims=True)
    return x * lax.rsqrt(ms + EPS) * g


def _silu(x):
    return x * _sigmoid(x)


def _sigmoid(x):
    e = jnp.exp(-jnp.abs(x))
    r = 1.0 / (1.0 + e)
    return jnp.where(x >= 0, r, e * r)


def _bdot(a, b):
    return jnp.dot(a.astype(BF16), b.astype(BF16), preferred_element_type=F32)


def _bdot_nt(a, b):
    return lax.dot_general(a.astype(BF16), b.astype(BF16), _NT, preferred_element_type=F32)


def _bdot_tn(a, b):
    return lax.dot_general(a.astype(BF16), b.astype(BF16), _TN, preferred_element_type=F32)


def _norm_proj_body(h_ref, g_ref, w_ref, z_ref, *, col_chunk):
    n = _rms(h_ref[...], g_ref[...]).astype(BF16)
    m = w_ref.shape[1]
    for c in range(0, m, col_chunk):
        ce = min(c + col_chunk, m)
        z_ref[:, c:ce] = jnp.dot(n, w_ref[:, c:ce], preferred_element_type=F32).astype(z_ref.dtype)


def _norm_proj(h, g, w, *, tm=512, col_chunk=512):
    n_tok, d = h.shape
    m = w.shape[1]
    assert n_tok % tm == 0
    return pl.pallas_call(
        functools.partial(_norm_proj_body, col_chunk=col_chunk),
        grid=(n_tok // tm,),
        in_specs=[pl.BlockSpec((tm, d), lambda i: (i, 0)),
                  pl.BlockSpec((1, d), lambda i: (0, 0)),
                  pl.BlockSpec((d, m), lambda i: (0, 0), pipeline_mode=pl.Buffered(1))],
        out_specs=pl.BlockSpec((tm, m), lambda i: (i, 0)),
        out_shape=jax.ShapeDtypeStruct((n_tok, m), BF16),
        compiler_params=_cparams(1),
        name="norm_proj",
    )(h, g.reshape(1, d), w)


def _ffn_body(*refs, n_in, ff_chunk):
    h_ref = refs[0]
    x_refs = refs[1:1 + n_in]
    w_refs = refs[1 + n_in:1 + 2 * n_in]
    g_ref, wg_ref, wu_ref, wd_ref, o_ref = refs[1 + 2 * n_in:]
    h = h_ref[...]
    for x_ref, w_ref in zip(x_refs, w_refs):
        h = h + jnp.dot(x_ref[...], w_ref[...], preferred_element_type=F32)
    o_ref[...] = h
    n = _rms(h, g_ref[...]).astype(BF16)
    acc = None
    ff = wg_ref.shape[1]
    for c in range(0, ff, ff_chunk):
        ce = min(c + ff_chunk, ff)
        a = jnp.dot(n, wg_ref[:, c:ce], preferred_element_type=F32)
        b = jnp.dot(n, wu_ref[:, c:ce], preferred_element_type=F32)
        hid = (_silu(a) * b).astype(BF16)
        part = jnp.dot(hid, wd_ref[c:ce, :], preferred_element_type=F32)
        acc = part if acc is None else acc + part
    o_ref[...] = o_ref[...] + acc


def _ffn(h, xs, ws, g, wg, wu, wd, *, tm=512, ff_chunk=6 * MXU_DIM):
    n_tok, d = h.shape
    ff = wg.shape[1]
    assert ff % MXU_DIM == 0
    const = lambda shape: pl.BlockSpec(shape, lambda i: (0, 0), pipeline_mode=pl.Buffered(1))
    in_specs = [pl.BlockSpec((tm, d), lambda i: (i, 0))]
    in_specs += [pl.BlockSpec((tm, x.shape[1]), lambda i: (i, 0)) for x in xs]
    in_specs += [const(w.shape) for w in ws]
    in_specs += [pl.BlockSpec((1, d), lambda i: (0, 0)), const((d, ff)), const((d, ff)), const((ff, d))]
    return pl.pallas_call(
        functools.partial(_ffn_body, n_in=len(xs), ff_chunk=ff_chunk),
        grid=(n_tok // tm,),
        in_specs=in_specs,
        out_specs=pl.BlockSpec((tm, d), lambda i: (i, 0)),
        out_shape=jax.ShapeDtypeStruct((n_tok, d), F32),
        compiler_params=_cparams(1),
        name="ffn_swiglu",
    )(h, *xs, *ws, g.reshape(1, d), wg, wu, wd)


def _decay_sum_matrix():
    c = CHUNK
    r = np.arange(c)[:, None]
    t = np.arange(c)[None, :]
    blocks = [(t <= r), (t > r)]
    for lvl in range(N_LEVELS):
        s = c >> (lvl + 1)
        m = (r // (2 * s)) * (2 * s) + s - 1
        upper = r > m
        blocks.append(np.where(upper, (t > m) & (t <= r), (t > r) & (t <= m)))
    w = np.concatenate(blocks, axis=0).astype(np.float32)
    return np.concatenate([w, w], axis=1)


def _level_masks():
    c = CHUNK
    row = lax.broadcasted_iota(I32, (c, LANES), 0)
    ri = lax.broadcasted_iota(I32, (c, c), 0)
    ci = lax.broadcasted_iota(I32, (c, c), 1)
    uppers, pairs = [], []
    for lvl in range(N_LEVELS):
        s = c >> (lvl + 1)
        uppers.append((row // s) % 2 == 1)
        pairs.append(((ri // (2 * s)) == (ci // (2 * s)))
                     & ((ri // s) % 2 == 1) & ((ci // s) % 2 == 0))
    return uppers, pairs, ri == ci


def _gated_chunk(qs, ks, lfs, vss, lmss, sts, wsum2, masks):
    c = CHUNK
    uppers, pairs, eye = masks
    n_g = len(qs)
    pieces = []
    for lf in lfs:
        lf2 = lf * LOG2_E
        hi = lf2.astype(BF16)
        lo = (lf2 - hi.astype(F32)).astype(BF16)
        pieces.append(jnp.concatenate([hi, lo], axis=0))
    ex = jnp.dot(wsum2, jnp.concatenate(pieces, axis=1), preferred_element_type=F32)
    fac = jnp.exp2(ex)
    fac_b = fac.astype(BF16)

    grams, inters, updates, diags = [], [], [], []
    for g in range(n_g):
        cols = slice(g * LANES, (g + 1) * LANES)
        q_b = qs[g].astype(BF16)
        k_b = ks[g].astype(BF16)
        w = [jnp.where(uppers[l], q_b, k_b) * fac_b[(2 + l) * c:(3 + l) * c, cols]
             for l in range(N_LEVELS)]
        qf = q_b * fac_b[0:c, cols]
        kf = k_b * fac_b[c:2 * c, cols]
        st_b = sts[g].astype(BF16)
        qk = qs[g] * ks[g]
        for v, lm in zip(vss[g], lmss[g]):
            if lm is None:
                sel = lambda a: a
            else:
                sel = lambda a, lm=lm: jnp.where(lm, a, jnp.zeros_like(a))
            grams.append([lax.dot_general(sel(w[l]), w[l], _NT, preferred_element_type=F32)
                          for l in range(N_LEVELS)])
            inters.append(lax.dot_general(sel(qf), st_b, _NT, preferred_element_type=F32))
            updates.append(lax.dot_general(v, sel(kf), _TN, preferred_element_type=F32))
            diags.append(jnp.sum(sel(qk), axis=-1, keepdims=True))

    outs, new_sts = [], []
    h = 0
    for g in range(n_g):
        cols = slice(g * LANES, (g + 1) * LANES)
        new_st = sts[g] * fac[c - 1:c, cols]
        outs_g = []
        for v in vss[g]:
            scores = jnp.where(eye, diags[h], 0.0)
            for l in range(N_LEVELS):
                scores = jnp.where(pairs[l], grams[h][l], scores)
            outs_g.append(jnp.dot(scores.astype(BF16), v, preferred_element_type=F32) + inters[h])
            new_st = new_st + updates[h]
            h += 1
        outs.append(outs_g)
        new_sts.append(new_st)
    return outs, new_sts


def _head_out(o, norm_g, gate):
    return (_rms(o, norm_g) * _silu(gate)).astype(BF16)


def _rope_body(pos_ref, invf_ref, cos_ref, sin_ref):
    ang = pos_ref[...].astype(F32) * invf_ref[...]
    lane = lax.broadcasted_iota(I32, (1, LANES), 1)
    cos_ref[...] = jnp.cos(ang)
    sin_ref[...] = jnp.sin(ang) * jnp.where(lane < LANES // 2, -1.0, 1.0)


def _rope_tables(pos_col, inv_freq2, *, tm=2048):
    n_tok = pos_col.shape[0]
    tm = min(tm, n_tok)
    table = jax.ShapeDtypeStruct((n_tok, LANES), F32)
    return pl.pallas_call(
        _rope_body,
        grid=(n_tok // tm,),
        in_specs=[pl.BlockSpec((tm, 1), lambda i: (i, 0)),
                  pl.BlockSpec((1, LANES), lambda i: (0, 0))],
        out_specs=[pl.BlockSpec((tm, LANES), lambda i: (i, 0))] * 2,
        out_shape=[table, table],
        compiler_params=_cparams(1),
        name="rope_tables",
    )(pos_col, inv_freq2)


def _retention_body(cos_ref, sin_ref, ng_ref, q_ref, k_ref, v_ref, gate_ref, o_ref, st_ref, *,
                    n_heads):
    c = CHUNK
    tb = q_ref.shape[0]

    @pl.when(pl.program_id(1) == 0)
    def _():
        st_ref[...] = jnp.zeros_like(st_ref)

    ri = lax.broadcasted_iota(I32, (c, c), 0)
    ci = lax.broadcasted_iota(I32, (c, c), 1)
    rel = (ri - ci).astype(F32)
    trow = lax.broadcasted_iota(I32, (c, LANES), 0).astype(F32)
    k_scale = float(LANES) ** -0.5
    ng = ng_ref[...]
    log_gammas = [math.log1p(-(2.0 ** (-5 - h))) for h in range(n_heads)]
    dmats = [jnp.where(rel >= 0, jnp.exp(rel * lg), 0.0) for lg in log_gammas]
    q_decays = [jnp.exp((trow + 1.0) * lg) for lg in log_gammas]
    k_decays = [jnp.exp((c - 1.0 - trow) * lg) * k_scale for lg in log_gammas]
    hcols = [slice(h * LANES, (h + 1) * LANES) for h in range(n_heads)]

    def chunk(ic, carry):
        r0 = pl.multiple_of(ic * c, c)
        rows = pl.ds(r0, c)
        cosv = cos_ref[rows, :]
        sinv = sin_ref[rows, :]
        scores, inters, updates, vs = [], [], [], []
        for h, cols in enumerate(hcols):
            q = q_ref[rows, cols].astype(F32)
            k = k_ref[rows, cols].astype(F32)
            v = v_ref[rows, cols]
            qr = q * cosv + pltpu.roll(q, LANES // 2, 1) * sinv
            kr = k * cosv + pltpu.roll(k, LANES // 2, 1) * sinv
            scores.append(_bdot_nt(qr, kr * k_scale))
            inters.append(_bdot_nt(qr * q_decays[h], st_ref[h]))
            updates.append(lax.dot_general(v, (kr * k_decays[h]).astype(BF16), _TN,
                                           preferred_element_type=F32))
            vs.append(v)
        for h, cols in enumerate(hcols):
            o = jnp.dot((scores[h] * dmats[h]).astype(BF16), vs[h],
                        preferred_element_type=F32) + inters[h]
            st_ref[h] = math.exp(c * log_gammas[h]) * st_ref[h] + updates[h]
            o_ref[rows, cols] = _head_out(o, ng, gate_ref[rows, cols].astype(F32))
        return carry

    lax.fori_loop(0, tb // c, chunk, 0, unroll=2)


def _retention(z, cos_t, sin_t, norm_g, *, batch, seq, n_heads, col0, tb=512):
    n_tok = z.shape[0]
    w = n_heads * LANES
    assert seq % tb == 0 and tb % CHUNK == 0
    nt = seq // tb
    cb = col0 // w
    zspec = lambda j: pl.BlockSpec((tb, w), lambda b, t, j=j: (b * nt + t, cb + j))
    tspec = pl.BlockSpec((tb, LANES), lambda b, t: (b * nt + t, 0))
    return pl.pallas_call(
        functools.partial(_retention_body, n_heads=n_heads),
        grid=(batch, nt),
        in_specs=[tspec, tspec,
                  pl.BlockSpec((1, LANES), lambda b, t: (0, 0)),
                  zspec(0), zspec(1), zspec(2), zspec(3)],
        out_specs=pl.BlockSpec((tb, w), lambda b, t: (b * nt + t, 0)),
        out_shape=jax.ShapeDtypeStruct((n_tok, w), BF16),
        scratch_shapes=[pltpu.VMEM((n_heads, HEAD_DV, LANES), F32)],
        compiler_params=_cparams(2),
        name="retention",
    )(cos_t, sin_t, norm_g.reshape(1, LANES), z, z, z, z)


def _gla_body(wsum_ref, wa_ref, ba2_ref, ng_ref, a_ref, q_ref, k_ref, v_ref, gate_ref, o_ref,
              st_ref, lf_ref, *, n_groups):
    c = CHUNK
    tb = q_ref.shape[0]

    @pl.when(pl.program_id(1) == 0)
    def _():
        st_ref[...] = jnp.zeros_like(st_ref)

    x = jnp.dot(a_ref[...], wa_ref[...], preferred_element_type=F32) + ba2_ref[...]
    lf_ref[...] = (jnp.minimum(x, 0.0) - jnp.log1p(jnp.exp(-jnp.abs(x)))) * (1.0 / GLA_TAU)

    masks = _level_masks()
    lane = lax.broadcasted_iota(I32, (1, LANES), 1)
    lane_masks = [lane < LANES // 2, lane >= LANES // 2]
    wsum = wsum_ref[...]
    ng = ng_ref[...]
    q_scale = float(LANES // 2) ** -0.5

    def chunk(ic, carry):
        r0 = pl.multiple_of(ic * c, c)
        rows = pl.ds(r0, c)
        kcols = [slice(g * LANES, (g + 1) * LANES) for g in range(n_groups)]
        vcols = [[slice((2 * g + j) * HEAD_DV, (2 * g + j + 1) * HEAD_DV) for j in range(2)]
                 for g in range(n_groups)]
        outs, new_sts = _gated_chunk(
            [q_ref[rows, kc].astype(F32) * q_scale for kc in kcols],
            [k_ref[rows, kc].astype(F32) for kc in kcols],
            [lf_ref[rows, kc] for kc in kcols],
            [[v_ref[rows, vc] for vc in vcs] for vcs in vcols],
            [lane_masks] * n_groups,
            [st_ref[g] for g in range(n_groups)],
            wsum, masks)
        for g in range(n_groups):
            st_ref[g] = new_sts[g]
            for o, vc in zip(outs[g], vcols[g]):
                o_ref[rows, vc] = _head_out(o, ng, gate_ref[rows, vc].astype(F32))
        return carry

    lax.fori_loop(0, tb // c, chunk, 0, unroll=2)


def _gla(z, wsum, wa_pad, b_a2, norm_g, *, batch, seq, n_heads, col0, col_a, tb=512):
    n_tok = z.shape[0]
    wk = n_heads * (LANES // 2)
    wv = n_heads * HEAD_DV
    assert seq % tb == 0 and tb % CHUNK == 0
    nt = seq // tb
    n_groups = n_heads // 2
    row = lambda b, t: b * nt + t
    const = lambda shape: pl.BlockSpec(shape, lambda b, t: (0, 0))
    return pl.pallas_call(
        functools.partial(_gla_body, n_groups=n_groups),
        grid=(batch, nt),
        in_specs=[const(wsum.shape), const(wa_pad.shape), const((1, wk)), const((1, LANES)),
                  pl.BlockSpec((tb, LANES), lambda b, t: (row(b, t), col_a // LANES)),
                  pl.BlockSpec((tb, wk), lambda b, t: (row(b, t), col0 // wk)),
                  pl.BlockSpec((tb, wk), lambda b, t: (row(b, t), col0 // wk + 1)),
                  pl.BlockSpec((tb, wv), lambda b, t: (row(b, t), (col0 + 2 * wk) // wv)),
                  pl.BlockSpec((tb, wv), lambda b, t: (row(b, t), (col0 + 2 * wk) // wv + 1))],
        out_specs=pl.BlockSpec((tb, wv), lambda b, t: (row(b, t), 0)),
        out_shape=jax.ShapeDtypeStruct((n_tok, wv), BF16),
        scratch_shapes=[pltpu.VMEM((n_groups, HEAD_DV, LANES), F32),
                        pltpu.VMEM((tb, wk), F32)],
        compiler_params=_cparams(2),
        name="gla",
    )(wsum, wa_pad, b_a2.reshape(1, wk), norm_g.reshape(1, LANES), z, z, z, z, z)


def _hgrn_body(wsum_ref, lb_ref, ng_ref, q_ref, f_ref, i_ref, gate_ref, o_ref, st_ref, *, n_heads):
    c = CHUNK
    tb = q_ref.shape[0]

    @pl.when(pl.program_id(2) == 0)
    def _():
        st_ref[...] = jnp.zeros_like(st_ref)

    masks = _level_masks()
    wsum = wsum_ref[...]
    ng = ng_ref[...]
    q_scale = float(LANES) ** -0.5

    def chunk(ic, carry):
        r0 = pl.multiple_of(ic * c, c)
        rows = pl.ds(r0, c)
        hcols = [slice(h * LANES, (h + 1) * LANES) for h in range(n_heads)]
        qs, ks, lfs = [], [], []
        for cols in hcols:
            lb = lb_ref[:, cols]
            f = f_ref[rows, cols].astype(F32)
            e = jnp.exp(-jnp.abs(f))
            r = 1.0 / (1.0 + e)
            sig_pos = jnp.where(f >= 0, r, e * r)
            sig_neg = jnp.where(f >= 0, e * r, r)
            lfs.append(jnp.log(lb + (1.0 - lb) * sig_pos))
            ks.append((1.0 - lb) * sig_neg)
            qs.append(_silu(q_ref[rows, cols].astype(F32)) * q_scale)
        outs, new_sts = _gated_chunk(
            qs, ks, lfs, [[i_ref[rows, cols]] for cols in hcols], [[None]] * n_heads,
            [st_ref[h] for h in range(n_heads)], wsum, masks)
        for h, cols in enumerate(hcols):
            st_ref[h] = new_sts[h]
            o_ref[rows, cols] = _head_out(outs[h][0], ng, gate_ref[rows, cols].astype(F32))
        return carry

    lax.fori_loop(0, tb // c, chunk, 0, unroll=2)


def _hgrn(z, wsum, lb, norm_g, *, batch, seq, n_heads, heads_per_step=8, tb=512):
    n_tok = z.shape[0]
    w = heads_per_step * LANES
    ng_ = n_heads // heads_per_step
    assert seq % tb == 0 and tb % CHUNK == 0
    nt = seq // tb
    zspec = lambda j: pl.BlockSpec((tb, w), lambda b, g, t, j=j: (b * nt + t, j * ng_ + g))
    return pl.pallas_call(
        functools.partial(_hgrn_body, n_heads=heads_per_step),
        grid=(batch, ng_, nt),
        in_specs=[pl.BlockSpec(wsum.shape, lambda b, g, t: (0, 0)),
                  pl.BlockSpec((1, w), lambda b, g, t: (0, g)),
                  pl.BlockSpec((1, LANES), lambda b, g, t: (0, 0)),
                  zspec(0), zspec(1), zspec(2), zspec(3)],
        out_specs=pl.BlockSpec((tb, w), lambda b, g, t: (b * nt + t, g)),
        out_shape=jax.ShapeDtypeStruct((n_tok, n_heads * LANES), BF16),
        scratch_shapes=[pltpu.VMEM((heads_per_step, HEAD_DV, LANES), F32)],
        compiler_params=_cparams(3),
        name="hgrn2",
    )(wsum, lb.reshape(1, n_heads * LANES), norm_g.reshape(1, LANES), z, z, z, z)


def _router_body(h_ref, x_ref, wo_ref, g_ref, w_ref, h1_ref, route_ref, cnt_ref, run_ref):
    tm = h_ref.shape[0]

    @pl.when(pl.program_id(0) == 0)
    def _():
        run_ref[...] = jnp.zeros_like(run_ref)

    ts = ROUTE_SUB
    w = w_ref[...]
    w_hi = w.astype(BF16)
    w_lo = (w - w_hi.astype(F32)).astype(BF16)
    lane = lax.broadcasted_iota(I32, (ts, LANES), 1)
    lane_f = lane.astype(F32)
    ri = lax.broadcasted_iota(I32, (ts, ts), 0)
    ci = lax.broadcasted_iota(I32, (ts, ts), 1)
    lstrict = jnp.where(ri > ci, 1.0, 0.0).astype(BF16)
    neg = -jnp.inf
    subs = [slice(r0, r0 + ts) for r0 in range(0, tm, ts)]
    h1s = [h_ref[rows, :] + jnp.dot(x_ref[rows, :], wo_ref[...], preferred_element_type=F32)
           for rows in subs]
    logits = []
    for rows, h1 in zip(subs, h1s):
        h1_ref[rows, :] = h1
        n = _rms(h1, g_ref[...])
        n_hi = n.astype(BF16)
        n_lo = (n - n_hi.astype(F32)).astype(BF16)
        logits.append(jnp.dot(n_hi, w_hi, preferred_element_type=F32)
                      + jnp.dot(n_hi, w_lo, preferred_element_type=F32)
                      + jnp.dot(n_lo, w_hi, preferred_element_type=F32))
    picks = []
    for lg in logits:
        lg1 = jnp.where(lane < N_EXPERTS, lg, neg)
        m1 = jnp.max(lg1, axis=-1, keepdims=True)
        i1 = jnp.min(jnp.where(lg1 == m1, lane_f, float(LANES)), axis=-1, keepdims=True)
        oh1 = lane_f == i1
        lg2 = jnp.where(oh1, neg, lg1)
        m2 = jnp.max(lg2, axis=-1, keepdims=True)
        i2 = jnp.min(jnp.where(lg2 == m2, lane_f, float(LANES)), axis=-1, keepdims=True)
        oh2 = lane_f == i2
        e2 = jnp.exp(m2 - m1)
        g1 = 1.0 / (1.0 + e2)
        both = jnp.where(oh1, 1.0, 0.0) + jnp.where(oh2, 1.0, 0.0)
        prefix = jnp.dot(lstrict, both.astype(BF16), preferred_element_type=F32)
        picks.append((i1, i2, g1, e2 * g1, oh1, oh2, both, prefix))
    run = run_ref[...]
    for rows, (i1, i2, g1, g2, oh1, oh2, both, prefix) in zip(subs, picks):
        before = prefix + run
        r1 = jnp.sum(jnp.where(oh1, before, 0.0), axis=-1, keepdims=True)
        r2 = jnp.sum(jnp.where(oh2, before, 0.0), axis=-1, keepdims=True)
        run = run + jnp.sum(both, axis=0, keepdims=True)
        out = jnp.where(lane == 0, i1, 0.0)
        out = jnp.where(lane == 1, i2, out)
        out = jnp.where(lane == 2, g1, out)
        out = jnp.where(lane == 3, g2, out)
        out = jnp.where(lane == 4, r1, out)
        out = jnp.where(lane == 5, r2, out)
        route_ref[rows, :] = out
    run_ref[...] = run
    cnt_ref[...] = run


def _router(h, x, w_out, g, w_pad, *, tm=512):
    n_tok, d = h.shape
    return pl.pallas_call(
        _router_body,
        grid=(n_tok // tm,),
        in_specs=[pl.BlockSpec((tm, d), lambda i: (i, 0)),
                  pl.BlockSpec((tm, x.shape[1]), lambda i: (i, 0)),
                  pl.BlockSpec(w_out.shape, lambda i: (0, 0), pipeline_mode=pl.Buffered(1)),
                  pl.BlockSpec((1, d), lambda i: (0, 0)),
                  pl.BlockSpec((d, LANES), lambda i: (0, 0))],
        out_specs=[pl.BlockSpec((tm, d), lambda i: (i, 0)),
                   pl.BlockSpec((tm, LANES), lambda i: (i, 0)),
                   pl.BlockSpec((1, LANES), lambda i: (0, 0))],
        out_shape=[jax.ShapeDtypeStruct((n_tok, d), F32),
                   jax.ShapeDtypeStruct((n_tok, LANES), F32),
                   jax.ShapeDtypeStruct((1, LANES), F32)],
        scratch_shapes=[pltpu.VMEM((1, LANES), F32)],
        compiler_params=_cparams(1),
        name="moe_router",
    )(h, x, w_out, g.reshape(1, d), w_pad)


def _dispatch_body(info_ref, h_ref, g_ref, pos_ref, xs_ref, nbuf, zbuf, idx, sems, isem, *, tm):
    i = pl.program_id(0)
    last = pl.num_programs(0) - 1
    slot = i % 2
    tg = zbuf.shape[0]

    def zero_copy(e):
        return pltpu.make_async_copy(
            zbuf, xs_ref.at[pl.ds(pl.multiple_of(info_ref[e], tg), tg)], sems.at[0])

    @pl.when(i == 0)
    def _():
        zbuf[...] = jnp.zeros_like(zbuf)
        for e in range(2 * N_EXPERTS):
            @pl.when(info_ref[e] >= 0)
            def _():
                zero_copy(e).start()
        for e in range(2 * N_EXPERTS):
            @pl.when(info_ref[e] >= 0)
            def _():
                zero_copy(e).wait()

    def drain(s):
        for _ in range(2):
            pltpu.make_async_copy(nbuf.at[s], xs_ref.at[pl.ds(0, tm)], sems.at[s]).wait()

    @pl.when(i >= 2)
    def _():
        drain(slot)

    idx_copy = pltpu.make_async_copy(pos_ref.at[i], idx, isem)
    idx_copy.start()
    nbuf[slot] = _rms(h_ref[...], g_ref[...])
    idx_copy.wait()

    def issue(t, carry):
        for k in range(2):
            pltpu.make_async_copy(nbuf.at[slot, pl.ds(t, 1)],
                                  xs_ref.at[pl.ds(idx[k * tm + t], 1)], sems.at[slot]).start()
        return carry

    lax.fori_loop(0, tm, issue, 0, unroll=8)

    @pl.when(i == last)
    def _():
        drain(slot)

        @pl.when(i >= 1)
        def _():
            drain(1 - slot)


def _dispatch(h, g, pos_tiles, info, rows_pad, *, tm, tg):
    n_tok, d = h.shape
    return pl.pallas_call(
        functools.partial(_dispatch_body, tm=tm),
        grid_spec=pltpu.PrefetchScalarGridSpec(
            num_scalar_prefetch=1,
            grid=(n_tok // tm,),
            in_specs=[pl.BlockSpec((tm, d), lambda i, info: (i, 0)),
                      pl.BlockSpec((1, d), lambda i, info: (0, 0)),
                      pl.BlockSpec(memory_space=pl.ANY)],
            out_specs=pl.BlockSpec(memory_space=pl.ANY),
            scratch_shapes=[pltpu.VMEM((2, tm, d), F32),
                            pltpu.VMEM((tg, d), F32),
                            pltpu.SMEM((2 * tm,), I32),
                            pltpu.SemaphoreType.DMA((2,)),
                            pltpu.SemaphoreType.DMA]),
        out_shape=jax.ShapeDtypeStruct((rows_pad, d), F32),
        compiler_params=_cparams(1),
        name="moe_dispatch",
    )(info, h, g.reshape(1, d), pos_tiles)


def _gmm_body(te_ref, nu_ref, x_ref, wg_ref, wu_ref, wd_ref, y_ref, *, ff_chunk):
    @pl.when(pl.program_id(0) >= nu_ref[0])
    def _():
        y_ref[...] = jnp.zeros_like(y_ref)

    @pl.when(pl.program_id(0) < nu_ref[0])
    def _():
        x = x_ref[...].astype(BF16)
        acc = None
        for c in range(0, wg_ref.shape[1], ff_chunk):
            a = jnp.dot(x, wg_ref[:, c:c + ff_chunk], preferred_element_type=F32)
            b = jnp.dot(x, wu_ref[:, c:c + ff_chunk], preferred_element_type=F32)
            hid = (_silu(a) * b).astype(BF16)
            part = jnp.dot(hid, wd_ref[c:c + ff_chunk, :], preferred_element_type=F32)
            acc = part if acc is None else acc + part
        y_ref[...] = acc


def _gmm(xs, tile_expert, n_used, wg, wu, wd, *, tm, ff_chunk=7 * MXU_DIM):
    rows_pad, d = xs.shape
    ff = wg.shape[2]
    assert ff % ff_chunk == 0 and ff_chunk % MXU_DIM == 0
    tile = lambda i, te, nu: jnp.minimum(i, nu[0] - 1)
    wspec = lambda shape: pl.BlockSpec((None,) + shape, lambda i, te, nu: (te[tile(i, te, nu)], 0, 0),
                                       pipeline_mode=pl.Buffered(1))
    return pl.pallas_call(
        functools.partial(_gmm_body, ff_chunk=ff_chunk),
        grid_spec=pltpu.PrefetchScalarGridSpec(
            num_scalar_prefetch=2,
            grid=(rows_pad // tm,),
            in_specs=[pl.BlockSpec((tm, d), lambda i, te, nu: (tile(i, te, nu), 0)),
                      wspec((d, ff)), wspec((d, ff)), wspec((ff, d))],
            out_specs=pl.BlockSpec((tm, d), lambda i, te, nu: (i, 0))),
        out_shape=jax.ShapeDtypeStruct((rows_pad, d), F32),
        compiler_params=_cparams(1),
        name="moe_experts",
    )(tile_expert, n_used, xs, wg, wu, wd)


def _combine_body(h_ref, route_ref, fg_ref, pos_ref, ys_ref, o_ref, gbuf, idx, sems, isem, *,
                  tm, final_norm):
    i = pl.program_id(0)
    slot = i % 2

    def gather(tile, s):
        idx_copy = pltpu.make_async_copy(pos_ref.at[tile], idx, isem)
        idx_copy.start()
        idx_copy.wait()

        def issue(t, carry):
            for k in range(2):
                pltpu.make_async_copy(ys_ref.at[pl.ds(idx[k * tm + t], 1)],
                                      gbuf.at[s, k, pl.ds(t, 1)], sems.at[s]).start()
            return carry

        lax.fori_loop(0, tm, issue, 0, unroll=8)

    @pl.when(i == 0)
    def _():
        gather(0, 0)

    @pl.when(i + 1 < pl.num_programs(0))
    def _():
        gather(i + 1, 1 - slot)

    for k in range(2):
        pltpu.make_async_copy(ys_ref.at[pl.ds(0, tm)], gbuf.at[slot, k], sems.at[slot]).wait()

    route = route_ref[...]
    out = h_ref[...] + route[:, 2:3] * gbuf[slot, 0] + route[:, 3:4] * gbuf[slot, 1]
    if final_norm:
        out = _rms(out, fg_ref[...])
    o_ref[...] = out


def _combine(h, route, final_g, pos_tiles, ys, *, tm, final_norm):
    n_tok, d = h.shape
    return pl.pallas_call(
        functools.partial(_combine_body, tm=tm, final_norm=final_norm),
        grid=(n_tok // tm,),
        in_specs=[pl.BlockSpec((tm, d), lambda i: (i, 0)),
                  pl.BlockSpec((tm, LANES), lambda i: (i, 0)),
                  pl.BlockSpec((1, d), lambda i: (0, 0)),
                  pl.BlockSpec(memory_space=pl.ANY),
                  pl.BlockSpec(memory_space=pl.ANY)],
        out_specs=pl.BlockSpec((tm, d), lambda i: (i, 0)),
        out_shape=jax.ShapeDtypeStruct((n_tok, d), F32),
        scratch_shapes=[pltpu.VMEM((2, 2, tm, d), F32),
                        pltpu.SMEM((2 * tm,), I32),
                        pltpu.SemaphoreType.DMA((2,)),
                        pltpu.SemaphoreType.DMA],
        compiler_params=_cparams(1),
        name="moe_combine",
    )(h, route, final_g.reshape(1, d), pos_tiles, ys)


def _moe(h, x, w_out, norm_g, w_router, wg, wu, wd, final_g, *, final_norm, tm_route=512,
         tm_gmm=512, tm_rows=256):
    n_tok, d = h.shape
    w_pad = jnp.zeros((d, LANES), F32).at[:, :N_EXPERTS].set(w_router.astype(F32))
    h, route, counts = _router(h, x, w_out, norm_g, w_pad, tm=tm_route)

    cnt = counts[0, :N_EXPERTS].astype(I32)
    gsz = ((cnt + tm_gmm - 1) // tm_gmm) * tm_gmm
    ends = jnp.cumsum(gsz)
    offs = ends - gsz
    rows_pad = (n_tok * 2 // tm_gmm + N_EXPERTS) * tm_gmm
    n_tiles = rows_pad // tm_gmm
    tile_expert = jnp.minimum(
        jnp.searchsorted(ends, jnp.arange(n_tiles, dtype=I32) * tm_gmm, side="right"),
        N_EXPERTS - 1).astype(I32)
    n_used = jnp.maximum(ends[-1:] // tm_gmm, 1).astype(I32)
    tail = ends[-1] + jnp.arange(N_EXPERTS, dtype=I32) * tm_gmm
    info = jnp.concatenate([jnp.where(gsz > 0, ends - tm_gmm, -1),
                            jnp.where(tail < rows_pad, tail, -1)]).astype(I32)

    e01 = route[:, 0:2].astype(I32)
    pos = jnp.clip(offs[jnp.clip(e01, 0, N_EXPERTS - 1)] + route[:, 4:6].astype(I32),
                   0, rows_pad - 1)
    pos_tiles = pos.reshape(n_tok // tm_rows, tm_rows, 2).transpose(0, 2, 1).reshape(
        n_tok // tm_rows, 2 * tm_rows)

    xs = _dispatch(h, norm_g, pos_tiles, info, rows_pad, tm=tm_rows, tg=tm_gmm)
    ys = _gmm(xs, tile_expert, n_used, wg, wu, wd, tm=tm_gmm)
    return _combine(h, route, final_g, pos_tiles, ys, tm=tm_rows, final_norm=final_norm)


def kernel(x, positions, norm_mix_g, norm_ffn_g, final_norm_g, ab_w_in, gla_w_a2, gla_b_a2,
           ret_norm_g, gla_norm_g, ab_w_out, ffn_w_gate, ffn_w_up, ffn_w_down,
           hgrn_lb_logits, c_w_in, hgrn_norm_g, c_w_out, moe_router, moe_w_gate,
           moe_w_up, moe_w_down):
    batch, seq, d = x.shape
    depth = norm_mix_g.shape[0]
    n_tok = batch * seq
    ret_heads = 4
    gla_heads = 4
    gla_rank = gla_w_a2.shape[1]
    hgrn_heads = d // LANES
    ab_cols = ab_w_in.shape[2] - gla_rank

    lb_cum = jnp.cumsum(jax.nn.softmax(hgrn_lb_logits.astype(F32), axis=0), axis=0)
    lower_bounds = lb_cum - lb_cum[0:1]

    wsum = jnp.asarray(_decay_sum_matrix(), BF16)
    inv_freq = ROPE_BASE ** (-jnp.arange(0, LANES, 2, dtype=F32) / LANES)
    inv_freq2 = jnp.concatenate([inv_freq, inv_freq]).reshape(1, LANES)
    cos_t, sin_t = _rope_tables(positions.reshape(n_tok, 1), inv_freq2)

    h = x.reshape(n_tok, d)
    for l in range(depth):
        e = l // 2
        if l % 2 == 0:
            w_in = jnp.pad(ab_w_in[e], ((0, 0), (0, LANES - gla_rank))).astype(BF16)
            z = _norm_proj(h, norm_mix_g[l], w_in)
            oa = _retention(z, cos_t, sin_t, ret_norm_g[e], batch=batch, seq=seq,
                            n_heads=ret_heads, col0=0)
            wa_pad = jnp.pad(gla_w_a2[e], ((0, LANES - gla_rank), (0, 0))).astype(BF16)
            ob = _gla(z, wsum, wa_pad, gla_b_a2[e], gla_norm_g[e], batch=batch, seq=seq,
                      n_heads=gla_heads, col0=4 * ret_heads * LANES, col_a=ab_cols)
            w_out = ab_w_out[e].astype(BF16)
            half = ret_heads * HEAD_DV
            h = _ffn(h, [oa, ob], [w_out[:half], w_out[half:]], norm_ffn_g[l],
                     ffn_w_gate[e].astype(BF16), ffn_w_up[e].astype(BF16),
                     ffn_w_down[e].astype(BF16))
        else:
            z = _norm_proj(h, norm_mix_g[l], c_w_in[e].astype(BF16))
            o = _hgrn(z, wsum, lower_bounds[l], hgrn_norm_g[e], batch=batch, seq=seq,
                      n_heads=hgrn_heads)
            h = _moe(h, o, c_w_out[e].astype(BF16), norm_ffn_g[l], moe_router[e],
                     moe_w_gate[e].astype(BF16),
                     moe_w_up[e].astype(BF16), moe_w_down[e].astype(BF16), final_norm_g,
                     final_norm=(l == depth - 1))
    if depth % 2 == 1:
        raise NotImplementedError("final norm is fused into the last (odd) layer")
    return h.reshape(batch, seq, d)
```

```python
import functools
import math

import numpy as np
import jax
import jax.numpy as jnp
from jax import lax
from jax.experimental import pallas as pl
from jax.experimental.pallas import tpu as pltpu

F32 = jnp.float32
BF16 = jnp.bfloat16
I32 = jnp.int32

EPS = 1e-6
LANES = 128
MXU_DIM = 256
CHUNK = 64
N_LEVELS = 6
HEAD_DV = 128
ROPE_BASE = 10000.0
GLA_TAU = 16.0
LOG2_E = 1.4426950408889634
N_EXPERTS = 8
ROUTE_SUB = 128
ROW_UNROLL = 8
VMEM_LIMIT = 56 * 1024 * 1024

_NT = (((1,), (1,)), ((), ()))
_TN = (((0,), (0,)), ((), ()))


def _cparams(n_axes):
    return pltpu.CompilerParams(dimension_semantics=("arbitrary",) * n_axes,
                                vmem_limit_bytes=VMEM_LIMIT)


def _rms(x, g):
    ms = jnp.mean(x * x, axis=-1, keepdims=True)
    return x * lax.rsqrt(ms + EPS) * g


def _silu(x, scale=1.0):
    t = jnp.tanh(0.5 * x)
    hx = x * (0.5 * scale)
    return hx + hx * t


def _bdot_nt(a, b):
    return lax.dot_general(a.astype(BF16), b.astype(BF16), _NT, preferred_element_type=F32)


def _bdot_tn(a, b):
    return lax.dot_general(a.astype(BF16), b.astype(BF16), _TN, preferred_element_type=F32)


def _norm_proj_body(h_ref, g_ref, w_ref, z_ref, *, col_chunk):
    n = _rms(h_ref[...], g_ref[...]).astype(BF16)
    m = w_ref.shape[1]
    for c in range(0, m, col_chunk):
        ce = min(c + col_chunk, m)
        z_ref[:, c:ce] = jnp.dot(n, w_ref[:, c:ce], preferred_element_type=F32).astype(z_ref.dtype)


def _norm_proj(h, g, w, *, tm=512, col_chunk=512):
    n_tok, d = h.shape
    m = w.shape[1]
    assert n_tok % tm == 0
    return pl.pallas_call(
        functools.partial(_norm_proj_body, col_chunk=col_chunk),
        grid=(n_tok // tm,),
        in_specs=[pl.BlockSpec((tm, d), lambda i: (i, 0)),
                  pl.BlockSpec((1, d), lambda i: (0, 0)),
                  pl.BlockSpec((d, m), lambda i: (0, 0), pipeline_mode=pl.Buffered(1))],
        out_specs=pl.BlockSpec((tm, m), lambda i: (i, 0)),
        out_shape=jax.ShapeDtypeStruct((n_tok, m), BF16),
        compiler_params=_cparams(1),
        name="norm_proj",
    )(h, g.reshape(1, d), w)


def _ffn_body(*refs, n_in, ff_chunk):
    h_ref = refs[0]
    x_refs = refs[1:1 + n_in]
    w_refs = refs[1 + n_in:1 + 2 * n_in]
    g_ref, wg_ref, wu_ref, wd_ref, o_ref = refs[1 + 2 * n_in:]
    h = h_ref[...]
    for x_ref, w_ref in zip(x_refs, w_refs):
        h = h + jnp.dot(x_ref[...], w_ref[...], preferred_element_type=F32)
    o_ref[...] = h
    n = _rms(h, g_ref[...]).astype(BF16)
    acc = None
    ff = wg_ref.shape[1]
    for c in range(0, ff, ff_chunk):
        ce = min(c + ff_chunk, ff)
        a = jnp.dot(n, wg_ref[:, c:ce], preferred_element_type=F32)
        b = jnp.dot(n, wu_ref[:, c:ce], preferred_element_type=F32)
        hid = (_silu(a) * b).astype(BF16)
        part = jnp.dot(hid, wd_ref[c:ce, :], preferred_element_type=F32)
        acc = part if acc is None else acc + part
    o_ref[...] = o_ref[...] + acc


def _ffn(h, xs, ws, g, wg, wu, wd, *, tm=512, ff_chunk=6 * MXU_DIM):
    n_tok, d = h.shape
    ff = wg.shape[1]
    assert ff % MXU_DIM == 0
    const = lambda shape: pl.BlockSpec(shape, lambda i: (0, 0), pipeline_mode=pl.Buffered(1))
    in_specs = [pl.BlockSpec((tm, d), lambda i: (i, 0))]
    in_specs += [pl.BlockSpec((tm, x.shape[1]), lambda i: (i, 0)) for x in xs]
    in_specs += [const(w.shape) for w in ws]
    in_specs += [pl.BlockSpec((1, d), lambda i: (0, 0)), const((d, ff)), const((d, ff)), const((ff, d))]
    return pl.pallas_call(
        functools.partial(_ffn_body, n_in=len(xs), ff_chunk=ff_chunk),
        grid=(n_tok // tm,),
        in_specs=in_specs,
        out_specs=pl.BlockSpec((tm, d), lambda i: (i, 0)),
        out_shape=jax.ShapeDtypeStruct((n_tok, d), F32),
        compiler_params=_cparams(1),
        name="ffn_swiglu",
    )(h, *xs, *ws, g.reshape(1, d), wg, wu, wd)


def _decay_sum_matrix():
    c = CHUNK
    r = np.arange(c)[:, None]
    t = np.arange(c)[None, :]
    blocks = [(t <= r), (t > r)]
    for lvl in range(N_LEVELS):
        s = c >> (lvl + 1)
        m = (r // (2 * s)) * (2 * s) + s - 1
        upper = r > m
        blocks.append(np.where(upper, (t > m) & (t <= r), (t > r) & (t <= m)))
    w = np.concatenate(blocks, axis=0).astype(np.float32)
    return np.concatenate([w, w], axis=1)


def _level_masks():
    c = CHUNK
    row = lax.broadcasted_iota(I32, (c, LANES), 0)
    ri = lax.broadcasted_iota(I32, (c, c), 0)
    ci = lax.broadcasted_iota(I32, (c, c), 1)
    uppers, pairs = [], []
    for lvl in range(N_LEVELS):
        s = c >> (lvl + 1)
        uppers.append((row // s) % 2 == 1)
        pairs.append(((ri // (2 * s)) == (ci // (2 * s)))
                     & ((ri // s) % 2 == 1) & ((ci // s) % 2 == 0))
    return uppers, pairs, ri == ci


def _gated_chunk(qs, ks, lfs, vss, lmss, sts, wsum2, masks):
    c = CHUNK
    uppers, pairs, eye = masks
    n_g = len(qs)
    pieces = []
    for lf in lfs:
        lf2 = lf * LOG2_E
        hi = lf2.astype(BF16)
        lo = (lf2 - hi.astype(F32)).astype(BF16)
        pieces.append(jnp.concatenate([hi, lo], axis=0))
    ex = jnp.dot(wsum2, jnp.concatenate(pieces, axis=1), preferred_element_type=F32)
    fac = jnp.exp2(ex)
    fac_b = fac.astype(BF16)

    grams, inters, updates, diags = [], [], [], []
    for g in range(n_g):
        cols = slice(g * LANES, (g + 1) * LANES)
        q_b = qs[g].astype(BF16)
        k_b = ks[g].astype(BF16)
        w = [jnp.where(uppers[l], q_b, k_b) * fac_b[(2 + l) * c:(3 + l) * c, cols]
             for l in range(N_LEVELS)]
        qf = q_b * fac_b[0:c, cols]
        kf = k_b * fac_b[c:2 * c, cols]
        st_b = sts[g].astype(BF16)
        qk = qs[g] * ks[g]
        for v, lm in zip(vss[g], lmss[g]):
            if lm is None:
                sel = lambda a: a
            else:
                sel = lambda a, lm=lm: jnp.where(lm, a, jnp.zeros_like(a))
            grams.append([lax.dot_general(sel(w[l]), w[l], _NT, preferred_element_type=F32)
                          for l in range(N_LEVELS)])
            inters.append(lax.dot_general(sel(qf), st_b, _NT, preferred_element_type=F32))
            updates.append(lax.dot_general(v, sel(kf), _TN, preferred_element_type=F32))
            diags.append(jnp.sum(sel(qk), axis=-1, keepdims=True))

    outs, new_sts = [], []
    h = 0
    for g in range(n_g):
        cols = slice(g * LANES, (g + 1) * LANES)
        new_st = sts[g] * fac[c - 1:c, cols]
        outs_g = []
        for v in vss[g]:
            scores = jnp.where(eye, diags[h], 0.0)
            for l in range(N_LEVELS):
                scores = jnp.where(pairs[l], grams[h][l], scores)
            outs_g.append(jnp.dot(scores.astype(BF16), v, preferred_element_type=F32) + inters[h])
            new_st = new_st + updates[h]
            h += 1
        outs.append(outs_g)
        new_sts.append(new_st)
    return outs, new_sts


def _head_out(o, norm_g, gate):
    return (_rms(o, norm_g) * _silu(gate)).astype(BF16)


def _rope_body(pos_ref, invf_ref, cos_ref, sin_ref):
    ang = pos_ref[...].astype(F32) * invf_ref[...]
    lane = lax.broadcasted_iota(I32, (1, LANES), 1)
    cos_ref[...] = jnp.cos(ang)
    sin_ref[...] = jnp.sin(ang) * jnp.where(lane < LANES // 2, -1.0, 1.0)


def _rope_tables(pos_col, inv_freq2, *, tm=2048):
    n_tok = pos_col.shape[0]
    tm = min(tm, n_tok)
    table = jax.ShapeDtypeStruct((n_tok, LANES), F32)
    return pl.pallas_call(
        _rope_body,
        grid=(n_tok // tm,),
        in_specs=[pl.BlockSpec((tm, 1), lambda i: (i, 0)),
                  pl.BlockSpec((1, LANES), lambda i: (0, 0))],
        out_specs=[pl.BlockSpec((tm, LANES), lambda i: (i, 0))] * 2,
        out_shape=[table, table],
        compiler_params=_cparams(1),
        name="rope_tables",
    )(pos_col, inv_freq2)


def _retention_body(cos_ref, sin_ref, ng_ref, q_ref, k_ref, v_ref, gate_ref, o_ref, st_ref, *,
                    n_heads):
    c = CHUNK
    tb = q_ref.shape[0]

    @pl.when(pl.program_id(1) == 0)
    def _():
        st_ref[...] = jnp.zeros_like(st_ref)

    ri = lax.broadcasted_iota(I32, (c, c), 0)
    ci = lax.broadcasted_iota(I32, (c, c), 1)
    rel = (ri - ci).astype(F32)
    trow = lax.broadcasted_iota(I32, (c, LANES), 0).astype(F32)
    k_scale = float(LANES) ** -0.5
    ng = ng_ref[...]
    log_gammas = [math.log1p(-(2.0 ** (-5 - h))) for h in range(n_heads)]
    dmats = [jnp.where(rel >= 0, jnp.exp(rel * lg), 0.0) for lg in log_gammas]
    q_decays = [jnp.exp((trow + 1.0) * lg) for lg in log_gammas]
    k_decays = [jnp.exp((c - 1.0 - trow) * lg) * k_scale for lg in log_gammas]
    hcols = [slice(h * LANES, (h + 1) * LANES) for h in range(n_heads)]

    def chunk(ic, carry):
        r0 = pl.multiple_of(ic * c, c)
        rows = pl.ds(r0, c)
        cosv = cos_ref[rows, :]
        sinv = sin_ref[rows, :]
        scores, inters, updates, vs = [], [], [], []
        for h, cols in enumerate(hcols):
            q = q_ref[rows, cols].astype(F32)
            k = k_ref[rows, cols].astype(F32)
            v = v_ref[rows, cols]
            qr = q * cosv + pltpu.roll(q, LANES // 2, 1) * sinv
            kr = k * cosv + pltpu.roll(k, LANES // 2, 1) * sinv
            scores.append(_bdot_nt(qr, kr * k_scale))
            inters.append(_bdot_nt(qr * q_decays[h], st_ref[h]))
            updates.append(lax.dot_general(v, (kr * k_decays[h]).astype(BF16), _TN,
                                           preferred_element_type=F32))
            vs.append(v)
        for h, cols in enumerate(hcols):
            o = jnp.dot((scores[h] * dmats[h]).astype(BF16), vs[h],
                        preferred_element_type=F32) + inters[h]
            st_ref[h] = math.exp(c * log_gammas[h]) * st_ref[h] + updates[h]
            o_ref[rows, cols] = _head_out(o, ng, gate_ref[rows, cols].astype(F32))
        return carry

    lax.fori_loop(0, tb // c, chunk, 0, unroll=2)


def _retention(z, cos_t, sin_t, norm_g, *, batch, seq, n_heads, col0, tb=512):
    n_tok = z.shape[0]
    w = n_heads * LANES
    assert seq % tb == 0 and tb % CHUNK == 0
    nt = seq // tb
    cb = col0 // w
    zspec = lambda j: pl.BlockSpec((tb, w), lambda b, t, j=j: (b * nt + t, cb + j))
    tspec = pl.BlockSpec((tb, LANES), lambda b, t: (b * nt + t, 0))
    return pl.pallas_call(
        functools.partial(_retention_body, n_heads=n_heads),
        grid=(batch, nt),
        in_specs=[tspec, tspec,
                  pl.BlockSpec((1, LANES), lambda b, t: (0, 0)),
                  zspec(0), zspec(1), zspec(2), zspec(3)],
        out_specs=pl.BlockSpec((tb, w), lambda b, t: (b * nt + t, 0)),
        out_shape=jax.ShapeDtypeStruct((n_tok, w), BF16),
        scratch_shapes=[pltpu.VMEM((n_heads, HEAD_DV, LANES), F32)],
        compiler_params=_cparams(2),
        name="retention",
    )(cos_t, sin_t, norm_g.reshape(1, LANES), z, z, z, z)


def _gla_body(wsum_ref, wa_ref, ba2_ref, ng_ref, a_ref, q_ref, k_ref, v_ref, gate_ref, o_ref,
              st_ref, lf_ref, *, n_groups):
    c = CHUNK
    tb = q_ref.shape[0]

    @pl.when(pl.program_id(1) == 0)
    def _():
        st_ref[...] = jnp.zeros_like(st_ref)

    x = jnp.dot(a_ref[...], wa_ref[...], preferred_element_type=F32) + ba2_ref[...]
    lf_ref[...] = (jnp.minimum(x, 0.0) - jnp.log1p(jnp.exp(-jnp.abs(x)))) * (1.0 / GLA_TAU)

    masks = _level_masks()
    lane = lax.broadcasted_iota(I32, (1, LANES), 1)
    lane_masks = [lane < LANES // 2, lane >= LANES // 2]
    wsum = wsum_ref[...]
    ng = ng_ref[...]
    q_scale = float(LANES // 2) ** -0.5

    def chunk(ic, carry):
        r0 = pl.multiple_of(ic * c, c)
        rows = pl.ds(r0, c)
        kcols = [slice(g * LANES, (g + 1) * LANES) for g in range(n_groups)]
        vcols = [[slice((2 * g + j) * HEAD_DV, (2 * g + j + 1) * HEAD_DV) for j in range(2)]
                 for g in range(n_groups)]
        outs, new_sts = _gated_chunk(
            [q_ref[rows, kc].astype(F32) * q_scale for kc in kcols],
            [k_ref[rows, kc].astype(F32) for kc in kcols],
            [lf_ref[rows, kc] for kc in kcols],
            [[v_ref[rows, vc] for vc in vcs] for vcs in vcols],
            [lane_masks] * n_groups,
            [st_ref[g] for g in range(n_groups)],
            wsum, masks)
        for g in range(n_groups):
            st_ref[g] = new_sts[g]
            for o, vc in zip(outs[g], vcols[g]):
                o_ref[rows, vc] = _head_out(o, ng, gate_ref[rows, vc].astype(F32))
        return carry

    lax.fori_loop(0, tb // c, chunk, 0, unroll=2)


def _gla(z, wsum, wa_pad, b_a2, norm_g, *, batch, seq, n_heads, col0, col_a, tb=512):
    n_tok = z.shape[0]
    wk = n_heads * (LANES // 2)
    wv = n_heads * HEAD_DV
    assert seq % tb == 0 and tb % CHUNK == 0
    nt = seq // tb
    n_groups = n_heads // 2
    row = lambda b, t: b * nt + t
    const = lambda shape: pl.BlockSpec(shape, lambda b, t: (0, 0))
    return pl.pallas_call(
        functools.partial(_gla_body, n_groups=n_groups),
        grid=(batch, nt),
        in_specs=[const(wsum.shape), const(wa_pad.shape), const((1, wk)), const((1, LANES)),
                  pl.BlockSpec((tb, LANES), lambda b, t: (row(b, t), col_a // LANES)),
                  pl.BlockSpec((tb, wk), lambda b, t: (row(b, t), col0 // wk)),
                  pl.BlockSpec((tb, wk), lambda b, t: (row(b, t), col0 // wk + 1)),
                  pl.BlockSpec((tb, wv), lambda b, t: (row(b, t), (col0 + 2 * wk) // wv)),
                  pl.BlockSpec((tb, wv), lambda b, t: (row(b, t), (col0 + 2 * wk) // wv + 1))],
        out_specs=pl.BlockSpec((tb, wv), lambda b, t: (row(b, t), 0)),
        out_shape=jax.ShapeDtypeStruct((n_tok, wv), BF16),
        scratch_shapes=[pltpu.VMEM((n_groups, HEAD_DV, LANES), F32),
                        pltpu.VMEM((tb, wk), F32)],
        compiler_params=_cparams(2),
        name="gla",
    )(wsum, wa_pad, b_a2.reshape(1, wk), norm_g.reshape(1, LANES), z, z, z, z, z)


def _hgrn_body(wsum_ref, lb_ref, ng_ref, q_ref, f_ref, i_ref, gate_ref, o_ref, st_ref, *, n_heads):
    c = CHUNK
    tb = q_ref.shape[0]

    @pl.when(pl.program_id(2) == 0)
    def _():
        st_ref[...] = jnp.zeros_like(st_ref)

    masks = _level_masks()
    wsum = wsum_ref[...]
    ng = ng_ref[...]
    q_scale = float(LANES) ** -0.5

    def chunk(ic, carry):
        r0 = pl.multiple_of(ic * c, c)
        rows = pl.ds(r0, c)
        hcols = [slice(h * LANES, (h + 1) * LANES) for h in range(n_heads)]
        qs, ks, lfs = [], [], []
        for cols in hcols:
            lb = lb_ref[:, cols]
            f = f_ref[rows, cols].astype(F32)
            t = jnp.tanh(0.5 * f)
            b = 0.5 * (1.0 - lb)
            bt = b * t
            lfs.append(jnp.log((0.5 * (1.0 + lb)) + bt))
            ks.append(b - bt)
            qs.append(_silu(q_ref[rows, cols].astype(F32), q_scale))
        outs, new_sts = _gated_chunk(
            qs, ks, lfs, [[i_ref[rows, cols]] for cols in hcols], [[None]] * n_heads,
            [st_ref[h] for h in range(n_heads)], wsum, masks)
        for h, cols in enumerate(hcols):
            st_ref[h] = new_sts[h]
            o_ref[rows, cols] = _head_out(outs[h][0], ng, gate_ref[rows, cols].astype(F32))
        return carry

    lax.fori_loop(0, tb // c, chunk, 0, unroll=2)


def _hgrn(z, wsum, lb, norm_g, *, batch, seq, n_heads, heads_per_step=8, tb=512):
    n_tok = z.shape[0]
    w = heads_per_step * LANES
    ng_ = n_heads // heads_per_step
    assert seq % tb == 0 and tb % CHUNK == 0
    nt = seq // tb
    zspec = lambda j: pl.BlockSpec((tb, w), lambda b, g, t, j=j: (b * nt + t, j * ng_ + g))
    return pl.pallas_call(
        functools.partial(_hgrn_body, n_heads=heads_per_step),
        grid=(batch, ng_, nt),
        in_specs=[pl.BlockSpec(wsum.shape, lambda b, g, t: (0, 0)),
                  pl.BlockSpec((1, w), lambda b, g, t: (0, g)),
                  pl.BlockSpec((1, LANES), lambda b, g, t: (0, 0)),
                  zspec(0), zspec(1), zspec(2), zspec(3)],
        out_specs=pl.BlockSpec((tb, w), lambda b, g, t: (b * nt + t, g)),
        out_shape=jax.ShapeDtypeStruct((n_tok, n_heads * LANES), BF16),
        scratch_shapes=[pltpu.VMEM((heads_per_step, HEAD_DV, LANES), F32)],
        compiler_params=_cparams(3),
        name="hgrn2",
    )(wsum, lb.reshape(1, n_heads * LANES), norm_g.reshape(1, LANES), z, z, z, z)


def _router_body(h_ref, x_ref, wo_ref, g_ref, w_ref, h1_ref, route_ref, cnt_ref, run_ref):
    tm = h_ref.shape[0]

    @pl.when(pl.program_id(0) == 0)
    def _():
        run_ref[...] = jnp.zeros_like(run_ref)

    ts = ROUTE_SUB
    w = w_ref[...]
    w_hi = w.astype(BF16)
    w_lo = (w - w_hi.astype(F32)).astype(BF16)
    lane = lax.broadcasted_iota(I32, (ts, LANES), 1)
    lane_f = lane.astype(F32)
    ri = lax.broadcasted_iota(I32, (ts, ts), 0)
    ci = lax.broadcasted_iota(I32, (ts, ts), 1)
    lstrict = jnp.where(ri > ci, 1.0, 0.0).astype(BF16)
    neg = -jnp.inf
    subs = [slice(r0, r0 + ts) for r0 in range(0, tm, ts)]
    h1s = [h_ref[rows, :] + jnp.dot(x_ref[rows, :], wo_ref[...], preferred_element_type=F32)
           for rows in subs]
    logits = []
    for rows, h1 in zip(subs, h1s):
        h1_ref[rows, :] = h1
        n = _rms(h1, g_ref[...])
        n_hi = n.astype(BF16)
        n_lo = (n - n_hi.astype(F32)).astype(BF16)
        logits.append(jnp.dot(n_hi, w_hi, preferred_element_type=F32)
                      + jnp.dot(n_hi, w_lo, preferred_element_type=F32)
                      + jnp.dot(n_lo, w_hi, preferred_element_type=F32))
    picks = []
    for lg in logits:
        lg1 = jnp.where(lane < N_EXPERTS, lg, neg)
        m1 = jnp.max(lg1, axis=-1, keepdims=True)
        i1 = jnp.min(jnp.where(lg1 == m1, lane_f, float(LANES)), axis=-1, keepdims=True)
        oh1 = lane_f == i1
        lg2 = jnp.where(oh1, neg, lg1)
        m2 = jnp.max(lg2, axis=-1, keepdims=True)
        i2 = jnp.min(jnp.where(lg2 == m2, lane_f, float(LANES)), axis=-1, keepdims=True)
        oh2 = lane_f == i2
        e2 = jnp.exp(m2 - m1)
        g1 = 1.0 / (1.0 + e2)
        both = jnp.where(oh1, 1.0, 0.0) + jnp.where(oh2, 1.0, 0.0)
        prefix = jnp.dot(lstrict, both.astype(BF16), preferred_element_type=F32)
        picks.append((i1, i2, g1, e2 * g1, oh1, oh2, both, prefix))
    run = run_ref[...]
    for rows, (i1, i2, g1, g2, oh1, oh2, both, prefix) in zip(subs, picks):
        before = prefix + run
        r1 = jnp.sum(jnp.where(oh1, before, 0.0), axis=-1, keepdims=True)
        r2 = jnp.sum(jnp.where(oh2, before, 0.0), axis=-1, keepdims=True)
        run = run + jnp.sum(both, axis=0, keepdims=True)
        out = jnp.where(lane == 0, i1, 0.0)
        out = jnp.where(lane == 1, i2, out)
        out = jnp.where(lane == 2, g1, out)
        out = jnp.where(lane == 3, g2, out)
        out = jnp.where(lane == 4, r1, out)
        out = jnp.where(lane == 5, r2, out)
        route_ref[rows, :] = out
    run_ref[...] = run
    cnt_ref[...] = run


def _router(h, x, w_out, g, w_pad, *, tm=512):
    n_tok, d = h.shape
    return pl.pallas_call(
        _router_body,
        grid=(n_tok // tm,),
        in_specs=[pl.BlockSpec((tm, d), lambda i: (i, 0)),
                  pl.BlockSpec((tm, x.shape[1]), lambda i: (i, 0)),
                  pl.BlockSpec(w_out.shape, lambda i: (0, 0), pipeline_mode=pl.Buffered(1)),
                  pl.BlockSpec((1, d), lambda i: (0, 0)),
                  pl.BlockSpec((d, LANES), lambda i: (0, 0))],
        out_specs=[pl.BlockSpec((tm, d), lambda i: (i, 0)),
                   pl.BlockSpec((tm, LANES), lambda i: (i, 0)),
                   pl.BlockSpec((1, LANES), lambda i: (0, 0))],
        out_shape=[jax.ShapeDtypeStruct((n_tok, d), F32),
                   jax.ShapeDtypeStruct((n_tok, LANES), F32),
                   jax.ShapeDtypeStruct((1, LANES), F32)],
        scratch_shapes=[pltpu.VMEM((1, LANES), F32)],
        compiler_params=_cparams(1),
        name="moe_router",
    )(h, x, w_out, g.reshape(1, d), w_pad)


def _dispatch_body(info_ref, h_ref, g_ref, pos_ref, xs_ref, nbuf, zbuf, idx, sems, isem, *, tm):
    i = pl.program_id(0)
    last = pl.num_programs(0) - 1
    slot = i % 2
    tg = zbuf.shape[0]

    def zero_copy(e):
        return pltpu.make_async_copy(
            zbuf, xs_ref.at[pl.ds(pl.multiple_of(info_ref[e], tg), tg)], sems.at[0])

    @pl.when(i == 0)
    def _():
        zbuf[...] = jnp.zeros_like(zbuf)
        for e in range(2 * N_EXPERTS):
            @pl.when(info_ref[e] >= 0)
            def _():
                zero_copy(e).start()
        for e in range(2 * N_EXPERTS):
            @pl.when(info_ref[e] >= 0)
            def _():
                zero_copy(e).wait()

    def drain(s):
        for _ in range(2):
            pltpu.make_async_copy(nbuf.at[s], xs_ref.at[pl.ds(0, tm)], sems.at[s]).wait()

    @pl.when(i >= 2)
    def _():
        drain(slot)

    idx_copy = pltpu.make_async_copy(pos_ref.at[i], idx, isem)
    idx_copy.start()
    nbuf[slot] = _rms(h_ref[...], g_ref[...])
    idx_copy.wait()

    def issue(j, carry):
        t0 = pl.multiple_of(j * ROW_UNROLL, ROW_UNROLL)
        for u in range(ROW_UNROLL):
            for k in range(2):
                pltpu.make_async_copy(nbuf.at[slot, pl.ds(t0 + u, 1)],
                                      xs_ref.at[pl.ds(idx[k * tm + t0 + u], 1)],
                                      sems.at[slot]).start()
        return carry

    lax.fori_loop(0, tm // ROW_UNROLL, issue, 0)

    @pl.when(i == last)
    def _():
        drain(slot)

        @pl.when(i >= 1)
        def _():
            drain(1 - slot)


def _dispatch(h, g, pos_tiles, info, rows_pad, *, tm, tg):
    n_tok, d = h.shape
    return pl.pallas_call(
        functools.partial(_dispatch_body, tm=tm),
        grid_spec=pltpu.PrefetchScalarGridSpec(
            num_scalar_prefetch=1,
            grid=(n_tok // tm,),
            in_specs=[pl.BlockSpec((tm, d), lambda i, info: (i, 0)),
                      pl.BlockSpec((1, d), lambda i, info: (0, 0)),
                      pl.BlockSpec(memory_space=pl.ANY)],
            out_specs=pl.BlockSpec(memory_space=pl.ANY),
            scratch_shapes=[pltpu.VMEM((2, tm, d), F32),
                            pltpu.VMEM((tg, d), F32),
                            pltpu.SMEM((2 * tm,), I32),
                            pltpu.SemaphoreType.DMA((2,)),
                            pltpu.SemaphoreType.DMA]),
        out_shape=jax.ShapeDtypeStruct((rows_pad, d), F32),
        compiler_params=_cparams(1),
        name="moe_dispatch",
    )(info, h, g.reshape(1, d), pos_tiles)


def _gmm_body(te_ref, nu_ref, x_ref, wg_ref, wu_ref, wd_ref, y_ref, *, ff_chunk):
    @pl.when(pl.program_id(0) >= nu_ref[0])
    def _():
        y_ref[...] = jnp.zeros_like(y_ref)

    @pl.when(pl.program_id(0) < nu_ref[0])
    def _():
        x = x_ref[...].astype(BF16)
        acc = None
        for c in range(0, wg_ref.shape[1], ff_chunk):
            a = jnp.dot(x, wg_ref[:, c:c + ff_chunk], preferred_element_type=F32)
            b = jnp.dot(x, wu_ref[:, c:c + ff_chunk], preferred_element_type=F32)
            hid = (_silu(a) * b).astype(BF16)
            part = jnp.dot(hid, wd_ref[c:c + ff_chunk, :], preferred_element_type=F32)
            acc = part if acc is None else acc + part
        y_ref[...] = acc


def _gmm(xs, tile_expert, n_used, wg, wu, wd, layer, *, tm, ff_chunk=7 * MXU_DIM):
    rows_pad, d = xs.shape
    ff = wg.shape[3]
    assert ff % ff_chunk == 0 and ff_chunk % MXU_DIM == 0
    tile = lambda i, te, nu: jnp.minimum(i, nu[0] - 1)
    wspec = lambda shape: pl.BlockSpec(
        (None, None) + shape, lambda i, te, nu: (layer, te[tile(i, te, nu)], 0, 0),
        pipeline_mode=pl.Buffered(1))
    return pl.pallas_call(
        functools.partial(_gmm_body, ff_chunk=ff_chunk),
        grid_spec=pltpu.PrefetchScalarGridSpec(
            num_scalar_prefetch=2,
            grid=(rows_pad // tm,),
            in_specs=[pl.BlockSpec((tm, d), lambda i, te, nu: (tile(i, te, nu), 0)),
                      wspec((d, ff)), wspec((d, ff)), wspec((ff, d))],
            out_specs=pl.BlockSpec((tm, d), lambda i, te, nu: (i, 0))),
        out_shape=jax.ShapeDtypeStruct((rows_pad, d), F32),
        compiler_params=_cparams(1),
        name="moe_experts",
    )(tile_expert, n_used, xs, wg, wu, wd)


def _combine_body(h_ref, route_ref, fg_ref, pos_ref, ys_ref, o_ref, gbuf, idx, sems, isem, *,
                  tm, final_norm):
    i = pl.program_id(0)
    slot = i % 2

    def gather(tile, s):
        idx_copy = pltpu.make_async_copy(pos_ref.at[tile], idx, isem)
        idx_copy.start()
        idx_copy.wait()

        def issue(j, carry):
            t0 = pl.multiple_of(j * ROW_UNROLL, ROW_UNROLL)
            for u in range(ROW_UNROLL):
                for k in range(2):
                    pltpu.make_async_copy(ys_ref.at[pl.ds(idx[k * tm + t0 + u], 1)],
                                          gbuf.at[s, k, pl.ds(t0 + u, 1)], sems.at[s]).start()
            return carry

        lax.fori_loop(0, tm // ROW_UNROLL, issue, 0)

    @pl.when(i == 0)
    def _():
        gather(0, 0)

    @pl.when(i + 1 < pl.num_programs(0))
    def _():
        gather(i + 1, 1 - slot)

    for k in range(2):
        pltpu.make_async_copy(ys_ref.at[pl.ds(0, tm)], gbuf.at[slot, k], sems.at[slot]).wait()

    route = route_ref[...]
    out = h_ref[...] + route[:, 2:3] * gbuf[slot, 0] + route[:, 3:4] * gbuf[slot, 1]
    if final_norm:
        out = _rms(out, fg_ref[...])
    o_ref[...] = out


def _combine(h, route, final_g, pos_tiles, ys, *, tm, final_norm):
    n_tok, d = h.shape
    return pl.pallas_call(
        functools.partial(_combine_body, tm=tm, final_norm=final_norm),
        grid=(n_tok // tm,),
        in_specs=[pl.BlockSpec((tm, d), lambda i: (i, 0)),
                  pl.BlockSpec((tm, LANES), lambda i: (i, 0)),
                  pl.BlockSpec((1, d), lambda i: (0, 0)),
                  pl.BlockSpec(memory_space=pl.ANY),
                  pl.BlockSpec(memory_space=pl.ANY)],
        out_specs=pl.BlockSpec((tm, d), lambda i: (i, 0)),
        out_shape=jax.ShapeDtypeStruct((n_tok, d), F32),
        scratch_shapes=[pltpu.VMEM((2, 2, tm, d), F32),
                        pltpu.SMEM((2 * tm,), I32),
                        pltpu.SemaphoreType.DMA((2,)),
                        pltpu.SemaphoreType.DMA],
        compiler_params=_cparams(1),
        name="moe_combine",
    )(h, route, final_g.reshape(1, d), pos_tiles, ys)


def _moe(h, x, w_out, norm_g, w_router, wg, wu, wd, layer, final_g, *, final_norm, tm_route=512,
         tm_gmm=512, tm_rows=256):
    n_tok, d = h.shape
    w_pad = jnp.zeros((d, LANES), F32).at[:, :N_EXPERTS].set(w_router.astype(F32))
    h, route, counts = _router(h, x, w_out, norm_g, w_pad, tm=tm_route)

    cnt = counts[0, :N_EXPERTS].astype(I32)
    gsz = ((cnt + tm_gmm - 1) // tm_gmm) * tm_gmm
    ends = jnp.cumsum(gsz)
    offs = ends - gsz
    rows_pad = (n_tok * 2 // tm_gmm + N_EXPERTS) * tm_gmm
    n_tiles = rows_pad // tm_gmm
    tile_expert = jnp.minimum(
        jnp.searchsorted(ends, jnp.arange(n_tiles, dtype=I32) * tm_gmm, side="right"),
        N_EXPERTS - 1).astype(I32)
    n_used = jnp.maximum(ends[-1:] // tm_gmm, 1).astype(I32)
    tail = ends[-1] + jnp.arange(N_EXPERTS, dtype=I32) * tm_gmm
    info = jnp.concatenate([jnp.where(gsz > 0, ends - tm_gmm, -1),
                            jnp.where(tail < rows_pad, tail, -1)]).astype(I32)

    e01 = route[:, 0:2].astype(I32)
    pos = jnp.clip(offs[jnp.clip(e01, 0, N_EXPERTS - 1)] + route[:, 4:6].astype(I32),
                   0, rows_pad - 1)
    pos_tiles = pos.reshape(n_tok // tm_rows, tm_rows, 2).transpose(0, 2, 1).reshape(
        n_tok // tm_rows, 2 * tm_rows)

    xs = _dispatch(h, norm_g, pos_tiles, info, rows_pad, tm=tm_rows, tg=tm_gmm)
    ys = _gmm(xs, tile_expert, n_used, wg, wu, wd, layer, tm=tm_gmm)
    return _combine(h, route, final_g, pos_tiles, ys, tm=tm_rows, final_norm=final_norm)


def kernel(x, positions, norm_mix_g, norm_ffn_g, final_norm_g, ab_w_in, gla_w_a2, gla_b_a2,
           ret_norm_g, gla_norm_g, ab_w_out, ffn_w_gate, ffn_w_up, ffn_w_down,
           hgrn_lb_logits, c_w_in, hgrn_norm_g, c_w_out, moe_router, moe_w_gate,
           moe_w_up, moe_w_down):
    batch, seq, d = x.shape
    depth = norm_mix_g.shape[0]
    n_tok = batch * seq
    ret_heads = 4
    gla_heads = 4
    gla_rank = gla_w_a2.shape[1]
    hgrn_heads = d // LANES
    ab_cols = ab_w_in.shape[2] - gla_rank

    lb_cum = jnp.cumsum(jax.nn.softmax(hgrn_lb_logits.astype(F32), axis=0), axis=0)
    lower_bounds = lb_cum - lb_cum[0:1]

    wsum = jnp.asarray(_decay_sum_matrix(), BF16)
    inv_freq = ROPE_BASE ** (-jnp.arange(0, LANES, 2, dtype=F32) / LANES)
    inv_freq2 = jnp.concatenate([inv_freq, inv_freq]).reshape(1, LANES)
    cos_t, sin_t = _rope_tables(positions.reshape(n_tok, 1), inv_freq2)

    moe_wg, moe_wu, moe_wd = (w.astype(BF16) for w in (moe_w_gate, moe_w_up, moe_w_down))

    h = x.reshape(n_tok, d)
    for l in range(depth):
        e = l // 2
        if l % 2 == 0:
            w_in = jnp.pad(ab_w_in[e], ((0, 0), (0, LANES - gla_rank))).astype(BF16)
            z = _norm_proj(h, norm_mix_g[l], w_in)
            oa = _retention(z, cos_t, sin_t, ret_norm_g[e], batch=batch, seq=seq,
                            n_heads=ret_heads, col0=0)
            wa_pad = jnp.pad(gla_w_a2[e], ((0, LANES - gla_rank), (0, 0))).astype(BF16)
            ob = _gla(z, wsum, wa_pad, gla_b_a2[e], gla_norm_g[e], batch=batch, seq=seq,
                      n_heads=gla_heads, col0=4 * ret_heads * LANES, col_a=ab_cols)
            w_out = ab_w_out[e].astype(BF16)
            half = ret_heads * HEAD_DV
            h = _ffn(h, [oa, ob], [w_out[:half], w_out[half:]], norm_ffn_g[l],
                     ffn_w_gate[e].astype(BF16), ffn_w_up[e].astype(BF16),
                     ffn_w_down[e].astype(BF16))
        else:
            z = _norm_proj(h, norm_mix_g[l], c_w_in[e].astype(BF16))
            o = _hgrn(z, wsum, lower_bounds[l], hgrn_norm_g[e], batch=batch, seq=seq,
                      n_heads=hgrn_heads)
            h = _moe(h, o, c_w_out[e].astype(BF16), norm_ffn_g[l], moe_router[e],
                     moe_wg, moe_wu, moe_wd, e, final_norm_g, final_norm=(l == depth - 1))
    if depth % 2 == 1:
        raise NotImplementedError("final norm is fused into the last (odd) layer")
    return h.reshape(batch, seq, d)
```

```python
import functools
import math

import numpy as np
import jax
import jax.numpy as jnp
from jax import lax
from jax.experimental import pallas as pl
from jax.experimental.pallas import tpu as pltpu

F32 = jnp.float32
BF16 = jnp.bfloat16
I32 = jnp.int32

EPS = 1e-6
LANES = 128
MXU_DIM = 256
CHUNK = 64
N_LEVELS = 6
HEAD_DV = 128
ROPE_BASE = 10000.0
GLA_TAU = 16.0
LOG2_E = 1.4426950408889634
N_EXPERTS = 8
ROUTE_SUB = 128
ROW_UNROLL = 8
VMEM_LIMIT = 56 * 1024 * 1024

_NT = (((1,), (1,)), ((), ()))
_TN = (((0,), (0,)), ((), ()))


def _cparams(n_axes):
    return pltpu.CompilerParams(dimension_semantics=("arbitrary",) * n_axes,
                                vmem_limit_bytes=VMEM_LIMIT)


def _rms(x, g):
    ms = jnp.mean(x * x, axis=-1, keepdims=True)
    return x * lax.rsqrt(ms + EPS) * g


def _silu(x, scale=1.0):
    t = jnp.tanh(0.5 * x)
    hx = x * (0.5 * scale)
    return hx + hx * t


def _bdot_nt(a, b):
    return lax.dot_general(a.astype(BF16), b.astype(BF16), _NT, preferred_element_type=F32)


def _bdot_tn(a, b):
    return lax.dot_general(a.astype(BF16), b.astype(BF16), _TN, preferred_element_type=F32)


def _norm_proj_body(h_ref, g_ref, w_ref, z_ref, *, col_chunk):
    n = _rms(h_ref[...], g_ref[...]).astype(BF16)
    m = w_ref.shape[1]
    for c in range(0, m, col_chunk):
        ce = min(c + col_chunk, m)
        z_ref[:, c:ce] = jnp.dot(n, w_ref[:, c:ce], preferred_element_type=F32).astype(z_ref.dtype)


def _norm_proj(h, g, w, *, tm=512, col_chunk=512):
    n_tok, d = h.shape
    m = w.shape[1]
    assert n_tok % tm == 0
    return pl.pallas_call(
        functools.partial(_norm_proj_body, col_chunk=col_chunk),
        grid=(n_tok // tm,),
        in_specs=[pl.BlockSpec((tm, d), lambda i: (i, 0)),
                  pl.BlockSpec((1, d), lambda i: (0, 0)),
                  pl.BlockSpec((d, m), lambda i: (0, 0), pipeline_mode=pl.Buffered(1))],
        out_specs=pl.BlockSpec((tm, m), lambda i: (i, 0)),
        out_shape=jax.ShapeDtypeStruct((n_tok, m), BF16),
        compiler_params=_cparams(1),
        name="norm_proj",
    )(h, g.reshape(1, d), w)


def _ffn_body(*refs, n_in, ff_chunk):
    h_ref = refs[0]
    x_refs = refs[1:1 + n_in]
    w_refs = refs[1 + n_in:1 + 2 * n_in]
    g_ref, wg_ref, wu_ref, wd_ref, o_ref = refs[1 + 2 * n_in:]
    h = h_ref[...]
    for x_ref, w_ref in zip(x_refs, w_refs):
        h = h + jnp.dot(x_ref[...], w_ref[...], preferred_element_type=F32)
    o_ref[...] = h
    n = _rms(h, g_ref[...]).astype(BF16)
    acc = None
    ff = wg_ref.shape[1]
    for c in range(0, ff, ff_chunk):
        ce = min(c + ff_chunk, ff)
        a = jnp.dot(n, wg_ref[:, c:ce], preferred_element_type=F32)
        b = jnp.dot(n, wu_ref[:, c:ce], preferred_element_type=F32)
        hid = (_silu(a) * b).astype(BF16)
        part = jnp.dot(hid, wd_ref[c:ce, :], preferred_element_type=F32)
        acc = part if acc is None else acc + part
    o_ref[...] = o_ref[...] + acc


def _ffn(h, xs, ws, g, wg, wu, wd, *, tm=512, ff_chunk=6 * MXU_DIM):
    n_tok, d = h.shape
    ff = wg.shape[1]
    assert ff % MXU_DIM == 0
    const = lambda shape: pl.BlockSpec(shape, lambda i: (0, 0), pipeline_mode=pl.Buffered(1))
    in_specs = [pl.BlockSpec((tm, d), lambda i: (i, 0))]
    in_specs += [pl.BlockSpec((tm, x.shape[1]), lambda i: (i, 0)) for x in xs]
    in_specs += [const(w.shape) for w in ws]
    in_specs += [pl.BlockSpec((1, d), lambda i: (0, 0)), const((d, ff)), const((d, ff)), const((ff, d))]
    return pl.pallas_call(
        functools.partial(_ffn_body, n_in=len(xs), ff_chunk=ff_chunk),
        grid=(n_tok // tm,),
        in_specs=in_specs,
        out_specs=pl.BlockSpec((tm, d), lambda i: (i, 0)),
        out_shape=jax.ShapeDtypeStruct((n_tok, d), F32),
        compiler_params=_cparams(1),
        name="ffn_swiglu",
    )(h, *xs, *ws, g.reshape(1, d), wg, wu, wd)


def _decay_sum_matrix():
    c = CHUNK
    r = np.arange(c)[:, None]
    t = np.arange(c)[None, :]
    blocks = [(t <= r), (t > r)]
    for lvl in range(N_LEVELS):
        s = c >> (lvl + 1)
        m = (r // (2 * s)) * (2 * s) + s - 1
        upper = r > m
        blocks.append(np.where(upper, (t > m) & (t <= r), (t > r) & (t <= m)))
    w = np.concatenate(blocks, axis=0).astype(np.float32)
    return np.concatenate([w, w], axis=1)


def _level_masks():
    c = CHUNK
    row = lax.broadcasted_iota(I32, (c, LANES), 0)
    ri = lax.broadcasted_iota(I32, (c, c), 0)
    ci = lax.broadcasted_iota(I32, (c, c), 1)
    uppers, pairs = [], []
    for lvl in range(N_LEVELS):
        s = c >> (lvl + 1)
        uppers.append((row // s) % 2 == 1)
        pairs.append(((ri // (2 * s)) == (ci // (2 * s)))
                     & ((ri // s) % 2 == 1) & ((ci // s) % 2 == 0))
    return uppers, pairs, ri == ci


def _gated_chunk(qs, ks, lfs, vss, lmss, sts, wsum2, masks):
    c = CHUNK
    uppers, pairs, eye = masks
    n_g = len(qs)
    pieces = []
    for lf in lfs:
        lf2 = lf * LOG2_E
        hi = lf2.astype(BF16)
        lo = (lf2 - hi.astype(F32)).astype(BF16)
        pieces.append(jnp.concatenate([hi, lo], axis=0))
    ex = jnp.dot(wsum2, jnp.concatenate(pieces, axis=1), preferred_element_type=F32)
    fac = jnp.exp2(ex)
    fac_b = fac.astype(BF16)

    grams, inters, updates, diags = [], [], [], []
    for g in range(n_g):
        cols = slice(g * LANES, (g + 1) * LANES)
        q_b = qs[g].astype(BF16)
        k_b = ks[g].astype(BF16)
        w = [jnp.where(uppers[l], q_b, k_b) * fac_b[(2 + l) * c:(3 + l) * c, cols]
             for l in range(N_LEVELS)]
        qf = q_b * fac_b[0:c, cols]
        kf = k_b * fac_b[c:2 * c, cols]
        st_b = sts[g].astype(BF16)
        qk = qs[g] * ks[g]
        for v, lm in zip(vss[g], lmss[g]):
            if lm is None:
                sel = lambda a: a
            else:
                sel = lambda a, lm=lm: jnp.where(lm, a, jnp.zeros_like(a))
            grams.append([lax.dot_general(sel(w[l]), w[l], _NT, preferred_element_type=F32)
                          for l in range(N_LEVELS)])
            inters.append(lax.dot_general(sel(qf), st_b, _NT, preferred_element_type=F32))
            updates.append(lax.dot_general(v, sel(kf), _TN, preferred_element_type=F32))
            diags.append(jnp.sum(sel(qk), axis=-1, keepdims=True))

    outs, new_sts = [], []
    h = 0
    for g in range(n_g):
        cols = slice(g * LANES, (g + 1) * LANES)
        new_st = sts[g] * fac[c - 1:c, cols]
        outs_g = []
        for v in vss[g]:
            scores = jnp.where(eye, diags[h], 0.0)
            for l in range(N_LEVELS):
                scores = jnp.where(pairs[l], grams[h][l], scores)
            outs_g.append(jnp.dot(scores.astype(BF16), v, preferred_element_type=F32) + inters[h])
            new_st = new_st + updates[h]
            h += 1
        outs.append(outs_g)
        new_sts.append(new_st)
    return outs, new_sts


def _head_out(o, norm_g, gate):
    return (_rms(o, norm_g) * _silu(gate)).astype(BF16)


def _rope_body(pos_ref, invf_ref, cos_ref, sin_ref):
    ang = pos_ref[...].astype(F32) * invf_ref[...]
    lane = lax.broadcasted_iota(I32, (1, LANES), 1)
    cos_ref[...] = jnp.cos(ang)
    sin_ref[...] = jnp.sin(ang) * jnp.where(lane < LANES // 2, -1.0, 1.0)


def _rope_tables(pos_col, inv_freq2, *, tm=2048):
    n_tok = pos_col.shape[0]
    tm = min(tm, n_tok)
    table = jax.ShapeDtypeStruct((n_tok, LANES), F32)
    return pl.pallas_call(
        _rope_body,
        grid=(n_tok // tm,),
        in_specs=[pl.BlockSpec((tm, 1), lambda i: (i, 0)),
                  pl.BlockSpec((1, LANES), lambda i: (0, 0))],
        out_specs=[pl.BlockSpec((tm, LANES), lambda i: (i, 0))] * 2,
        out_shape=[table, table],
        compiler_params=_cparams(1),
        name="rope_tables",
    )(pos_col, inv_freq2)


def _retention_body(cos_ref, sin_ref, ng_ref, q_ref, k_ref, v_ref, gate_ref, o_ref, st_ref, *,
                    n_heads):
    c = CHUNK
    tb = q_ref.shape[0]

    @pl.when(pl.program_id(1) == 0)
    def _():
        st_ref[...] = jnp.zeros_like(st_ref)

    ri = lax.broadcasted_iota(I32, (c, c), 0)
    ci = lax.broadcasted_iota(I32, (c, c), 1)
    rel = (ri - ci).astype(F32)
    trow = lax.broadcasted_iota(I32, (c, LANES), 0).astype(F32)
    k_scale = float(LANES) ** -0.5
    ng = ng_ref[...]
    log_gammas = [math.log1p(-(2.0 ** (-5 - h))) for h in range(n_heads)]
    dmats = [jnp.where(rel >= 0, jnp.exp(rel * lg), 0.0) for lg in log_gammas]
    q_decays = [jnp.exp((trow + 1.0) * lg) for lg in log_gammas]
    k_decays = [jnp.exp((c - 1.0 - trow) * lg) * k_scale for lg in log_gammas]
    hcols = [slice(h * LANES, (h + 1) * LANES) for h in range(n_heads)]

    def chunk(ic, carry):
        r0 = pl.multiple_of(ic * c, c)
        rows = pl.ds(r0, c)
        cosv = cos_ref[rows, :]
        sinv = sin_ref[rows, :]
        scores, inters, updates, vs = [], [], [], []
        for h, cols in enumerate(hcols):
            q = q_ref[rows, cols].astype(F32)
            k = k_ref[rows, cols].astype(F32)
            v = v_ref[rows, cols]
            qr = q * cosv + pltpu.roll(q, LANES // 2, 1) * sinv
            kr = k * cosv + pltpu.roll(k, LANES // 2, 1) * sinv
            scores.append(_bdot_nt(qr, kr * k_scale))
            inters.append(_bdot_nt(qr * q_decays[h], st_ref[h]))
            updates.append(lax.dot_general(v, (kr * k_decays[h]).astype(BF16), _TN,
                                           preferred_element_type=F32))
            vs.append(v)
        for h, cols in enumerate(hcols):
            o = jnp.dot((scores[h] * dmats[h]).astype(BF16), vs[h],
                        preferred_element_type=F32) + inters[h]
            st_ref[h] = math.exp(c * log_gammas[h]) * st_ref[h] + updates[h]
            o_ref[rows, cols] = _head_out(o, ng, gate_ref[rows, cols].astype(F32))
        return carry

    lax.fori_loop(0, tb // c, chunk, 0, unroll=4)


def _retention(z, cos_t, sin_t, norm_g, *, batch, seq, n_heads, col0, tb=512):
    n_tok = z.shape[0]
    w = n_heads * LANES
    assert seq % tb == 0 and tb % CHUNK == 0
    nt = seq // tb
    cb = col0 // w
    zspec = lambda j: pl.BlockSpec((tb, w), lambda b, t, j=j: (b * nt + t, cb + j))
    tspec = pl.BlockSpec((tb, LANES), lambda b, t: (b * nt + t, 0))
    return pl.pallas_call(
        functools.partial(_retention_body, n_heads=n_heads),
        grid=(batch, nt),
        in_specs=[tspec, tspec,
                  pl.BlockSpec((1, LANES), lambda b, t: (0, 0)),
                  zspec(0), zspec(1), zspec(2), zspec(3)],
        out_specs=pl.BlockSpec((tb, w), lambda b, t: (b * nt + t, 0)),
        out_shape=jax.ShapeDtypeStruct((n_tok, w), BF16),
        scratch_shapes=[pltpu.VMEM((n_heads, HEAD_DV, LANES), F32)],
        compiler_params=_cparams(2),
        name="retention",
    )(cos_t, sin_t, norm_g.reshape(1, LANES), z, z, z, z)


def _gla_body(wsum_ref, wa_ref, ba2_ref, ng_ref, a_ref, q_ref, k_ref, v_ref, gate_ref, o_ref,
              st_ref, lf_ref, *, n_groups):
    c = CHUNK
    tb = q_ref.shape[0]

    @pl.when(pl.program_id(1) == 0)
    def _():
        st_ref[...] = jnp.zeros_like(st_ref)

    x = jnp.dot(a_ref[...], wa_ref[...], preferred_element_type=F32) + ba2_ref[...]
    lf_ref[...] = (jnp.minimum(x, 0.0) - jnp.log1p(jnp.exp(-jnp.abs(x)))) * (1.0 / GLA_TAU)

    masks = _level_masks()
    lane = lax.broadcasted_iota(I32, (1, LANES), 1)
    lane_masks = [lane < LANES // 2, lane >= LANES // 2]
    wsum = wsum_ref[...]
    ng = ng_ref[...]
    q_scale = float(LANES // 2) ** -0.5

    def chunk(ic, carry):
        r0 = pl.multiple_of(ic * c, c)
        rows = pl.ds(r0, c)
        kcols = [slice(g * LANES, (g + 1) * LANES) for g in range(n_groups)]
        vcols = [[slice((2 * g + j) * HEAD_DV, (2 * g + j + 1) * HEAD_DV) for j in range(2)]
                 for g in range(n_groups)]
        outs, new_sts = _gated_chunk(
            [q_ref[rows, kc].astype(F32) * q_scale for kc in kcols],
            [k_ref[rows, kc].astype(F32) for kc in kcols],
            [lf_ref[rows, kc] for kc in kcols],
            [[v_ref[rows, vc] for vc in vcs] for vcs in vcols],
            [lane_masks] * n_groups,
            [st_ref[g] for g in range(n_groups)],
            wsum, masks)
        for g in range(n_groups):
            st_ref[g] = new_sts[g]
            for o, vc in zip(outs[g], vcols[g]):
                o_ref[rows, vc] = _head_out(o, ng, gate_ref[rows, vc].astype(F32))
        return carry

    lax.fori_loop(0, tb // c, chunk, 0, unroll=4)


def _gla(z, wsum, wa_pad, b_a2, norm_g, *, batch, seq, n_heads, col0, col_a, tb=512):
    n_tok = z.shape[0]
    wk = n_heads * (LANES // 2)
    wv = n_heads * HEAD_DV
    assert seq % tb == 0 and tb % CHUNK == 0
    nt = seq // tb
    n_groups = n_heads // 2
    row = lambda b, t: b * nt + t
    const = lambda shape: pl.BlockSpec(shape, lambda b, t: (0, 0))
    return pl.pallas_call(
        functools.partial(_gla_body, n_groups=n_groups),
        grid=(batch, nt),
        in_specs=[const(wsum.shape), const(wa_pad.shape), const((1, wk)), const((1, LANES)),
                  pl.BlockSpec((tb, LANES), lambda b, t: (row(b, t), col_a // LANES)),
                  pl.BlockSpec((tb, wk), lambda b, t: (row(b, t), col0 // wk)),
                  pl.BlockSpec((tb, wk), lambda b, t: (row(b, t), col0 // wk + 1)),
                  pl.BlockSpec((tb, wv), lambda b, t: (row(b, t), (col0 + 2 * wk) // wv)),
                  pl.BlockSpec((tb, wv), lambda b, t: (row(b, t), (col0 + 2 * wk) // wv + 1))],
        out_specs=pl.BlockSpec((tb, wv), lambda b, t: (row(b, t), 0)),
        out_shape=jax.ShapeDtypeStruct((n_tok, wv), BF16),
        scratch_shapes=[pltpu.VMEM((n_groups, HEAD_DV, LANES), F32),
                        pltpu.VMEM((tb, wk), F32)],
        compiler_params=_cparams(2),
        name="gla",
    )(wsum, wa_pad, b_a2.reshape(1, wk), norm_g.reshape(1, LANES), z, z, z, z, z)


def _hgrn_body(wsum_ref, lb_ref, ng_ref, q_ref, f_ref, i_ref, gate_ref, o_ref, st_ref, *, n_heads):
    c = CHUNK
    tb = q_ref.shape[0]

    @pl.when(pl.program_id(2) == 0)
    def _():
        st_ref[...] = jnp.zeros_like(st_ref)

    masks = _level_masks()
    wsum = wsum_ref[...]
    ng = ng_ref[...]
    q_scale = float(LANES) ** -0.5

    def chunk(ic, carry):
        r0 = pl.multiple_of(ic * c, c)
        rows = pl.ds(r0, c)
        hcols = [slice(h * LANES, (h + 1) * LANES) for h in range(n_heads)]
        qs, ks, lfs = [], [], []
        for cols in hcols:
            lb = lb_ref[:, cols]
            f = f_ref[rows, cols].astype(F32)
            t = jnp.tanh(0.5 * f)
            b = 0.5 * (1.0 - lb)
            bt = b * t
            lfs.append(jnp.log((0.5 * (1.0 + lb)) + bt))
            ks.append(b - bt)
            qs.append(_silu(q_ref[rows, cols].astype(F32), q_scale))
        outs, new_sts = _gated_chunk(
            qs, ks, lfs, [[i_ref[rows, cols]] for cols in hcols], [[None]] * n_heads,
            [st_ref[h] for h in range(n_heads)], wsum, masks)
        for h, cols in enumerate(hcols):
            st_ref[h] = new_sts[h]
            o_ref[rows, cols] = _head_out(outs[h][0], ng, gate_ref[rows, cols].astype(F32))
        return carry

    lax.fori_loop(0, tb // c, chunk, 0, unroll=2)


def _hgrn(z, wsum, lb, norm_g, *, batch, seq, n_heads, heads_per_step=8, tb=512):
    n_tok = z.shape[0]
    w = heads_per_step * LANES
    ng_ = n_heads // heads_per_step
    assert seq % tb == 0 and tb % CHUNK == 0
    nt = seq // tb
    zspec = lambda j: pl.BlockSpec((tb, w), lambda b, g, t, j=j: (b * nt + t, j * ng_ + g))
    return pl.pallas_call(
        functools.partial(_hgrn_body, n_heads=heads_per_step),
        grid=(batch, ng_, nt),
        in_specs=[pl.BlockSpec(wsum.shape, lambda b, g, t: (0, 0)),
                  pl.BlockSpec((1, w), lambda b, g, t: (0, g)),
                  pl.BlockSpec((1, LANES), lambda b, g, t: (0, 0)),
                  zspec(0), zspec(1), zspec(2), zspec(3)],
        out_specs=pl.BlockSpec((tb, w), lambda b, g, t: (b * nt + t, g)),
        out_shape=jax.ShapeDtypeStruct((n_tok, n_heads * LANES), BF16),
        scratch_shapes=[pltpu.VMEM((heads_per_step, HEAD_DV, LANES), F32)],
        compiler_params=_cparams(3),
        name="hgrn2",
    )(wsum, lb.reshape(1, n_heads * LANES), norm_g.reshape(1, LANES), z, z, z, z)


def _router_body(h_ref, x_ref, wo_ref, g_ref, w_ref, h1_ref, route_ref, cnt_ref, run_ref):
    tm = h_ref.shape[0]

    @pl.when(pl.program_id(0) == 0)
    def _():
        run_ref[...] = jnp.zeros_like(run_ref)

    ts = ROUTE_SUB
    w = w_ref[...]
    w_hi = w.astype(BF16)
    w_lo = (w - w_hi.astype(F32)).astype(BF16)
    lane = lax.broadcasted_iota(I32, (ts, LANES), 1)
    lane_f = lane.astype(F32)
    ri = lax.broadcasted_iota(I32, (ts, ts), 0)
    ci = lax.broadcasted_iota(I32, (ts, ts), 1)
    lstrict = jnp.where(ri > ci, 1.0, 0.0).astype(BF16)
    neg = -jnp.inf
    subs = [slice(r0, r0 + ts) for r0 in range(0, tm, ts)]
    h1s = [h_ref[rows, :] + jnp.dot(x_ref[rows, :], wo_ref[...], preferred_element_type=F32)
           for rows in subs]
    logits = []
    for rows, h1 in zip(subs, h1s):
        h1_ref[rows, :] = h1
        n = _rms(h1, g_ref[...])
        n_hi = n.astype(BF16)
        n_lo = (n - n_hi.astype(F32)).astype(BF16)
        logits.append(jnp.dot(n_hi, w_hi, preferred_element_type=F32)
                      + jnp.dot(n_hi, w_lo, preferred_element_type=F32)
                      + jnp.dot(n_lo, w_hi, preferred_element_type=F32))
    picks = []
    for lg in logits:
        lg1 = jnp.where(lane < N_EXPERTS, lg, neg)
        m1 = jnp.max(lg1, axis=-1, keepdims=True)
        i1 = jnp.min(jnp.where(lg1 == m1, lane_f, float(LANES)), axis=-1, keepdims=True)
        oh1 = lane_f == i1
        lg2 = jnp.where(oh1, neg, lg1)
        m2 = jnp.max(lg2, axis=-1, keepdims=True)
        i2 = jnp.min(jnp.where(lg2 == m2, lane_f, float(LANES)), axis=-1, keepdims=True)
        oh2 = lane_f == i2
        e2 = jnp.exp(m2 - m1)
        g1 = 1.0 / (1.0 + e2)
        both = jnp.where(oh1, 1.0, 0.0) + jnp.where(oh2, 1.0, 0.0)
        prefix = jnp.dot(lstrict, both.astype(BF16), preferred_element_type=F32)
        picks.append((i1, i2, g1, e2 * g1, oh1, oh2, both, prefix))
    run = run_ref[...]
    for rows, (i1, i2, g1, g2, oh1, oh2, both, prefix) in zip(subs, picks):
        before = prefix + run
        r1 = jnp.sum(jnp.where(oh1, before, 0.0), axis=-1, keepdims=True)
        r2 = jnp.sum(jnp.where(oh2, before, 0.0), axis=-1, keepdims=True)
        run = run + jnp.sum(both, axis=0, keepdims=True)
        out = jnp.where(lane == 0, i1, 0.0)
        out = jnp.where(lane == 1, i2, out)
        out = jnp.where(lane == 2, g1, out)
        out = jnp.where(lane == 3, g2, out)
        out = jnp.where(lane == 4, r1, out)
        out = jnp.where(lane == 5, r2, out)
        route_ref[rows, :] = out
    run_ref[...] = run
    cnt_ref[...] = run


def _router(h, x, w_out, g, w_pad, *, tm=512):
    n_tok, d = h.shape
    return pl.pallas_call(
        _router_body,
        grid=(n_tok // tm,),
        in_specs=[pl.BlockSpec((tm, d), lambda i: (i, 0)),
                  pl.BlockSpec((tm, x.shape[1]), lambda i: (i, 0)),
                  pl.BlockSpec(w_out.shape, lambda i: (0, 0), pipeline_mode=pl.Buffered(1)),
                  pl.BlockSpec((1, d), lambda i: (0, 0)),
                  pl.BlockSpec((d, LANES), lambda i: (0, 0))],
        out_specs=[pl.BlockSpec((tm, d), lambda i: (i, 0)),
                   pl.BlockSpec((tm, LANES), lambda i: (i, 0)),
                   pl.BlockSpec((1, LANES), lambda i: (0, 0))],
        out_shape=[jax.ShapeDtypeStruct((n_tok, d), F32),
                   jax.ShapeDtypeStruct((n_tok, LANES), F32),
                   jax.ShapeDtypeStruct((1, LANES), F32)],
        scratch_shapes=[pltpu.VMEM((1, LANES), F32)],
        compiler_params=_cparams(1),
        name="moe_router",
    )(h, x, w_out, g.reshape(1, d), w_pad)


def _dispatch_body(info_ref, h_ref, g_ref, pos_ref, xs_ref, nbuf, zbuf, idx, sems, isem, *, tm):
    i = pl.program_id(0)
    last = pl.num_programs(0) - 1
    slot = i % 2
    tg = zbuf.shape[0]

    def zero_copy(e):
        return pltpu.make_async_copy(
            zbuf, xs_ref.at[pl.ds(pl.multiple_of(info_ref[e], tg), tg)], sems.at[0])

    @pl.when(i == 0)
    def _():
        zbuf[...] = jnp.zeros_like(zbuf)
        for e in range(2 * N_EXPERTS):
            @pl.when(info_ref[e] >= 0)
            def _():
                zero_copy(e).start()
        for e in range(2 * N_EXPERTS):
            @pl.when(info_ref[e] >= 0)
            def _():
                zero_copy(e).wait()

    def drain(s):
        for _ in range(2):
            pltpu.make_async_copy(nbuf.at[s], xs_ref.at[pl.ds(0, tm)], sems.at[s]).wait()

    @pl.when(i >= 2)
    def _():
        drain(slot)

    idx_copy = pltpu.make_async_copy(pos_ref.at[i], idx, isem)
    idx_copy.start()
    nbuf[slot] = _rms(h_ref[...], g_ref[...])
    idx_copy.wait()

    def issue(j, carry):
        t0 = pl.multiple_of(j * ROW_UNROLL, ROW_UNROLL)
        for u in range(ROW_UNROLL):
            for k in range(2):
                pltpu.make_async_copy(nbuf.at[slot, pl.ds(t0 + u, 1)],
                                      xs_ref.at[pl.ds(idx[k * tm + t0 + u], 1)],
                                      sems.at[slot]).start(priority=k)
        return carry

    lax.fori_loop(0, tm // ROW_UNROLL, issue, 0)

    @pl.when(i == last)
    def _():
        drain(slot)

        @pl.when(i >= 1)
        def _():
            drain(1 - slot)


def _dispatch(h, g, pos_tiles, info, rows_pad, *, tm, tg):
    n_tok, d = h.shape
    return pl.pallas_call(
        functools.partial(_dispatch_body, tm=tm),
        grid_spec=pltpu.PrefetchScalarGridSpec(
            num_scalar_prefetch=1,
            grid=(n_tok // tm,),
            in_specs=[pl.BlockSpec((tm, d), lambda i, info: (i, 0)),
                      pl.BlockSpec((1, d), lambda i, info: (0, 0)),
                      pl.BlockSpec(memory_space=pl.ANY)],
            out_specs=pl.BlockSpec(memory_space=pl.ANY),
            scratch_shapes=[pltpu.VMEM((2, tm, d), F32),
                            pltpu.VMEM((tg, d), F32),
                            pltpu.SMEM((2 * tm,), I32),
                            pltpu.SemaphoreType.DMA((2,)),
                            pltpu.SemaphoreType.DMA]),
        out_shape=jax.ShapeDtypeStruct((rows_pad, d), F32),
        compiler_params=_cparams(1),
        name="moe_dispatch",
    )(info, h, g.reshape(1, d), pos_tiles)


def _gmm_body(te_ref, nu_ref, x_ref, wg_ref, wu_ref, wd_ref, y_ref, *, ff_chunk):
    @pl.when(pl.program_id(0) >= nu_ref[0])
    def _():
        y_ref[...] = jnp.zeros_like(y_ref)

    @pl.when(pl.program_id(0) < nu_ref[0])
    def _():
        x = x_ref[...].astype(BF16)
        acc = None
        for c in range(0, wg_ref.shape[1], ff_chunk):
            a = jnp.dot(x, wg_ref[:, c:c + ff_chunk], preferred_element_type=F32)
            b = jnp.dot(x, wu_ref[:, c:c + ff_chunk], preferred_element_type=F32)
            hid = (_silu(a) * b).astype(BF16)
            part = jnp.dot(hid, wd_ref[c:c + ff_chunk, :], preferred_element_type=F32)
            acc = part if acc is None else acc + part
        y_ref[...] = acc


def _gmm(xs, tile_expert, n_used, wg, wu, wd, layer, *, tm, ff_chunk=7 * MXU_DIM):
    rows_pad, d = xs.shape
    ff = wg.shape[3]
    assert ff % ff_chunk == 0 and ff_chunk % MXU_DIM == 0
    tile = lambda i, te, nu: jnp.minimum(i, nu[0] - 1)
    wspec = lambda shape: pl.BlockSpec(
        (None, None) + shape, lambda i, te, nu: (layer, te[tile(i, te, nu)], 0, 0),
        pipeline_mode=pl.Buffered(1))
    return pl.pallas_call(
        functools.partial(_gmm_body, ff_chunk=ff_chunk),
        grid_spec=pltpu.PrefetchScalarGridSpec(
            num_scalar_prefetch=2,
            grid=(rows_pad // tm,),
            in_specs=[pl.BlockSpec((tm, d), lambda i, te, nu: (tile(i, te, nu), 0)),
                      wspec((d, ff)), wspec((d, ff)), wspec((ff, d))],
            out_specs=pl.BlockSpec((tm, d), lambda i, te, nu: (i, 0))),
        out_shape=jax.ShapeDtypeStruct((rows_pad, d), F32),
        compiler_params=_cparams(1),
        name="moe_experts",
    )(tile_expert, n_used, xs, wg, wu, wd)


def _combine_body(h_ref, route_ref, fg_ref, pos_ref, ys_ref, o_ref, gbuf, idx, sems, isem, *,
                  tm, final_norm):
    i = pl.program_id(0)
    slot = i % 2

    def gather(tile, s):
        idx_copy = pltpu.make_async_copy(pos_ref.at[tile], idx, isem)
        idx_copy.start()
        idx_copy.wait()

        def issue(j, carry):
            t0 = pl.multiple_of(j * ROW_UNROLL, ROW_UNROLL)
            for u in range(ROW_UNROLL):
                for k in range(2):
                    pltpu.make_async_copy(ys_ref.at[pl.ds(idx[k * tm + t0 + u], 1)],
                                          gbuf.at[s, k, pl.ds(t0 + u, 1)],
                                          sems.at[s]).start(priority=k)
            return carry

        lax.fori_loop(0, tm // ROW_UNROLL, issue, 0)

    @pl.when(i == 0)
    def _():
        gather(0, 0)

    @pl.when(i + 1 < pl.num_programs(0))
    def _():
        gather(i + 1, 1 - slot)

    for k in range(2):
        pltpu.make_async_copy(ys_ref.at[pl.ds(0, tm)], gbuf.at[slot, k], sems.at[slot]).wait()

    route = route_ref[...]
    out = h_ref[...] + route[:, 2:3] * gbuf[slot, 0] + route[:, 3:4] * gbuf[slot, 1]
    if final_norm:
        out = _rms(out, fg_ref[...])
    o_ref[...] = out


def _combine(h, route, final_g, pos_tiles, ys, *, tm, final_norm):
    n_tok, d = h.shape
    return pl.pallas_call(
        functools.partial(_combine_body, tm=tm, final_norm=final_norm),
        grid=(n_tok // tm,),
        in_specs=[pl.BlockSpec((tm, d), lambda i: (i, 0)),
                  pl.BlockSpec((tm, LANES), lambda i: (i, 0)),
                  pl.BlockSpec((1, d), lambda i: (0, 0)),
                  pl.BlockSpec(memory_space=pl.ANY),
                  pl.BlockSpec(memory_space=pl.ANY)],
        out_specs=pl.BlockSpec((tm, d), lambda i: (i, 0)),
        out_shape=jax.ShapeDtypeStruct((n_tok, d), F32),
        scratch_shapes=[pltpu.VMEM((2, 2, tm, d), F32),
                        pltpu.SMEM((2 * tm,), I32),
                        pltpu.SemaphoreType.DMA((2,)),
                        pltpu.SemaphoreType.DMA],
        compiler_params=_cparams(1),
        name="moe_combine",
    )(h, route, final_g.reshape(1, d), pos_tiles, ys)


def _moe(h, x, w_out, norm_g, w_router, wg, wu, wd, layer, final_g, *, final_norm, tm_route=512,
         tm_gmm=512, tm_rows=256):
    n_tok, d = h.shape
    w_pad = jnp.zeros((d, LANES), F32).at[:, :N_EXPERTS].set(w_router.astype(F32))
    h, route, counts = _router(h, x, w_out, norm_g, w_pad, tm=tm_route)

    cnt = counts[0, :N_EXPERTS].astype(I32)
    gsz = ((cnt + tm_gmm - 1) // tm_gmm) * tm_gmm
    ends = jnp.cumsum(gsz)
    offs = ends - gsz
    rows_pad = (n_tok * 2 // tm_gmm + N_EXPERTS) * tm_gmm
    n_tiles = rows_pad // tm_gmm
    tile_expert = jnp.minimum(
        jnp.searchsorted(ends, jnp.arange(n_tiles, dtype=I32) * tm_gmm, side="right"),
        N_EXPERTS - 1).astype(I32)
    n_used = jnp.maximum(ends[-1:] // tm_gmm, 1).astype(I32)
    tail = ends[-1] + jnp.arange(N_EXPERTS, dtype=I32) * tm_gmm
    info = jnp.concatenate([jnp.where(gsz > 0, ends - tm_gmm, -1),
                            jnp.where(tail < rows_pad, tail, -1)]).astype(I32)

    e01 = route[:, 0:2].astype(I32)
    pos = jnp.clip(offs[jnp.clip(e01, 0, N_EXPERTS - 1)] + route[:, 4:6].astype(I32),
                   0, rows_pad - 1)
    pos_tiles = pos.reshape(n_tok // tm_rows, tm_rows, 2).transpose(0, 2, 1).reshape(
        n_tok // tm_rows, 2 * tm_rows)

    xs = _dispatch(h, norm_g, pos_tiles, info, rows_pad, tm=tm_rows, tg=tm_gmm)
    ys = _gmm(xs, tile_expert, n_used, wg, wu, wd, layer, tm=tm_gmm)
    return _combine(h, route, final_g, pos_tiles, ys, tm=tm_rows, final_norm=final_norm)


def kernel(x, positions, norm_mix_g, norm_ffn_g, final_norm_g, ab_w_in, gla_w_a2, gla_b_a2,
           ret_norm_g, gla_norm_g, ab_w_out, ffn_w_gate, ffn_w_up, ffn_w_down,
           hgrn_lb_logits, c_w_in, hgrn_norm_g, c_w_out, moe_router, moe_w_gate,
           moe_w_up, moe_w_down):
    batch, seq, d = x.shape
    depth = norm_mix_g.shape[0]
    n_tok = batch * seq
    ret_heads = 4
    gla_heads = 4
    gla_rank = gla_w_a2.shape[1]
    hgrn_heads = d // LANES
    ab_cols = ab_w_in.shape[2] - gla_rank

    lb_cum = jnp.cumsum(jax.nn.softmax(hgrn_lb_logits.astype(F32), axis=0), axis=0)
    lower_bounds = lb_cum - lb_cum[0:1]

    wsum = jnp.asarray(_decay_sum_matrix(), BF16)
    inv_freq = ROPE_BASE ** (-jnp.arange(0, LANES, 2, dtype=F32) / LANES)
    inv_freq2 = jnp.concatenate([inv_freq, inv_freq]).reshape(1, LANES)
    cos_t, sin_t = _rope_tables(positions.reshape(n_tok, 1), inv_freq2)

    moe_wg, moe_wu, moe_wd = (w.astype(BF16) for w in (moe_w_gate, moe_w_up, moe_w_down))

    h = x.reshape(n_tok, d)
    for l in range(depth):
        e = l // 2
        if l % 2 == 0:
            w_in = jnp.pad(ab_w_in[e], ((0, 0), (0, LANES - gla_rank))).astype(BF16)
            z = _norm_proj(h, norm_mix_g[l], w_in)
            oa = _retention(z, cos_t, sin_t, ret_norm_g[e], batch=batch, seq=seq,
                            n_heads=ret_heads, col0=0)
            wa_pad = jnp.pad(gla_w_a2[e], ((0, LANES - gla_rank), (0, 0))).astype(BF16)
            ob = _gla(z, wsum, wa_pad, gla_b_a2[e], gla_norm_g[e], batch=batch, seq=seq,
                      n_heads=gla_heads, col0=4 * ret_heads * LANES, col_a=ab_cols)
            w_out = ab_w_out[e].astype(BF16)
            half = ret_heads * HEAD_DV
            h = _ffn(h, [oa, ob], [w_out[:half], w_out[half:]], norm_ffn_g[l],
                     ffn_w_gate[e].astype(BF16), ffn_w_up[e].astype(BF16),
                     ffn_w_down[e].astype(BF16))
        else:
            z = _norm_proj(h, norm_mix_g[l], c_w_in[e].astype(BF16))
            o = _hgrn(z, wsum, lower_bounds[l], hgrn_norm_g[e], batch=batch, seq=seq,
                      n_heads=hgrn_heads)
            h = _moe(h, o, c_w_out[e].astype(BF16), norm_ffn_g[l], moe_router[e],
                     moe_wg, moe_wu, moe_wd, e, final_norm_g, final_norm=(l == depth - 1))
    if depth % 2 == 1:
        raise NotImplementedError("final norm is fused into the last (odd) layer")
    return h.reshape(batch, seq, d)
```

```python
import functools
import math

import numpy as np
import jax
import jax.numpy as jnp
from jax import lax
from jax.experimental import pallas as pl
from jax.experimental.pallas import tpu as pltpu

F32 = jnp.float32
BF16 = jnp.bfloat16
I32 = jnp.int32

EPS = 1e-6
LANES = 128
MXU_DIM = 256
CHUNK = 64
N_LEVELS = 6
HEAD_DV = 128
ROPE_BASE = 10000.0
GLA_TAU = 16.0
LOG2_E = 1.4426950408889634
N_EXPERTS = 8
ROUTE_SUB = 128
ROW_UNROLL = 8
VMEM_LIMIT = 56 * 1024 * 1024

_NT = (((1,), (1,)), ((), ()))
_TN = (((0,), (0,)), ((), ()))


def _cparams(n_axes):
    return pltpu.CompilerParams(dimension_semantics=("arbitrary",) * n_axes,
                                vmem_limit_bytes=VMEM_LIMIT)


def _rms(x, g):
    ms = jnp.mean(x * x, axis=-1, keepdims=True)
    return x * lax.rsqrt(ms + EPS) * g


def _silu(x, scale=1.0):
    t = jnp.tanh(0.5 * x)
    hx = x * (0.5 * scale)
    return hx + hx * t


def _bdot_nt(a, b):
    return lax.dot_general(a.astype(BF16), b.astype(BF16), _NT, preferred_element_type=F32)


def _bdot_tn(a, b):
    return lax.dot_general(a.astype(BF16), b.astype(BF16), _TN, preferred_element_type=F32)


def _norm_proj_body(h_ref, g_ref, w_ref, z_ref, *, col_chunk):
    n = _rms(h_ref[...], g_ref[...]).astype(BF16)
    m = w_ref.shape[1]
    for c in range(0, m, col_chunk):
        ce = min(c + col_chunk, m)
        z_ref[:, c:ce] = jnp.dot(n, w_ref[:, c:ce], preferred_element_type=F32).astype(z_ref.dtype)


def _norm_proj(h, g, w, *, tm=512, col_chunk=512):
    n_tok, d = h.shape
    m = w.shape[1]
    assert n_tok % tm == 0
    return pl.pallas_call(
        functools.partial(_norm_proj_body, col_chunk=col_chunk),
        grid=(n_tok // tm,),
        in_specs=[pl.BlockSpec((tm, d), lambda i: (i, 0)),
                  pl.BlockSpec((1, d), lambda i: (0, 0)),
                  pl.BlockSpec((d, m), lambda i: (0, 0), pipeline_mode=pl.Buffered(1))],
        out_specs=pl.BlockSpec((tm, m), lambda i: (i, 0)),
        out_shape=jax.ShapeDtypeStruct((n_tok, m), BF16),
        compiler_params=_cparams(1),
        name="norm_proj",
    )(h, g.reshape(1, d), w)


def _ffn_body(*refs, n_in, ff_chunk):
    h_ref = refs[0]
    x_refs = refs[1:1 + n_in]
    w_refs = refs[1 + n_in:1 + 2 * n_in]
    g_ref, wg_ref, wu_ref, wd_ref, o_ref = refs[1 + 2 * n_in:]
    h = h_ref[...]
    for x_ref, w_ref in zip(x_refs, w_refs):
        h = h + jnp.dot(x_ref[...], w_ref[...], preferred_element_type=F32)
    o_ref[...] = h
    n = _rms(h, g_ref[...]).astype(BF16)
    acc = None
    ff = wg_ref.shape[1]
    for c in range(0, ff, ff_chunk):
        ce = min(c + ff_chunk, ff)
        a = jnp.dot(n, wg_ref[:, c:ce], preferred_element_type=F32)
        b = jnp.dot(n, wu_ref[:, c:ce], preferred_element_type=F32)
        hid = (_silu(a) * b).astype(BF16)
        part = jnp.dot(hid, wd_ref[c:ce, :], preferred_element_type=F32)
        acc = part if acc is None else acc + part
    o_ref[...] = o_ref[...] + acc


def _ffn(h, xs, ws, g, wg, wu, wd, *, tm=512, ff_chunk=6 * MXU_DIM):
    n_tok, d = h.shape
    ff = wg.shape[1]
    assert ff % MXU_DIM == 0
    const = lambda shape: pl.BlockSpec(shape, lambda i: (0, 0), pipeline_mode=pl.Buffered(1))
    in_specs = [pl.BlockSpec((tm, d), lambda i: (i, 0))]
    in_specs += [pl.BlockSpec((tm, x.shape[1]), lambda i: (i, 0)) for x in xs]
    in_specs += [const(w.shape) for w in ws]
    in_specs += [pl.BlockSpec((1, d), lambda i: (0, 0)), const((d, ff)), const((d, ff)), const((ff, d))]
    return pl.pallas_call(
        functools.partial(_ffn_body, n_in=len(xs), ff_chunk=ff_chunk),
        grid=(n_tok // tm,),
        in_specs=in_specs,
        out_specs=pl.BlockSpec((tm, d), lambda i: (i, 0)),
        out_shape=jax.ShapeDtypeStruct((n_tok, d), F32),
        compiler_params=_cparams(1),
        name="ffn_swiglu",
    )(h, *xs, *ws, g.reshape(1, d), wg, wu, wd)


def _decay_sum_matrix():
    c = CHUNK
    r = np.arange(c)[:, None]
    t = np.arange(c)[None, :]
    blocks = [(t <= r), (t > r)]
    for lvl in range(N_LEVELS):
        s = c >> (lvl + 1)
        m = (r // (2 * s)) * (2 * s) + s - 1
        upper = r > m
        blocks.append(np.where(upper, (t > m) & (t <= r), (t > r) & (t <= m)))
    w = np.concatenate(blocks, axis=0).astype(np.float32)
    return np.concatenate([w, w], axis=1)


def _level_masks():
    c = CHUNK
    row = lax.broadcasted_iota(I32, (c, LANES), 0)
    ri = lax.broadcasted_iota(I32, (c, c), 0)
    ci = lax.broadcasted_iota(I32, (c, c), 1)
    uppers, pairs = [], []
    for lvl in range(N_LEVELS):
        s = c >> (lvl + 1)
        uppers.append((row // s) % 2 == 1)
        pairs.append(((ri // (2 * s)) == (ci // (2 * s)))
                     & ((ri // s) % 2 == 1) & ((ci // s) % 2 == 0))
    return uppers, pairs, ri == ci


def _gated_chunk(qs, ks, lfs, vss, lmss, sts, wsum2, masks):
    c = CHUNK
    uppers, pairs, eye = masks
    n_g = len(qs)
    pieces = []
    for lf in lfs:
        lf2 = lf * LOG2_E
        hi = lf2.astype(BF16)
        lo = (lf2 - hi.astype(F32)).astype(BF16)
        pieces.append(jnp.concatenate([hi, lo], axis=0))
    ex = jnp.dot(wsum2, jnp.concatenate(pieces, axis=1), preferred_element_type=F32)
    fac = jnp.exp2(ex)
    fac_b = fac.astype(BF16)

    grams, inters, updates, diags = [], [], [], []
    for g in range(n_g):
        cols = slice(g * LANES, (g + 1) * LANES)
        q_b = qs[g].astype(BF16)
        k_b = ks[g].astype(BF16)
        w = [jnp.where(uppers[l], q_b, k_b) * fac_b[(2 + l) * c:(3 + l) * c, cols]
             for l in range(N_LEVELS)]
        qf = q_b * fac_b[0:c, cols]
        kf = k_b * fac_b[c:2 * c, cols]
        st_b = sts[g].astype(BF16)
        qk = qs[g] * ks[g]
        for v, lm in zip(vss[g], lmss[g]):
            if lm is None:
                sel = lambda a: a
            else:
                sel = lambda a, lm=lm: jnp.where(lm, a, jnp.zeros_like(a))
            grams.append([lax.dot_general(sel(w[l]), w[l], _NT, preferred_element_type=F32)
                          for l in range(N_LEVELS)])
            inters.append(lax.dot_general(sel(qf), st_b, _NT, preferred_element_type=F32))
            updates.append(lax.dot_general(v, sel(kf), _TN, preferred_element_type=F32))
            diags.append(jnp.sum(sel(qk), axis=-1, keepdims=True))

    outs, new_sts = [], []
    h = 0
    for g in range(n_g):
        cols = slice(g * LANES, (g + 1) * LANES)
        new_st = sts[g] * fac[c - 1:c, cols]
        outs_g = []
        for v in vss[g]:
            scores = jnp.where(eye, diags[h], 0.0)
            for l in range(N_LEVELS):
                scores = jnp.where(pairs[l], grams[h][l], scores)
            outs_g.append(jnp.dot(scores.astype(BF16), v, preferred_element_type=F32) + inters[h])
            new_st = new_st + updates[h]
            h += 1
        outs.append(outs_g)
        new_sts.append(new_st)
    return outs, new_sts


def _head_out(o, norm_g, gate):
    return (_rms(o, norm_g) * _silu(gate)).astype(BF16)


def _rope_body(pos_ref, invf_ref, cos_ref, sin_ref):
    ang = pos_ref[...].astype(F32) * invf_ref[...]
    lane = lax.broadcasted_iota(I32, (1, LANES), 1)
    cos_ref[...] = jnp.cos(ang)
    sin_ref[...] = jnp.sin(ang) * jnp.where(lane < LANES // 2, -1.0, 1.0)


def _rope_tables(pos_col, inv_freq2, *, tm=2048):
    n_tok = pos_col.shape[0]
    tm = min(tm, n_tok)
    table = jax.ShapeDtypeStruct((n_tok, LANES), F32)
    return pl.pallas_call(
        _rope_body,
        grid=(n_tok // tm,),
        in_specs=[pl.BlockSpec((tm, 1), lambda i: (i, 0)),
                  pl.BlockSpec((1, LANES), lambda i: (0, 0))],
        out_specs=[pl.BlockSpec((tm, LANES), lambda i: (i, 0))] * 2,
        out_shape=[table, table],
        compiler_params=_cparams(1),
        name="rope_tables",
    )(pos_col, inv_freq2)


def _retention_body(cos_ref, sin_ref, ng_ref, q_ref, k_ref, v_ref, gate_ref, o_ref, st_ref, *,
                    n_heads):
    c = CHUNK
    tb = q_ref.shape[0]

    @pl.when(pl.program_id(1) == 0)
    def _():
        st_ref[...] = jnp.zeros_like(st_ref)

    ri = lax.broadcasted_iota(I32, (c, c), 0)
    ci = lax.broadcasted_iota(I32, (c, c), 1)
    rel = (ri - ci).astype(F32)
    trow = lax.broadcasted_iota(I32, (c, LANES), 0).astype(F32)
    k_scale = float(LANES) ** -0.5
    ng = ng_ref[...]
    log_gammas = [math.log1p(-(2.0 ** (-5 - h))) for h in range(n_heads)]
    dmats = [jnp.where(rel >= 0, jnp.exp(rel * lg), 0.0) for lg in log_gammas]
    q_decays = [jnp.exp((trow + 1.0) * lg) for lg in log_gammas]
    k_decays = [jnp.exp((c - 1.0 - trow) * lg) * k_scale for lg in log_gammas]
    hcols = [slice(h * LANES, (h + 1) * LANES) for h in range(n_heads)]

    def chunk(ic, carry):
        r0 = pl.multiple_of(ic * c, c)
        rows = pl.ds(r0, c)
        cosv = cos_ref[rows, :]
        sinv = sin_ref[rows, :]
        scores, inters, updates, vs = [], [], [], []
        for h, cols in enumerate(hcols):
            q = q_ref[rows, cols].astype(F32)
            k = k_ref[rows, cols].astype(F32)
            v = v_ref[rows, cols]
            qr = q * cosv + pltpu.roll(q, LANES // 2, 1) * sinv
            kr = k * cosv + pltpu.roll(k, LANES // 2, 1) * sinv
            scores.append(_bdot_nt(qr, kr * k_scale))
            inters.append(_bdot_nt(qr * q_decays[h], st_ref[h]))
            updates.append(lax.dot_general(v, (kr * k_decays[h]).astype(BF16), _TN,
                                           preferred_element_type=F32))
            vs.append(v)
        for h, cols in enumerate(hcols):
            o = jnp.dot((scores[h] * dmats[h]).astype(BF16), vs[h],
                        preferred_element_type=F32) + inters[h]
            st_ref[h] = math.exp(c * log_gammas[h]) * st_ref[h] + updates[h]
            o_ref[rows, cols] = _head_out(o, ng, gate_ref[rows, cols].astype(F32))
        return carry

    lax.fori_loop(0, tb // c, chunk, 0, unroll=4)


def _retention(z, cos_t, sin_t, norm_g, *, batch, seq, n_heads, col0, tb=512):
    n_tok = z.shape[0]
    w = n_heads * LANES
    assert seq % tb == 0 and tb % CHUNK == 0
    nt = seq // tb
    cb = col0 // w
    zspec = lambda j: pl.BlockSpec((tb, w), lambda b, t, j=j: (b * nt + t, cb + j))
    tspec = pl.BlockSpec((tb, LANES), lambda b, t: (b * nt + t, 0))
    return pl.pallas_call(
        functools.partial(_retention_body, n_heads=n_heads),
        grid=(batch, nt),
        in_specs=[tspec, tspec,
                  pl.BlockSpec((1, LANES), lambda b, t: (0, 0)),
                  zspec(0), zspec(1), zspec(2), zspec(3)],
        out_specs=pl.BlockSpec((tb, w), lambda b, t: (b * nt + t, 0)),
        out_shape=jax.ShapeDtypeStruct((n_tok, w), BF16),
        scratch_shapes=[pltpu.VMEM((n_heads, HEAD_DV, LANES), F32)],
        compiler_params=_cparams(2),
        name="retention",
    )(cos_t, sin_t, norm_g.reshape(1, LANES), z, z, z, z)


def _gla_body(wsum_ref, wa_ref, ba2_ref, ng_ref, a_ref, q_ref, k_ref, v_ref, gate_ref, o_ref,
              st_ref, lf_ref, *, n_groups):
    c = CHUNK
    tb = q_ref.shape[0]

    @pl.when(pl.program_id(1) == 0)
    def _():
        st_ref[...] = jnp.zeros_like(st_ref)

    x = jnp.dot(a_ref[...], wa_ref[...], preferred_element_type=F32) + ba2_ref[...]
    lf_ref[...] = (jnp.minimum(x, 0.0) - jnp.log1p(jnp.exp(-jnp.abs(x)))) * (1.0 / GLA_TAU)

    masks = _level_masks()
    lane = lax.broadcasted_iota(I32, (1, LANES), 1)
    lane_masks = [lane < LANES // 2, lane >= LANES // 2]
    wsum = wsum_ref[...]
    ng = ng_ref[...]
    q_scale = float(LANES // 2) ** -0.5

    def chunk(ic, carry):
        r0 = pl.multiple_of(ic * c, c)
        rows = pl.ds(r0, c)
        kcols = [slice(g * LANES, (g + 1) * LANES) for g in range(n_groups)]
        vcols = [[slice((2 * g + j) * HEAD_DV, (2 * g + j + 1) * HEAD_DV) for j in range(2)]
                 for g in range(n_groups)]
        outs, new_sts = _gated_chunk(
            [q_ref[rows, kc].astype(F32) * q_scale for kc in kcols],
            [k_ref[rows, kc].astype(F32) for kc in kcols],
            [lf_ref[rows, kc] for kc in kcols],
            [[v_ref[rows, vc] for vc in vcs] for vcs in vcols],
            [lane_masks] * n_groups,
            [st_ref[g] for g in range(n_groups)],
            wsum, masks)
        for g in range(n_groups):
            st_ref[g] = new_sts[g]
            for o, vc in zip(outs[g], vcols[g]):
                o_ref[rows, vc] = _head_out(o, ng, gate_ref[rows, vc].astype(F32))
        return carry

    lax.fori_loop(0, tb // c, chunk, 0, unroll=8)


def _gla(z, wsum, wa_pad, b_a2, norm_g, *, batch, seq, n_heads, col0, col_a, tb=512):
    n_tok = z.shape[0]
    wk = n_heads * (LANES // 2)
    wv = n_heads * HEAD_DV
    assert seq % tb == 0 and tb % CHUNK == 0
    nt = seq // tb
    n_groups = n_heads // 2
    row = lambda b, t: b * nt + t
    const = lambda shape: pl.BlockSpec(shape, lambda b, t: (0, 0))
    return pl.pallas_call(
        functools.partial(_gla_body, n_groups=n_groups),
        grid=(batch, nt),
        in_specs=[const(wsum.shape), const(wa_pad.shape), const((1, wk)), const((1, LANES)),
                  pl.BlockSpec((tb, LANES), lambda b, t: (row(b, t), col_a // LANES)),
                  pl.BlockSpec((tb, wk), lambda b, t: (row(b, t), col0 // wk)),
                  pl.BlockSpec((tb, wk), lambda b, t: (row(b, t), col0 // wk + 1)),
                  pl.BlockSpec((tb, wv), lambda b, t: (row(b, t), (col0 + 2 * wk) // wv)),
                  pl.BlockSpec((tb, wv), lambda b, t: (row(b, t), (col0 + 2 * wk) // wv + 1))],
        out_specs=pl.BlockSpec((tb, wv), lambda b, t: (row(b, t), 0)),
        out_shape=jax.ShapeDtypeStruct((n_tok, wv), BF16),
        scratch_shapes=[pltpu.VMEM((n_groups, HEAD_DV, LANES), F32),
                        pltpu.VMEM((tb, wk), F32)],
        compiler_params=_cparams(2),
        name="gla",
    )(wsum, wa_pad, b_a2.reshape(1, wk), norm_g.reshape(1, LANES), z, z, z, z, z)


def _hgrn_body(wsum_ref, lb_ref, ng_ref, q_ref, f_ref, i_ref, gate_ref, o_ref, st_ref, *, n_heads):
    c = CHUNK
    tb = q_ref.shape[0]

    @pl.when(pl.program_id(2) == 0)
    def _():
        st_ref[...] = jnp.zeros_like(st_ref)

    masks = _level_masks()
    wsum = wsum_ref[...]
    ng = ng_ref[...]
    q_scale = float(LANES) ** -0.5

    def chunk(ic, carry):
        r0 = pl.multiple_of(ic * c, c)
        rows = pl.ds(r0, c)
        hcols = [slice(h * LANES, (h + 1) * LANES) for h in range(n_heads)]
        qs, ks, lfs = [], [], []
        for cols in hcols:
            lb = lb_ref[:, cols]
            f = f_ref[rows, cols].astype(F32)
            t = jnp.tanh(0.5 * f)
            b = 0.5 * (1.0 - lb)
            bt = b * t
            lfs.append(jnp.log((0.5 * (1.0 + lb)) + bt))
            ks.append(b - bt)
            qs.append(_silu(q_ref[rows, cols].astype(F32), q_scale))
        outs, new_sts = _gated_chunk(
            qs, ks, lfs, [[i_ref[rows, cols]] for cols in hcols], [[None]] * n_heads,
            [st_ref[h] for h in range(n_heads)], wsum, masks)
        for h, cols in enumerate(hcols):
            st_ref[h] = new_sts[h]
            o_ref[rows, cols] = _head_out(outs[h][0], ng, gate_ref[rows, cols].astype(F32))
        return carry

    lax.fori_loop(0, tb // c, chunk, 0, unroll=2)


def _hgrn(z, wsum, lb, norm_g, *, batch, seq, n_heads, heads_per_step=8, tb=512):
    n_tok = z.shape[0]
    w = heads_per_step * LANES
    ng_ = n_heads // heads_per_step
    assert seq % tb == 0 and tb % CHUNK == 0
    nt = seq // tb
    zspec = lambda j: pl.BlockSpec((tb, w), lambda b, g, t, j=j: (b * nt + t, j * ng_ + g))
    return pl.pallas_call(
        functools.partial(_hgrn_body, n_heads=heads_per_step),
        grid=(batch, ng_, nt),
        in_specs=[pl.BlockSpec(wsum.shape, lambda b, g, t: (0, 0)),
                  pl.BlockSpec((1, w), lambda b, g, t: (0, g)),
                  pl.BlockSpec((1, LANES), lambda b, g, t: (0, 0)),
                  zspec(0), zspec(1), zspec(2), zspec(3)],
        out_specs=pl.BlockSpec((tb, w), lambda b, g, t: (b * nt + t, g)),
        out_shape=jax.ShapeDtypeStruct((n_tok, n_heads * LANES), BF16),
        scratch_shapes=[pltpu.VMEM((heads_per_step, HEAD_DV, LANES), F32)],
        compiler_params=_cparams(3),
        name="hgrn2",
    )(wsum, lb.reshape(1, n_heads * LANES), norm_g.reshape(1, LANES), z, z, z, z)


def _router_body(h_ref, x_ref, wo_ref, g_ref, w_ref, h1_ref, route_ref, cnt_ref, run_ref):
    tm = h_ref.shape[0]

    @pl.when(pl.program_id(0) == 0)
    def _():
        run_ref[...] = jnp.zeros_like(run_ref)

    ts = ROUTE_SUB
    w = w_ref[...]
    w_hi = w.astype(BF16)
    w_lo = (w - w_hi.astype(F32)).astype(BF16)
    lane = lax.broadcasted_iota(I32, (ts, LANES), 1)
    lane_f = lane.astype(F32)
    ri = lax.broadcasted_iota(I32, (ts, ts), 0)
    ci = lax.broadcasted_iota(I32, (ts, ts), 1)
    lstrict = jnp.where(ri > ci, 1.0, 0.0).astype(BF16)
    neg = -jnp.inf
    subs = [slice(r0, r0 + ts) for r0 in range(0, tm, ts)]
    h1s = [h_ref[rows, :] + jnp.dot(x_ref[rows, :], wo_ref[...], preferred_element_type=F32)
           for rows in subs]
    logits = []
    for rows, h1 in zip(subs, h1s):
        h1_ref[rows, :] = h1
        n = _rms(h1, g_ref[...])
        n_hi = n.astype(BF16)
        n_lo = (n - n_hi.astype(F32)).astype(BF16)
        logits.append(jnp.dot(n_hi, w_hi, preferred_element_type=F32)
                      + jnp.dot(n_hi, w_lo, preferred_element_type=F32)
                      + jnp.dot(n_lo, w_hi, preferred_element_type=F32))
    picks = []
    for lg in logits:
        lg1 = jnp.where(lane < N_EXPERTS, lg, neg)
        m1 = jnp.max(lg1, axis=-1, keepdims=True)
        i1 = jnp.min(jnp.where(lg1 == m1, lane_f, float(LANES)), axis=-1, keepdims=True)
        oh1 = lane_f == i1
        lg2 = jnp.where(oh1, neg, lg1)
        m2 = jnp.max(lg2, axis=-1, keepdims=True)
        i2 = jnp.min(jnp.where(lg2 == m2, lane_f, float(LANES)), axis=-1, keepdims=True)
        oh2 = lane_f == i2
        e2 = jnp.exp(m2 - m1)
        g1 = 1.0 / (1.0 + e2)
        both = jnp.where(oh1, 1.0, 0.0) + jnp.where(oh2, 1.0, 0.0)
        prefix = jnp.dot(lstrict, both.astype(BF16), preferred_element_type=F32)
        picks.append((i1, i2, g1, e2 * g1, oh1, oh2, both, prefix))
    run = run_ref[...]
    for rows, (i1, i2, g1, g2, oh1, oh2, both, prefix) in zip(subs, picks):
        before = prefix + run
        r1 = jnp.sum(jnp.where(oh1, before, 0.0), axis=-1, keepdims=True)
        r2 = jnp.sum(jnp.where(oh2, before, 0.0), axis=-1, keepdims=True)
        run = run + jnp.sum(both, axis=0, keepdims=True)
        out = jnp.where(lane == 0, i1, 0.0)
        out = jnp.where(lane == 1, i2, out)
        out = jnp.where(lane == 2, g1, out)
        out = jnp.where(lane == 3, g2, out)
        out = jnp.where(lane == 4, r1, out)
        out = jnp.where(lane == 5, r2, out)
        route_ref[rows, :] = out
    run_ref[...] = run
    cnt_ref[...] = run


def _router(h, x, w_out, g, w_pad, *, tm=512):
    n_tok, d = h.shape
    return pl.pallas_call(
        _router_body,
        grid=(n_tok // tm,),
        in_specs=[pl.BlockSpec((tm, d), lambda i: (i, 0)),
                  pl.BlockSpec((tm, x.shape[1]), lambda i: (i, 0)),
                  pl.BlockSpec(w_out.shape, lambda i: (0, 0), pipeline_mode=pl.Buffered(1)),
                  pl.BlockSpec((1, d), lambda i: (0, 0)),
                  pl.BlockSpec((d, LANES), lambda i: (0, 0))],
        out_specs=[pl.BlockSpec((tm, d), lambda i: (i, 0)),
                   pl.BlockSpec((tm, LANES), lambda i: (i, 0)),
                   pl.BlockSpec((1, LANES), lambda i: (0, 0))],
        out_shape=[jax.ShapeDtypeStruct((n_tok, d), F32),
                   jax.ShapeDtypeStruct((n_tok, LANES), F32),
                   jax.ShapeDtypeStruct((1, LANES), F32)],
        scratch_shapes=[pltpu.VMEM((1, LANES), F32)],
        compiler_params=_cparams(1),
        name="moe_router",
    )(h, x, w_out, g.reshape(1, d), w_pad)


def _dispatch_body(info_ref, h_ref, g_ref, pos_ref, xs_ref, nbuf, zbuf, idx, sems, isem, *, tm):
    i = pl.program_id(0)
    last = pl.num_programs(0) - 1
    slot = i % 2
    tg = zbuf.shape[0]

    def zero_copy(e):
        return pltpu.make_async_copy(
            zbuf, xs_ref.at[pl.ds(pl.multiple_of(info_ref[e], tg), tg)], sems.at[0, 0])

    @pl.when(i == 0)
    def _():
        zbuf[...] = jnp.zeros_like(zbuf)
        for e in range(2 * N_EXPERTS):
            @pl.when(info_ref[e] >= 0)
            def _():
                zero_copy(e).start()
        for e in range(2 * N_EXPERTS):
            @pl.when(info_ref[e] >= 0)
            def _():
                zero_copy(e).wait()

    def drain(s):
        for k in range(2):
            pltpu.make_async_copy(nbuf.at[s], xs_ref.at[pl.ds(0, tm)], sems.at[s, k]).wait()

    @pl.when(i >= 2)
    def _():
        drain(slot)

    idx_copy = pltpu.make_async_copy(pos_ref.at[i], idx, isem)
    idx_copy.start()
    nbuf[slot] = _rms(h_ref[...], g_ref[...])
    idx_copy.wait()

    def issue(j, carry):
        t0 = pl.multiple_of(j * ROW_UNROLL, ROW_UNROLL)
        for u in range(ROW_UNROLL):
            for k in range(2):
                pltpu.make_async_copy(nbuf.at[slot, pl.ds(t0 + u, 1)],
                                      xs_ref.at[pl.ds(idx[k * tm + t0 + u], 1)],
                                      sems.at[slot, k]).start()
        return carry

    lax.fori_loop(0, tm // ROW_UNROLL, issue, 0)

    @pl.when(i == last)
    def _():
        drain(slot)

        @pl.when(i >= 1)
        def _():
            drain(1 - slot)


def _dispatch(h, g, pos_tiles, info, rows_pad, *, tm, tg):
    n_tok, d = h.shape
    return pl.pallas_call(
        functools.partial(_dispatch_body, tm=tm),
        grid_spec=pltpu.PrefetchScalarGridSpec(
            num_scalar_prefetch=1,
            grid=(n_tok // tm,),
            in_specs=[pl.BlockSpec((tm, d), lambda i, info: (i, 0)),
                      pl.BlockSpec((1, d), lambda i, info: (0, 0)),
                      pl.BlockSpec(memory_space=pl.ANY)],
            out_specs=pl.BlockSpec(memory_space=pl.ANY),
            scratch_shapes=[pltpu.VMEM((2, tm, d), F32),
                            pltpu.VMEM((tg, d), F32),
                            pltpu.SMEM((2 * tm,), I32),
                            pltpu.SemaphoreType.DMA((2, 2)),
                            pltpu.SemaphoreType.DMA]),
        out_shape=jax.ShapeDtypeStruct((rows_pad, d), F32),
        compiler_params=_cparams(1),
        name="moe_dispatch",
    )(info, h, g.reshape(1, d), pos_tiles)


def _gmm_body(te_ref, nu_ref, x_ref, wg_ref, wu_ref, wd_ref, y_ref, *, ff_chunk):
    @pl.when(pl.program_id(0) >= nu_ref[0])
    def _():
        y_ref[...] = jnp.zeros_like(y_ref)

    @pl.when(pl.program_id(0) < nu_ref[0])
    def _():
        x = x_ref[...].astype(BF16)
        acc = None
        for c in range(0, wg_ref.shape[1], ff_chunk):
            a = jnp.dot(x, wg_ref[:, c:c + ff_chunk], preferred_element_type=F32)
            b = jnp.dot(x, wu_ref[:, c:c + ff_chunk], preferred_element_type=F32)
            hid = (_silu(a) * b).astype(BF16)
            part = jnp.dot(hid, wd_ref[c:c + ff_chunk, :], preferred_element_type=F32)
            acc = part if acc is None else acc + part
        y_ref[...] = acc


def _gmm(xs, tile_expert, n_used, wg, wu, wd, layer, *, tm, ff_chunk=7 * MXU_DIM):
    rows_pad, d = xs.shape
    ff = wg.shape[3]
    assert ff % ff_chunk == 0 and ff_chunk % MXU_DIM == 0
    tile = lambda i, te, nu: jnp.minimum(i, nu[0] - 1)
    wspec = lambda shape: pl.BlockSpec(
        (None, None) + shape, lambda i, te, nu: (layer, te[tile(i, te, nu)], 0, 0),
        pipeline_mode=pl.Buffered(1))
    return pl.pallas_call(
        functools.partial(_gmm_body, ff_chunk=ff_chunk),
        grid_spec=pltpu.PrefetchScalarGridSpec(
            num_scalar_prefetch=2,
            grid=(rows_pad // tm,),
            in_specs=[pl.BlockSpec((tm, d), lambda i, te, nu: (tile(i, te, nu), 0)),
                      wspec((d, ff)), wspec((d, ff)), wspec((ff, d))],
            out_specs=pl.BlockSpec((tm, d), lambda i, te, nu: (i, 0))),
        out_shape=jax.ShapeDtypeStruct((rows_pad, d), F32),
        compiler_params=_cparams(1),
        name="moe_experts",
    )(tile_expert, n_used, xs, wg, wu, wd)


def _combine_body(h_ref, route_ref, fg_ref, pos_ref, ys_ref, o_ref, gbuf, idx, sems, isem, *,
                  tm, final_norm):
    i = pl.program_id(0)
    slot = i % 2

    def gather(tile, s):
        idx_copy = pltpu.make_async_copy(pos_ref.at[tile], idx, isem)
        idx_copy.start()
        idx_copy.wait()

        def issue(j, carry):
            t0 = pl.multiple_of(j * ROW_UNROLL, ROW_UNROLL)
            for u in range(ROW_UNROLL):
                for k in range(2):
                    pltpu.make_async_copy(ys_ref.at[pl.ds(idx[k * tm + t0 + u], 1)],
                                          gbuf.at[s, k, pl.ds(t0 + u, 1)], sems.at[s, k]).start()
            return carry

        lax.fori_loop(0, tm // ROW_UNROLL, issue, 0)

    @pl.when(i == 0)
    def _():
        gather(0, 0)

    @pl.when(i + 1 < pl.num_programs(0))
    def _():
        gather(i + 1, 1 - slot)

    for k in range(2):
        pltpu.make_async_copy(ys_ref.at[pl.ds(0, tm)], gbuf.at[slot, k], sems.at[slot, k]).wait()

    route = route_ref[...]
    out = h_ref[...] + route[:, 2:3] * gbuf[slot, 0] + route[:, 3:4] * gbuf[slot, 1]
    if final_norm:
        out = _rms(out, fg_ref[...])
    o_ref[...] = out


def _combine(h, route, final_g, pos_tiles, ys, *, tm, final_norm):
    n_tok, d = h.shape
    return pl.pallas_call(
        functools.partial(_combine_body, tm=tm, final_norm=final_norm),
        grid=(n_tok // tm,),
        in_specs=[pl.BlockSpec((tm, d), lambda i: (i, 0)),
                  pl.BlockSpec((tm, LANES), lambda i: (i, 0)),
                  pl.BlockSpec((1, d), lambda i: (0, 0)),
                  pl.BlockSpec(memory_space=pl.ANY),
                  pl.BlockSpec(memory_space=pl.ANY)],
        out_specs=pl.BlockSpec((tm, d), lambda i: (i, 0)),
        out_shape=jax.ShapeDtypeStruct((n_tok, d), F32),
        scratch_shapes=[pltpu.VMEM((2, 2, tm, d), F32),
                        pltpu.SMEM((2 * tm,), I32),
                        pltpu.SemaphoreType.DMA((2, 2)),
                        pltpu.SemaphoreType.DMA],
        compiler_params=_cparams(1),
        name="moe_combine",
    )(h, route, final_g.reshape(1, d), pos_tiles, ys)


def _moe(h, x, w_out, norm_g, w_router, wg, wu, wd, layer, final_g, *, final_norm, tm_route=512,
         tm_gmm=512, tm_rows=512):
    n_tok, d = h.shape
    w_pad = jnp.zeros((d, LANES), F32).at[:, :N_EXPERTS].set(w_router.astype(F32))
    h, route, counts = _router(h, x, w_out, norm_g, w_pad, tm=tm_route)

    cnt = counts[0, :N_EXPERTS].astype(I32)
    gsz = ((cnt + tm_gmm - 1) // tm_gmm) * tm_gmm
    ends = jnp.cumsum(gsz)
    offs = ends - gsz
    rows_pad = (n_tok * 2 // tm_gmm + N_EXPERTS) * tm_gmm
    n_tiles = rows_pad // tm_gmm
    tile_expert = jnp.minimum(
        jnp.searchsorted(ends, jnp.arange(n_tiles, dtype=I32) * tm_gmm, side="right"),
        N_EXPERTS - 1).astype(I32)
    n_used = jnp.maximum(ends[-1:] // tm_gmm, 1).astype(I32)
    tail = ends[-1] + jnp.arange(N_EXPERTS, dtype=I32) * tm_gmm
    info = jnp.concatenate([jnp.where(gsz > 0, ends - tm_gmm, -1),
                            jnp.where(tail < rows_pad, tail, -1)]).astype(I32)

    e01 = route[:, 0:2].astype(I32)
    pos = jnp.clip(offs[jnp.clip(e01, 0, N_EXPERTS - 1)] + route[:, 4:6].astype(I32),
                   0, rows_pad - 1)
    pos_tiles = pos.reshape(n_tok // tm_rows, tm_rows, 2).transpose(0, 2, 1).reshape(
        n_tok // tm_rows, 2 * tm_rows)

    xs = _dispatch(h, norm_g, pos_tiles, info, rows_pad, tm=tm_rows, tg=tm_gmm)
    ys = _gmm(xs, tile_expert, n_used, wg, wu, wd, layer, tm=tm_gmm)
    return _combine(h, route, final_g, pos_tiles, ys, tm=tm_rows, final_norm=final_norm)


def kernel(x, positions, norm_mix_g, norm_ffn_g, final_norm_g, ab_w_in, gla_w_a2, gla_b_a2,
           ret_norm_g, gla_norm_g, ab_w_out, ffn_w_gate, ffn_w_up, ffn_w_down,
           hgrn_lb_logits, c_w_in, hgrn_norm_g, c_w_out, moe_router, moe_w_gate,
           moe_w_up, moe_w_down):
    batch, seq, d = x.shape
    depth = norm_mix_g.shape[0]
    n_tok = batch * seq
    ret_heads = 4
    gla_heads = 4
    gla_rank = gla_w_a2.shape[1]
    hgrn_heads = d // LANES
    ab_cols = ab_w_in.shape[2] - gla_rank

    lb_cum = jnp.cumsum(jax.nn.softmax(hgrn_lb_logits.astype(F32), axis=0), axis=0)
    lower_bounds = lb_cum - lb_cum[0:1]

    wsum = jnp.asarray(_decay_sum_matrix(), BF16)
    inv_freq = ROPE_BASE ** (-jnp.arange(0, LANES, 2, dtype=F32) / LANES)
    inv_freq2 = jnp.concatenate([inv_freq, inv_freq]).reshape(1, LANES)
    cos_t, sin_t = _rope_tables(positions.reshape(n_tok, 1), inv_freq2)

    moe_wg, moe_wu, moe_wd = (w.astype(BF16) for w in (moe_w_gate, moe_w_up, moe_w_down))

    h = x.reshape(n_tok, d)
    for l in range(depth):
        e = l // 2
        if l % 2 == 0:
            w_in = jnp.pad(ab_w_in[e], ((0, 0), (0, LANES - gla_rank))).astype(BF16)
            z = _norm_proj(h, norm_mix_g[l], w_in)
            oa = _retention(z, cos_t, sin_t, ret_norm_g[e], batch=batch, seq=seq,
                            n_heads=ret_heads, col0=0)
            wa_pad = jnp.pad(gla_w_a2[e], ((0, LANES - gla_rank), (0, 0))).astype(BF16)
            ob = _gla(z, wsum, wa_pad, gla_b_a2[e], gla_norm_g[e], batch=batch, seq=seq,
                      n_heads=gla_heads, col0=4 * ret_heads * LANES, col_a=ab_cols)
            w_out = ab_w_out[e].astype(BF16)
            half = ret_heads * HEAD_DV
            h = _ffn(h, [oa, ob], [w_out[:half], w_out[half:]], norm_ffn_g[l],
                     ffn_w_gate[e].astype(BF16), ffn_w_up[e].astype(BF16),
                     ffn_w_down[e].astype(BF16))
        else:
            z = _norm_proj(h, norm_mix_g[l], c_w_in[e].astype(BF16))
            o = _hgrn(z, wsum, lower_bounds[l], hgrn_norm_g[e], batch=batch, seq=seq,
                      n_heads=hgrn_heads)
            h = _moe(h, o, c_w_out[e].astype(BF16), norm_ffn_g[l], moe_router[e],
                     moe_wg, moe_wu, moe_wd, e, final_norm_g, final_norm=(l == depth - 1))
    if depth % 2 == 1:
        raise NotImplementedError("final norm is fused into the last (odd) layer")
    return h.reshape(batch, seq, d)
```

```python
import functools
import math

import numpy as np
import jax
import jax.numpy as jnp
from jax import lax
from jax.experimental import pallas as pl
from jax.experimental.pallas import tpu as pltpu

F32 = jnp.float32
BF16 = jnp.bfloat16
I32 = jnp.int32

EPS = 1e-6
LANES = 128
MXU_DIM = 256
CHUNK = 64
N_LEVELS = 6
HEAD_DV = 128
ROPE_BASE = 10000.0
GLA_TAU = 16.0
LOG2_E = 1.4426950408889634
N_EXPERTS = 8
ROUTE_SUB = 128
ROW_UNROLL = 8
VMEM_LIMIT = 56 * 1024 * 1024

_NT = (((1,), (1,)), ((), ()))
_TN = (((0,), (0,)), ((), ()))


def _cparams(n_axes):
    return pltpu.CompilerParams(dimension_semantics=("arbitrary",) * n_axes,
                                vmem_limit_bytes=VMEM_LIMIT)


def _rms(x, g):
    ms = jnp.mean(x * x, axis=-1, keepdims=True)
    return x * lax.rsqrt(ms + EPS) * g


def _silu(x, scale=1.0):
    t = jnp.tanh(0.5 * x)
    hx = x * (0.5 * scale)
    return hx + hx * t


def _bdot_nt(a, b):
    return lax.dot_general(a.astype(BF16), b.astype(BF16), _NT, preferred_element_type=F32)


def _bdot_tn(a, b):
    return lax.dot_general(a.astype(BF16), b.astype(BF16), _TN, preferred_element_type=F32)


def _norm_proj_body(h_ref, g_ref, w_ref, z_ref, *, col_chunk):
    n = _rms(h_ref[...], g_ref[...]).astype(BF16)
    m = w_ref.shape[1]
    for c in range(0, m, col_chunk):
        ce = min(c + col_chunk, m)
        z_ref[:, c:ce] = jnp.dot(n, w_ref[:, c:ce], preferred_element_type=F32).astype(z_ref.dtype)


def _norm_proj(h, g, w, *, tm=512, col_chunk=512):
    n_tok, d = h.shape
    m = w.shape[1]
    assert n_tok % tm == 0
    return pl.pallas_call(
        functools.partial(_norm_proj_body, col_chunk=col_chunk),
        grid=(n_tok // tm,),
        in_specs=[pl.BlockSpec((tm, d), lambda i: (i, 0)),
                  pl.BlockSpec((1, d), lambda i: (0, 0)),
                  pl.BlockSpec((d, m), lambda i: (0, 0), pipeline_mode=pl.Buffered(1))],
        out_specs=pl.BlockSpec((tm, m), lambda i: (i, 0)),
        out_shape=jax.ShapeDtypeStruct((n_tok, m), BF16),
        compiler_params=_cparams(1),
        name="norm_proj",
    )(h, g.reshape(1, d), w)


def _ffn_body(*refs, n_in, ff_chunk):
    h_ref = refs[0]
    x_refs = refs[1:1 + n_in]
    w_refs = refs[1 + n_in:1 + 2 * n_in]
    g_ref, wg_ref, wu_ref, wd_ref, o_ref = refs[1 + 2 * n_in:]
    h = h_ref[...]
    for x_ref, w_ref in zip(x_refs, w_refs):
        h = h + jnp.dot(x_ref[...], w_ref[...], preferred_element_type=F32)
    o_ref[...] = h
    n = _rms(h, g_ref[...]).astype(BF16)
    acc = None
    ff = wg_ref.shape[1]
    for c in range(0, ff, ff_chunk):
        ce = min(c + ff_chunk, ff)
        a = jnp.dot(n, wg_ref[:, c:ce], preferred_element_type=F32)
        b = jnp.dot(n, wu_ref[:, c:ce], preferred_element_type=F32)
        hid = (_silu(a) * b).astype(BF16)
        part = jnp.dot(hid, wd_ref[c:ce, :], preferred_element_type=F32)
        acc = part if acc is None else acc + part
    o_ref[...] = o_ref[...] + acc


def _ffn(h, xs, ws, g, wg, wu, wd, *, tm=512, ff_chunk=6 * MXU_DIM):
    n_tok, d = h.shape
    ff = wg.shape[1]
    assert ff % MXU_DIM == 0
    const = lambda shape: pl.BlockSpec(shape, lambda i: (0, 0), pipeline_mode=pl.Buffered(1))
    in_specs = [pl.BlockSpec((tm, d), lambda i: (i, 0))]
    in_specs += [pl.BlockSpec((tm, x.shape[1]), lambda i: (i, 0)) for x in xs]
    in_specs += [const(w.shape) for w in ws]
    in_specs += [pl.BlockSpec((1, d), lambda i: (0, 0)), const((d, ff)), const((d, ff)), const((ff, d))]
    return pl.pallas_call(
        functools.partial(_ffn_body, n_in=len(xs), ff_chunk=ff_chunk),
        grid=(n_tok // tm,),
        in_specs=in_specs,
        out_specs=pl.BlockSpec((tm, d), lambda i: (i, 0)),
        out_shape=jax.ShapeDtypeStruct((n_tok, d), F32),
        compiler_params=_cparams(1),
        name="ffn_swiglu",
    )(h, *xs, *ws, g.reshape(1, d), wg, wu, wd)


def _decay_sum_matrix():
    c = CHUNK
    r = np.arange(c)[:, None]
    t = np.arange(c)[None, :]
    blocks = [(t <= r), (t > r)]
    for lvl in range(N_LEVELS):
        s = c >> (lvl + 1)
        m = (r // (2 * s)) * (2 * s) + s - 1
        upper = r > m
        blocks.append(np.where(upper, (t > m) & (t <= r), (t > r) & (t <= m)))
    w = np.concatenate(blocks, axis=0).astype(np.float32)
    return np.concatenate([w, w], axis=1)


def _level_masks():
    c = CHUNK
    assert 2 * c == LANES
    row = lax.broadcasted_iota(I32, (c, LANES), 0)
    col = lax.broadcasted_iota(I32, (c, LANES), 1) % c
    uppers, pairs = [], []
    for lvl in range(N_LEVELS):
        s = c >> (lvl + 1)
        uppers.append((row // s) % 2 == 1)
        pairs.append(((row // (2 * s)) == (col // (2 * s)))
                     & ((row // s) % 2 == 1) & ((col // s) % 2 == 0))
    return uppers, pairs, row == col


def _single_head_masks():
    c = CHUNK
    ri = lax.broadcasted_iota(I32, (c, c), 0)
    ci = lax.broadcasted_iota(I32, (c, c), 1)
    pairs = []
    for lvl in range(N_LEVELS):
        s = c >> (lvl + 1)
        pairs.append(((ri // (2 * s)) == (ci // (2 * s)))
                     & ((ri // s) % 2 == 1) & ((ci // s) % 2 == 0))
    return pairs, ri == ci


def _block_diag(a, b):
    z = jnp.zeros_like(a)
    return jnp.concatenate([jnp.concatenate([a, z], axis=1),
                            jnp.concatenate([z, b], axis=1)], axis=0)


def _gated_chunk(qs, ks, lfs, vss, lmss, sts, wsum2, masks, pack_pairs):
    c = CHUNK
    uppers, pairs, eye, pairs1, eye1 = masks
    n_g = len(qs)
    pieces = []
    for lf in lfs:
        lf2 = lf * LOG2_E
        hi = lf2.astype(BF16)
        lo = (lf2 - hi.astype(F32)).astype(BF16)
        pieces.append(jnp.concatenate([hi, lo], axis=0))
    ex = jnp.dot(wsum2, jnp.concatenate(pieces, axis=1), preferred_element_type=F32)
    fac = jnp.exp2(ex)
    fac_b = fac.astype(BF16)

    heads = []
    for g in range(n_g):
        cols = slice(g * LANES, (g + 1) * LANES)
        q_b = qs[g].astype(BF16)
        k_b = ks[g].astype(BF16)
        w = [jnp.where(uppers[l], q_b, k_b) * fac_b[(2 + l) * c:(3 + l) * c, cols]
             for l in range(N_LEVELS)]
        qf = q_b * fac_b[0:c, cols]
        kf = k_b * fac_b[c:2 * c, cols]
        st_b = sts[g].astype(BF16)
        qk = qs[g] * ks[g]
        for v, lm in zip(vss[g], lmss[g]):
            if lm is None:
                sel = lambda a: a
            else:
                sel = lambda a, lm=lm: jnp.where(lm, a, jnp.zeros_like(a))
            heads.append(dict(lhs=w, rhs=[sel(x) for x in w], qf=sel(qf), kf=sel(kf), st=st_b, v=v,
                              diag=jnp.sum(sel(qk), axis=-1, keepdims=True)))

    pair_outs = []
    state_update = lambda a: lax.dot_general(a["v"], a["kf"], _TN, preferred_element_type=F32)
    if pack_pairs:
        assert len(heads) % 2 == 0
        lane_lo = lax.broadcasted_iota(I32, (1, LANES), 1) < c
        updates = [state_update(a) for a in heads]
        grams, inters = [], []
        for a, b in zip(heads[0::2], heads[1::2]):
            grams.append([lax.dot_general(jnp.concatenate([a["lhs"][l], b["lhs"][l]], axis=1),
                                          _block_diag(a["rhs"][l], b["rhs"][l]), _NT,
                                          preferred_element_type=F32) for l in range(N_LEVELS)])
            inters.append(lax.dot_general(jnp.concatenate([a["qf"], b["qf"]], axis=1),
                                          _block_diag(a["st"], b["st"]), _NT,
                                          preferred_element_type=F32))
        for p, (a, b) in enumerate(zip(heads[0::2], heads[1::2])):
            scores = jnp.where(eye, jnp.where(lane_lo, a["diag"], b["diag"]), 0.0)
            for l in range(N_LEVELS):
                scores = jnp.where(pairs[l], grams[p][l], scores)
            o2 = jnp.dot(scores.astype(BF16), _block_diag(a["v"], b["v"]),
                         preferred_element_type=F32) + inters[p]
            pair_outs += [o2[:, :HEAD_DV], o2[:, HEAD_DV:]]
    else:
        grams = [[lax.dot_general(a["lhs"][l], a["rhs"][l], _NT, preferred_element_type=F32)
                  for l in range(N_LEVELS)] for a in heads]
        updates = [state_update(a) for a in heads]
        inters = [lax.dot_general(a["qf"], a["st"], _NT, preferred_element_type=F32)
                  for a in heads]
        for p, a in enumerate(heads):
            scores = jnp.where(eye1, a["diag"], 0.0)
            for l in range(N_LEVELS):
                scores = jnp.where(pairs1[l], grams[p][l], scores)
            pair_outs.append(jnp.dot(scores.astype(BF16), a["v"],
                                     preferred_element_type=F32) + inters[p])

    outs, new_sts = [], []
    h = 0
    for g in range(n_g):
        cols = slice(g * LANES, (g + 1) * LANES)
        new_st = sts[g] * fac[c - 1:c, cols]
        outs_g = []
        for _ in vss[g]:
            outs_g.append(pair_outs[h])
            new_st = new_st + updates[h]
            h += 1
        outs.append(outs_g)
        new_sts.append(new_st)
    return outs, new_sts


def _head_out(o, norm_g, gate):
    return (_rms(o, norm_g) * _silu(gate)).astype(BF16)


def _rope_body(pos_ref, invf_ref, cos_ref, sin_ref):
    ang = pos_ref[...].astype(F32) * invf_ref[...]
    lane = lax.broadcasted_iota(I32, (1, LANES), 1)
    cos_ref[...] = jnp.cos(ang)
    sin_ref[...] = jnp.sin(ang) * jnp.where(lane < LANES // 2, -1.0, 1.0)


def _rope_tables(pos_col, inv_freq2, *, tm=2048):
    n_tok = pos_col.shape[0]
    tm = min(tm, n_tok)
    table = jax.ShapeDtypeStruct((n_tok, LANES), F32)
    return pl.pallas_call(
        _rope_body,
        grid=(n_tok // tm,),
        in_specs=[pl.BlockSpec((tm, 1), lambda i: (i, 0)),
                  pl.BlockSpec((1, LANES), lambda i: (0, 0))],
        out_specs=[pl.BlockSpec((tm, LANES), lambda i: (i, 0))] * 2,
        out_shape=[table, table],
        compiler_params=_cparams(1),
        name="rope_tables",
    )(pos_col, inv_freq2)


def _retention_body(cos_ref, sin_ref, ng_ref, q_ref, k_ref, v_ref, gate_ref, o_ref, st_ref, *,
                    n_heads):
    c = CHUNK
    tb = q_ref.shape[0]

    @pl.when(pl.program_id(1) == 0)
    def _():
        st_ref[...] = jnp.zeros_like(st_ref)

    ri = lax.broadcasted_iota(I32, (c, c), 0)
    ci = lax.broadcasted_iota(I32, (c, c), 1)
    rel = (ri - ci).astype(F32)
    trow = lax.broadcasted_iota(I32, (c, LANES), 0).astype(F32)
    k_scale = float(LANES) ** -0.5
    ng = ng_ref[...]
    log_gammas = [math.log1p(-(2.0 ** (-5 - h))) for h in range(n_heads)]
    dmats = [jnp.where(rel >= 0, jnp.exp(rel * lg), 0.0) for lg in log_gammas]
    q_decays = [jnp.exp((trow + 1.0) * lg) for lg in log_gammas]
    k_decays = [jnp.exp((c - 1.0 - trow) * lg) * k_scale for lg in log_gammas]
    hcols = [slice(h * LANES, (h + 1) * LANES) for h in range(n_heads)]

    def chunk(ic, carry):
        r0 = pl.multiple_of(ic * c, c)
        rows = pl.ds(r0, c)
        cosv = cos_ref[rows, :]
        sinv = sin_ref[rows, :]
        scores, inters, updates, vs = [], [], [], []
        for h, cols in enumerate(hcols):
            q = q_ref[rows, cols].astype(F32)
            k = k_ref[rows, cols].astype(F32)
            v = v_ref[rows, cols]
            qr = q * cosv + pltpu.roll(q, LANES // 2, 1) * sinv
            kr = k * cosv + pltpu.roll(k, LANES // 2, 1) * sinv
            scores.append(_bdot_nt(qr, kr * k_scale))
            inters.append(_bdot_nt(qr * q_decays[h], st_ref[h]))
            updates.append(lax.dot_general(v, (kr * k_decays[h]).astype(BF16), _TN,
                                           preferred_element_type=F32))
            vs.append(v)
        for h, cols in enumerate(hcols):
            o = jnp.dot((scores[h] * dmats[h]).astype(BF16), vs[h],
                        preferred_element_type=F32) + inters[h]
            st_ref[h] = math.exp(c * log_gammas[h]) * st_ref[h] + updates[h]
            o_ref[rows, cols] = _head_out(o, ng, gate_ref[rows, cols].astype(F32))
        return carry

    lax.fori_loop(0, tb // c, chunk, 0, unroll=4)


def _retention(z, cos_t, sin_t, norm_g, *, batch, seq, n_heads, col0, tb=512):
    n_tok = z.shape[0]
    w = n_heads * LANES
    assert seq % tb == 0 and tb % CHUNK == 0
    nt = seq // tb
    cb = col0 // w
    zspec = lambda j: pl.BlockSpec((tb, w), lambda b, t, j=j: (b * nt + t, cb + j))
    tspec = pl.BlockSpec((tb, LANES), lambda b, t: (b * nt + t, 0))
    return pl.pallas_call(
        functools.partial(_retention_body, n_heads=n_heads),
        grid=(batch, nt),
        in_specs=[tspec, tspec,
                  pl.BlockSpec((1, LANES), lambda b, t: (0, 0)),
                  zspec(0), zspec(1), zspec(2), zspec(3)],
        out_specs=pl.BlockSpec((tb, w), lambda b, t: (b * nt + t, 0)),
        out_shape=jax.ShapeDtypeStruct((n_tok, w), BF16),
        scratch_shapes=[pltpu.VMEM((n_heads, HEAD_DV, LANES), F32)],
        compiler_params=_cparams(2),
        name="retention",
    )(cos_t, sin_t, norm_g.reshape(1, LANES), z, z, z, z)


def _gla_body(wsum_ref, wa_ref, ba2_ref, ng_ref, a_ref, q_ref, k_ref, v_ref, gate_ref, o_ref,
              st_ref, lf_ref, *, n_groups):
    c = CHUNK
    tb = q_ref.shape[0]

    @pl.when(pl.program_id(1) == 0)
    def _():
        st_ref[...] = jnp.zeros_like(st_ref)

    x = jnp.dot(a_ref[...], wa_ref[...], preferred_element_type=F32) + ba2_ref[...]
    lf_ref[...] = (jnp.minimum(x, 0.0) - jnp.log1p(jnp.exp(-jnp.abs(x)))) * (1.0 / GLA_TAU)

    masks = _level_masks() + _single_head_masks()
    lane = lax.broadcasted_iota(I32, (1, LANES), 1)
    lane_masks = [lane < LANES // 2, lane >= LANES // 2]
    wsum = wsum_ref[...]
    ng = ng_ref[...]
    q_scale = float(LANES // 2) ** -0.5

    def chunk(ic, carry):
        r0 = pl.multiple_of(ic * c, c)
        rows = pl.ds(r0, c)
        kcols = [slice(g * LANES, (g + 1) * LANES) for g in range(n_groups)]
        vcols = [[slice((2 * g + j) * HEAD_DV, (2 * g + j + 1) * HEAD_DV) for j in range(2)]
                 for g in range(n_groups)]
        outs, new_sts = _gated_chunk(
            [q_ref[rows, kc].astype(F32) * q_scale for kc in kcols],
            [k_ref[rows, kc].astype(F32) for kc in kcols],
            [lf_ref[rows, kc] for kc in kcols],
            [[v_ref[rows, vc] for vc in vcs] for vcs in vcols],
            [lane_masks] * n_groups,
            [st_ref[g] for g in range(n_groups)],
            wsum, masks, pack_pairs=True)
        for g in range(n_groups):
            st_ref[g] = new_sts[g]
            for o, vc in zip(outs[g], vcols[g]):
                o_ref[rows, vc] = _head_out(o, ng, gate_ref[rows, vc].astype(F32))
        return carry

    lax.fori_loop(0, tb // c, chunk, 0, unroll=8)


def _gla(z, wsum, wa_pad, b_a2, norm_g, *, batch, seq, n_heads, col0, col_a, tb=512):
    n_tok = z.shape[0]
    wk = n_heads * (LANES // 2)
    wv = n_heads * HEAD_DV
    assert seq % tb == 0 and tb % CHUNK == 0
    nt = seq // tb
    n_groups = n_heads // 2
    row = lambda b, t: b * nt + t
    const = lambda shape: pl.BlockSpec(shape, lambda b, t: (0, 0))
    return pl.pallas_call(
        functools.partial(_gla_body, n_groups=n_groups),
        grid=(batch, nt),
        in_specs=[const(wsum.shape), const(wa_pad.shape), const((1, wk)), const((1, LANES)),
                  pl.BlockSpec((tb, LANES), lambda b, t: (row(b, t), col_a // LANES)),
                  pl.BlockSpec((tb, wk), lambda b, t: (row(b, t), col0 // wk)),
                  pl.BlockSpec((tb, wk), lambda b, t: (row(b, t), col0 // wk + 1)),
                  pl.BlockSpec((tb, wv), lambda b, t: (row(b, t), (col0 + 2 * wk) // wv)),
                  pl.BlockSpec((tb, wv), lambda b, t: (row(b, t), (col0 + 2 * wk) // wv + 1))],
        out_specs=pl.BlockSpec((tb, wv), lambda b, t: (row(b, t), 0)),
        out_shape=jax.ShapeDtypeStruct((n_tok, wv), BF16),
        scratch_shapes=[pltpu.VMEM((n_groups, HEAD_DV, LANES), F32),
                        pltpu.VMEM((tb, wk), F32)],
        compiler_params=_cparams(2),
        name="gla",
    )(wsum, wa_pad, b_a2.reshape(1, wk), norm_g.reshape(1, LANES), z, z, z, z, z)


def _hgrn_body(wsum_ref, lb_ref, ng_ref, q_ref, f_ref, i_ref, gate_ref, o_ref, st_ref, *, n_heads):
    c = CHUNK
    tb = q_ref.shape[0]

    @pl.when(pl.program_id(2) == 0)
    def _():
        st_ref[...] = jnp.zeros_like(st_ref)

    masks = _level_masks() + _single_head_masks()
    wsum = wsum_ref[...]
    ng = ng_ref[...]
    q_scale = float(LANES) ** -0.5

    def chunk(ic, carry):
        r0 = pl.multiple_of(ic * c, c)
        rows = pl.ds(r0, c)
        hcols = [slice(h * LANES, (h + 1) * LANES) for h in range(n_heads)]
        qs, ks, lfs = [], [], []
        for cols in hcols:
            lb = lb_ref[:, cols]
            f = f_ref[rows, cols].astype(F32)
            t = jnp.tanh(0.5 * f)
            b = 0.5 * (1.0 - lb)
            bt = b * t
            lfs.append(jnp.log((0.5 * (1.0 + lb)) + bt))
            ks.append(b - bt)
            qs.append(_silu(q_ref[rows, cols].astype(F32), q_scale))
        outs, new_sts = _gated_chunk(
            qs, ks, lfs, [[i_ref[rows, cols]] for cols in hcols], [[None]] * n_heads,
            [st_ref[h] for h in range(n_heads)], wsum, masks, pack_pairs=False)
        for h, cols in enumerate(hcols):
            st_ref[h] = new_sts[h]
            o_ref[rows, cols] = _head_out(outs[h][0], ng, gate_ref[rows, cols].astype(F32))
        return carry

    lax.fori_loop(0, tb // c, chunk, 0, unroll=2)


def _hgrn(z, wsum, lb, norm_g, *, batch, seq, n_heads, heads_per_step=8, tb=512):
    n_tok = z.shape[0]
    w = heads_per_step * LANES
    ng_ = n_heads // heads_per_step
    assert seq % tb == 0 and tb % CHUNK == 0
    nt = seq // tb
    zspec = lambda j: pl.BlockSpec((tb, w), lambda b, g, t, j=j: (b * nt + t, j * ng_ + g))
    return pl.pallas_call(
        functools.partial(_hgrn_body, n_heads=heads_per_step),
        grid=(batch, ng_, nt),
        in_specs=[pl.BlockSpec(wsum.shape, lambda b, g, t: (0, 0)),
                  pl.BlockSpec((1, w), lambda b, g, t: (0, g)),
                  pl.BlockSpec((1, LANES), lambda b, g, t: (0, 0)),
                  zspec(0), zspec(1), zspec(2), zspec(3)],
        out_specs=pl.BlockSpec((tb, w), lambda b, g, t: (b * nt + t, g)),
        out_shape=jax.ShapeDtypeStruct((n_tok, n_heads * LANES), BF16),
        scratch_shapes=[pltpu.VMEM((heads_per_step, HEAD_DV, LANES), F32)],
        compiler_params=_cparams(3),
        name="hgrn2",
    )(wsum, lb.reshape(1, n_heads * LANES), norm_g.reshape(1, LANES), z, z, z, z)


def _router_body(h_ref, x_ref, wo_ref, g_ref, w_ref, h1_ref, route_ref, cnt_ref, run_ref):
    tm = h_ref.shape[0]

    @pl.when(pl.program_id(0) == 0)
    def _():
        run_ref[...] = jnp.zeros_like(run_ref)

    ts = ROUTE_SUB
    w = w_ref[...]
    w_hi = w.astype(BF16)
    w_lo = (w - w_hi.astype(F32)).astype(BF16)
    lane = lax.broadcasted_iota(I32, (ts, LANES), 1)
    lane_f = lane.astype(F32)
    ri = lax.broadcasted_iota(I32, (ts, ts), 0)
    ci = lax.broadcasted_iota(I32, (ts, ts), 1)
    lstrict = jnp.where(ri > ci, 1.0, 0.0).astype(BF16)
    neg = -jnp.inf
    subs = [slice(r0, r0 + ts) for r0 in range(0, tm, ts)]
    h1s = [h_ref[rows, :] + jnp.dot(x_ref[rows, :], wo_ref[...], preferred_element_type=F32)
           for rows in subs]
    logits = []
    for rows, h1 in zip(subs, h1s):
        h1_ref[rows, :] = h1
        n = _rms(h1, g_ref[...])
        n_hi = n.astype(BF16)
        n_lo = (n - n_hi.astype(F32)).astype(BF16)
        logits.append(jnp.dot(n_hi, w_hi, preferred_element_type=F32)
                      + jnp.dot(n_hi, w_lo, preferred_element_type=F32)
                      + jnp.dot(n_lo, w_hi, preferred_element_type=F32))
    picks = []
    for lg in logits:
        lg1 = jnp.where(lane < N_EXPERTS, lg, neg)
        m1 = jnp.max(lg1, axis=-1, keepdims=True)
        i1 = jnp.min(jnp.where(lg1 == m1, lane_f, float(LANES)), axis=-1, keepdims=True)
        oh1 = lane_f == i1
        lg2 = jnp.where(oh1, neg, lg1)
        m2 = jnp.max(lg2, axis=-1, keepdims=True)
        i2 = jnp.min(jnp.where(lg2 == m2, lane_f, float(LANES)), axis=-1, keepdims=True)
        oh2 = lane_f == i2
        e2 = jnp.exp(m2 - m1)
        g1 = 1.0 / (1.0 + e2)
        both = jnp.where(oh1, 1.0, 0.0) + jnp.where(oh2, 1.0, 0.0)
        prefix = jnp.dot(lstrict, both.astype(BF16), preferred_element_type=F32)
        picks.append((i1, i2, g1, e2 * g1, oh1, oh2, both, prefix))
    run = run_ref[...]
    for rows, (i1, i2, g1, g2, oh1, oh2, both, prefix) in zip(subs, picks):
        before = prefix + run
        r1 = jnp.sum(jnp.where(oh1, before, 0.0), axis=-1, keepdims=True)
        r2 = jnp.sum(jnp.where(oh2, before, 0.0), axis=-1, keepdims=True)
        run = run + jnp.sum(both, axis=0, keepdims=True)
        out = jnp.where(lane == 0, i1, 0.0)
        out = jnp.where(lane == 1, i2, out)
        out = jnp.where(lane == 2, g1, out)
        out = jnp.where(lane == 3, g2, out)
        out = jnp.where(lane == 4, r1, out)
        out = jnp.where(lane == 5, r2, out)
        route_ref[rows, :] = out
    run_ref[...] = run
    cnt_ref[...] = run


def _router(h, x, w_out, g, w_pad, *, tm=512):
    n_tok, d = h.shape
    return pl.pallas_call(
        _router_body,
        grid=(n_tok // tm,),
        in_specs=[pl.BlockSpec((tm, d), lambda i: (i, 0)),
                  pl.BlockSpec((tm, x.shape[1]), lambda i: (i, 0)),
                  pl.BlockSpec(w_out.shape, lambda i: (0, 0), pipeline_mode=pl.Buffered(1)),
                  pl.BlockSpec((1, d), lambda i: (0, 0)),
                  pl.BlockSpec((d, LANES), lambda i: (0, 0))],
        out_specs=[pl.BlockSpec((tm, d), lambda i: (i, 0)),
                   pl.BlockSpec((tm, LANES), lambda i: (i, 0)),
                   pl.BlockSpec((1, LANES), lambda i: (0, 0))],
        out_shape=[jax.ShapeDtypeStruct((n_tok, d), F32),
                   jax.ShapeDtypeStruct((n_tok, LANES), F32),
                   jax.ShapeDtypeStruct((1, LANES), F32)],
        scratch_shapes=[pltpu.VMEM((1, LANES), F32)],
        compiler_params=_cparams(1),
        name="moe_router",
    )(h, x, w_out, g.reshape(1, d), w_pad)


def _dispatch_body(info_ref, h_ref, g_ref, pos_ref, xs_ref, nbuf, zbuf, idx, sems, isem, *, tm):
    i = pl.program_id(0)
    last = pl.num_programs(0) - 1
    slot = i % 2
    tg = zbuf.shape[0]

    def zero_copy(e):
        return pltpu.make_async_copy(
            zbuf, xs_ref.at[pl.ds(pl.multiple_of(info_ref[e], tg), tg)], sems.at[0, 0])

    @pl.when(i == 0)
    def _():
        zbuf[...] = jnp.zeros_like(zbuf)
        for e in range(2 * N_EXPERTS):
            @pl.when(info_ref[e] >= 0)
            def _():
                zero_copy(e).start()
        for e in range(2 * N_EXPERTS):
            @pl.when(info_ref[e] >= 0)
            def _():
                zero_copy(e).wait()

    def drain(s):
        for k in range(2):
            pltpu.make_async_copy(nbuf.at[s], xs_ref.at[pl.ds(0, tm)], sems.at[s, k]).wait()

    @pl.when(i >= 2)
    def _():
        drain(slot)

    idx_copy = pltpu.make_async_copy(pos_ref.at[i], idx, isem)
    idx_copy.start()
    nbuf[slot] = _rms(h_ref[...], g_ref[...])
    idx_copy.wait()

    def issue(j, carry):
        t0 = pl.multiple_of(j * ROW_UNROLL, ROW_UNROLL)
        for u in range(ROW_UNROLL):
            for k in range(2):
                pltpu.make_async_copy(nbuf.at[slot, pl.ds(t0 + u, 1)],
                                      xs_ref.at[pl.ds(idx[k * tm + t0 + u], 1)],
                                      sems.at[slot, k]).start()
        return carry

    lax.fori_loop(0, tm // ROW_UNROLL, issue, 0)

    @pl.when(i == last)
    def _():
        drain(slot)

        @pl.when(i >= 1)
        def _():
            drain(1 - slot)


def _dispatch(h, g, pos_tiles, info, rows_pad, *, tm, tg):
    n_tok, d = h.shape
    return pl.pallas_call(
        functools.partial(_dispatch_body, tm=tm),
        grid_spec=pltpu.PrefetchScalarGridSpec(
            num_scalar_prefetch=1,
            grid=(n_tok // tm,),
            in_specs=[pl.BlockSpec((tm, d), lambda i, info: (i, 0)),
                      pl.BlockSpec((1, d), lambda i, info: (0, 0)),
                      pl.BlockSpec(memory_space=pl.ANY)],
            out_specs=pl.BlockSpec(memory_space=pl.ANY),
            scratch_shapes=[pltpu.VMEM((2, tm, d), F32),
                            pltpu.VMEM((tg, d), F32),
                            pltpu.SMEM((2 * tm,), I32),
                            pltpu.SemaphoreType.DMA((2, 2)),
                            pltpu.SemaphoreType.DMA]),
        out_shape=jax.ShapeDtypeStruct((rows_pad, d), F32),
        compiler_params=_cparams(1),
        name="moe_dispatch",
    )(info, h, g.reshape(1, d), pos_tiles)


def _gmm_body(te_ref, nu_ref, x_ref, wg_ref, wu_ref, wd_ref, y_ref, *, ff_chunk):
    @pl.when(pl.program_id(0) >= nu_ref[0])
    def _():
        y_ref[...] = jnp.zeros_like(y_ref)

    @pl.when(pl.program_id(0) < nu_ref[0])
    def _():
        x = x_ref[...].astype(BF16)
        acc = None
        for c in range(0, wg_ref.shape[1], ff_chunk):
            a = jnp.dot(x, wg_ref[:, c:c + ff_chunk], preferred_element_type=F32)
            b = jnp.dot(x, wu_ref[:, c:c + ff_chunk], preferred_element_type=F32)
            hid = (_silu(a) * b).astype(BF16)
            part = jnp.dot(hid, wd_ref[c:c + ff_chunk, :], preferred_element_type=F32)
            acc = part if acc is None else acc + part
        y_ref[...] = acc


def _gmm(xs, tile_expert, n_used, wg, wu, wd, layer, *, tm, ff_chunk=7 * MXU_DIM):
    rows_pad, d = xs.shape
    ff = wg.shape[3]
    assert ff % ff_chunk == 0 and ff_chunk % MXU_DIM == 0
    tile = lambda i, te, nu: jnp.minimum(i, nu[0] - 1)
    wspec = lambda shape: pl.BlockSpec(
        (None, None) + shape, lambda i, te, nu: (layer, te[tile(i, te, nu)], 0, 0),
        pipeline_mode=pl.Buffered(1))
    return pl.pallas_call(
        functools.partial(_gmm_body, ff_chunk=ff_chunk),
        grid_spec=pltpu.PrefetchScalarGridSpec(
            num_scalar_prefetch=2,
            grid=(rows_pad // tm,),
            in_specs=[pl.BlockSpec((tm, d), lambda i, te, nu: (tile(i, te, nu), 0)),
                      wspec((d, ff)), wspec((d, ff)), wspec((ff, d))],
            out_specs=pl.BlockSpec((tm, d), lambda i, te, nu: (i, 0))),
        out_shape=jax.ShapeDtypeStruct((rows_pad, d), F32),
        compiler_params=_cparams(1),
        name="moe_experts",
    )(tile_expert, n_used, xs, wg, wu, wd)


def _combine_body(h_ref, route_ref, fg_ref, pos_ref, ys_ref, o_ref, gbuf, idx, sems, isem, *,
                  tm, final_norm):
    i = pl.program_id(0)
    slot = i % 2

    def gather(tile, s):
        idx_copy = pltpu.make_async_copy(pos_ref.at[tile], idx, isem)
        idx_copy.start()
        idx_copy.wait()

        def issue(j, carry):
            t0 = pl.multiple_of(j * ROW_UNROLL, ROW_UNROLL)
            for u in range(ROW_UNROLL):
                for k in range(2):
                    pltpu.make_async_copy(ys_ref.at[pl.ds(idx[k * tm + t0 + u], 1)],
                                          gbuf.at[s, k, pl.ds(t0 + u, 1)], sems.at[s, k]).start()
            return carry

        lax.fori_loop(0, tm // ROW_UNROLL, issue, 0)

    @pl.when(i == 0)
    def _():
        gather(0, 0)

    @pl.when(i + 1 < pl.num_programs(0))
    def _():
        gather(i + 1, 1 - slot)

    for k in range(2):
        pltpu.make_async_copy(ys_ref.at[pl.ds(0, tm)], gbuf.at[slot, k], sems.at[slot, k]).wait()

    route = route_ref[...]
    out = h_ref[...] + route[:, 2:3] * gbuf[slot, 0] + route[:, 3:4] * gbuf[slot, 1]
    if final_norm:
        out = _rms(out, fg_ref[...])
    o_ref[...] = out


def _combine(h, route, final_g, pos_tiles, ys, *, tm, final_norm):
    n_tok, d = h.shape
    return pl.pallas_call(
        functools.partial(_combine_body, tm=tm, final_norm=final_norm),
        grid=(n_tok // tm,),
        in_specs=[pl.BlockSpec((tm, d), lambda i: (i, 0)),
                  pl.BlockSpec((tm, LANES), lambda i: (i, 0)),
                  pl.BlockSpec((1, d), lambda i: (0, 0)),
                  pl.BlockSpec(memory_space=pl.ANY),
                  pl.BlockSpec(memory_space=pl.ANY)],
        out_specs=pl.BlockSpec((tm, d), lambda i: (i, 0)),
        out_shape=jax.ShapeDtypeStruct((n_tok, d), F32),
        scratch_shapes=[pltpu.VMEM((2, 2, tm, d), F32),
                        pltpu.SMEM((2 * tm,), I32),
                        pltpu.SemaphoreType.DMA((2, 2)),
                        pltpu.SemaphoreType.DMA],
        compiler_params=_cparams(1),
        name="moe_combine",
    )(h, route, final_g.reshape(1, d), pos_tiles, ys)


def _moe(h, x, w_out, norm_g, w_router, wg, wu, wd, layer, final_g, *, final_norm, tm_route=512,
         tm_gmm=512, tm_rows=1024):
    n_tok, d = h.shape
    w_pad = jnp.zeros((d, LANES), F32).at[:, :N_EXPERTS].set(w_router.astype(F32))
    h, route, counts = _router(h, x, w_out, norm_g, w_pad, tm=tm_route)

    cnt = counts[0, :N_EXPERTS].astype(I32)
    gsz = ((cnt + tm_gmm - 1) // tm_gmm) * tm_gmm
    ends = jnp.cumsum(gsz)
    offs = ends - gsz
    rows_pad = (n_tok * 2 // tm_gmm + N_EXPERTS) * tm_gmm
    n_tiles = rows_pad // tm_gmm
    tile_expert = jnp.minimum(
        jnp.searchsorted(ends, jnp.arange(n_tiles, dtype=I32) * tm_gmm, side="right"),
        N_EXPERTS - 1).astype(I32)
    n_used = jnp.maximum(ends[-1:] // tm_gmm, 1).astype(I32)
    tail = ends[-1] + jnp.arange(N_EXPERTS, dtype=I32) * tm_gmm
    info = jnp.concatenate([jnp.where(gsz > 0, ends - tm_gmm, -1),
                            jnp.where(tail < rows_pad, tail, -1)]).astype(I32)

    e01 = route[:, 0:2].astype(I32)
    pos = jnp.clip(offs[jnp.clip(e01, 0, N_EXPERTS - 1)] + route[:, 4:6].astype(I32),
                   0, rows_pad - 1)
    pos_tiles = pos.reshape(n_tok // tm_rows, tm_rows, 2).transpose(0, 2, 1).reshape(
        n_tok // tm_rows, 2 * tm_rows)

    xs = _dispatch(h, norm_g, pos_tiles, info, rows_pad, tm=tm_rows, tg=tm_gmm)
    ys = _gmm(xs, tile_expert, n_used, wg, wu, wd, layer, tm=tm_gmm)
    return _combine(h, route, final_g, pos_tiles, ys, tm=tm_rows, final_norm=final_norm)


def kernel(x, positions, norm_mix_g, norm_ffn_g, final_norm_g, ab_w_in, gla_w_a2, gla_b_a2,
           ret_norm_g, gla_norm_g, ab_w_out, ffn_w_gate, ffn_w_up, ffn_w_down,
           hgrn_lb_logits, c_w_in, hgrn_norm_g, c_w_out, moe_router, moe_w_gate,
           moe_w_up, moe_w_down):
    batch, seq, d = x.shape
    depth = norm_mix_g.shape[0]
    n_tok = batch * seq
    ret_heads = 4
    gla_heads = 4
    gla_rank = gla_w_a2.shape[1]
    hgrn_heads = d // LANES
    ab_cols = ab_w_in.shape[2] - gla_rank

    lb_cum = jnp.cumsum(jax.nn.softmax(hgrn_lb_logits.astype(F32), axis=0), axis=0)
    lower_bounds = lb_cum - lb_cum[0:1]

    wsum = jnp.asarray(_decay_sum_matrix(), BF16)
    inv_freq = ROPE_BASE ** (-jnp.arange(0, LANES, 2, dtype=F32) / LANES)
    inv_freq2 = jnp.concatenate([inv_freq, inv_freq]).reshape(1, LANES)
    cos_t, sin_t = _rope_tables(positions.reshape(n_tok, 1), inv_freq2)

    moe_wg, moe_wu, moe_wd = (w.astype(BF16) for w in (moe_w_gate, moe_w_up, moe_w_down))

    h = x.reshape(n_tok, d)
    for l in range(depth):
        e = l // 2
        if l % 2 == 0:
            w_in = jnp.pad(ab_w_in[e], ((0, 0), (0, LANES - gla_rank))).astype(BF16)
            z = _norm_proj(h, norm_mix_g[l], w_in)
            oa = _retention(z, cos_t, sin_t, ret_norm_g[e], batch=batch, seq=seq,
                            n_heads=ret_heads, col0=0)
            wa_pad = jnp.pad(gla_w_a2[e], ((0, LANES - gla_rank), (0, 0))).astype(BF16)
            ob = _gla(z, wsum, wa_pad, gla_b_a2[e], gla_norm_g[e], batch=batch, seq=seq,
                      n_heads=gla_heads, col0=4 * ret_heads * LANES, col_a=ab_cols)
            w_out = ab_w_out[e].astype(BF16)
            half = ret_heads * HEAD_DV
            h = _ffn(h, [oa, ob], [w_out[:half], w_out[half:]], norm_ffn_g[l],
                     ffn_w_gate[e].astype(BF16), ffn_w_up[e].astype(BF16),
                     ffn_w_down[e].astype(BF16))
        else:
            z = _norm_proj(h, norm_mix_g[l], c_w_in[e].astype(BF16))
            o = _hgrn(z, wsum, lower_bounds[l], hgrn_norm_g[e], batch=batch, seq=seq,
                      n_heads=hgrn_heads)
            h = _moe(h, o, c_w_out[e].astype(BF16), norm_ffn_g[l], moe_router[e],
                     moe_wg, moe_wu, moe_wd, e, final_norm_g, final_norm=(l == depth - 1))
    if depth % 2 == 1:
        raise NotImplementedError("final norm is fused into the last (odd) layer")
    return h.reshape(batch, seq, d)
```

```python
import functools
import math

import numpy as np
import jax
import jax.numpy as jnp
from jax import lax
from jax.experimental import pallas as pl
from jax.experimental.pallas import tpu as pltpu

F32 = jnp.float32
BF16 = jnp.bfloat16
I32 = jnp.int32

EPS = 1e-6
LANES = 128
MXU_DIM = 256
CHUNK = 64
N_LEVELS = 6
HEAD_DV = 128
ROPE_BASE = 10000.0
GLA_TAU = 16.0
LOG2_E = 1.4426950408889634
N_EXPERTS = 8
ROUTE_SUB = 128
ROW_UNROLL = 8
VMEM_LIMIT = 56 * 1024 * 1024

_NT = (((1,), (1,)), ((), ()))
_TN = (((0,), (0,)), ((), ()))


def _cparams(n_axes):
    return pltpu.CompilerParams(dimension_semantics=("arbitrary",) * n_axes,
                                vmem_limit_bytes=VMEM_LIMIT)


def _rms(x, g):
    ms = jnp.mean(x * x, axis=-1, keepdims=True)
    return x * lax.rsqrt(ms + EPS) * g


def _silu(x, scale=1.0):
    t = jnp.tanh(0.5 * x)
    hx = x * (0.5 * scale)
    return hx + hx * t


def _bdot_nt(a, b):
    return lax.dot_general(a.astype(BF16), b.astype(BF16), _NT, preferred_element_type=F32)


def _bdot_tn(a, b):
    return lax.dot_general(a.astype(BF16), b.astype(BF16), _TN, preferred_element_type=F32)


def _norm_proj_body(h_ref, g_ref, w_ref, z_ref, *, col_chunk):
    n = _rms(h_ref[...], g_ref[...]).astype(BF16)
    m = w_ref.shape[1]
    for c in range(0, m, col_chunk):
        ce = min(c + col_chunk, m)
        z_ref[:, c:ce] = jnp.dot(n, w_ref[:, c:ce], preferred_element_type=F32).astype(z_ref.dtype)


def _norm_proj(h, g, w, *, tm=512, col_chunk=512):
    n_tok, d = h.shape
    m = w.shape[1]
    assert n_tok % tm == 0
    return pl.pallas_call(
        functools.partial(_norm_proj_body, col_chunk=col_chunk),
        grid=(n_tok // tm,),
        in_specs=[pl.BlockSpec((tm, d), lambda i: (i, 0)),
                  pl.BlockSpec((1, d), lambda i: (0, 0)),
                  pl.BlockSpec((d, m), lambda i: (0, 0), pipeline_mode=pl.Buffered(1))],
        out_specs=pl.BlockSpec((tm, m), lambda i: (i, 0)),
        out_shape=jax.ShapeDtypeStruct((n_tok, m), BF16),
        compiler_params=_cparams(1),
        name="norm_proj",
    )(h, g.reshape(1, d), w)


def _ffn_body(*refs, n_in, ff_chunk):
    h_ref = refs[0]
    x_refs = refs[1:1 + n_in]
    w_refs = refs[1 + n_in:1 + 2 * n_in]
    g_ref, wg_ref, wu_ref, wd_ref, o_ref = refs[1 + 2 * n_in:]
    h = h_ref[...]
    for x_ref, w_ref in zip(x_refs, w_refs):
        h = h + jnp.dot(x_ref[...], w_ref[...], preferred_element_type=F32)
    o_ref[...] = h
    n = _rms(h, g_ref[...]).astype(BF16)
    acc = None
    ff = wg_ref.shape[1]
    for c in range(0, ff, ff_chunk):
        ce = min(c + ff_chunk, ff)
        a = jnp.dot(n, wg_ref[:, c:ce], preferred_element_type=F32)
        b = jnp.dot(n, wu_ref[:, c:ce], preferred_element_type=F32)
        hid = (_silu(a) * b).astype(BF16)
        part = jnp.dot(hid, wd_ref[c:ce, :], preferred_element_type=F32)
        acc = part if acc is None else acc + part
    o_ref[...] = o_ref[...] + acc


def _ffn(h, xs, ws, g, wg, wu, wd, *, tm=512, ff_chunk=6 * MXU_DIM):
    n_tok, d = h.shape
    ff = wg.shape[1]
    assert ff % MXU_DIM == 0
    const = lambda shape: pl.BlockSpec(shape, lambda i: (0, 0), pipeline_mode=pl.Buffered(1))
    in_specs = [pl.BlockSpec((tm, d), lambda i: (i, 0))]
    in_specs += [pl.BlockSpec((tm, x.shape[1]), lambda i: (i, 0)) for x in xs]
    in_specs += [const(w.shape) for w in ws]
    in_specs += [pl.BlockSpec((1, d), lambda i: (0, 0)), const((d, ff)), const((d, ff)), const((ff, d))]
    return pl.pallas_call(
        functools.partial(_ffn_body, n_in=len(xs), ff_chunk=ff_chunk),
        grid=(n_tok // tm,),
        in_specs=in_specs,
        out_specs=pl.BlockSpec((tm, d), lambda i: (i, 0)),
        out_shape=jax.ShapeDtypeStruct((n_tok, d), F32),
        compiler_params=_cparams(1),
        name="ffn_swiglu",
    )(h, *xs, *ws, g.reshape(1, d), wg, wu, wd)


def _decay_sum_matrix():
    c = CHUNK
    r = np.arange(c)[:, None]
    t = np.arange(c)[None, :]
    blocks = [(t <= r), (t > r)]
    for lvl in range(N_LEVELS):
        s = c >> (lvl + 1)
        m = (r // (2 * s)) * (2 * s) + s - 1
        upper = r > m
        blocks.append(np.where(upper, (t > m) & (t <= r), (t > r) & (t <= m)))
    w = np.concatenate(blocks, axis=0).astype(np.float32)
    return np.concatenate([w, w], axis=1)


def _level_masks():
    c = CHUNK
    assert 2 * c == LANES
    row = lax.broadcasted_iota(I32, (c, LANES), 0)
    col = lax.broadcasted_iota(I32, (c, LANES), 1) % c
    uppers, pairs = [], []
    for lvl in range(N_LEVELS):
        s = c >> (lvl + 1)
        uppers.append((row // s) % 2 == 1)
        pairs.append(((row // (2 * s)) == (col // (2 * s)))
                     & ((row // s) % 2 == 1) & ((col // s) % 2 == 0))
    return uppers, pairs, row == col


def _single_head_masks():
    c = CHUNK
    ri = lax.broadcasted_iota(I32, (c, c), 0)
    ci = lax.broadcasted_iota(I32, (c, c), 1)
    pairs = []
    for lvl in range(N_LEVELS):
        s = c >> (lvl + 1)
        pairs.append(((ri // (2 * s)) == (ci // (2 * s)))
                     & ((ri // s) % 2 == 1) & ((ci // s) % 2 == 0))
    return pairs, ri == ci


def _block_diag(a, b):
    z = jnp.zeros_like(a)
    return jnp.concatenate([jnp.concatenate([a, z], axis=1),
                            jnp.concatenate([z, b], axis=1)], axis=0)


def _gated_chunk(qs, ks, lfs, vss, lmss, sts, wsum2, masks, pack_pairs):
    c = CHUNK
    uppers, pairs, eye, pairs1, eye1 = masks
    n_g = len(qs)
    pieces = []
    for lf in lfs:
        lf2 = lf * LOG2_E
        hi = lf2.astype(BF16)
        lo = (lf2 - hi.astype(F32)).astype(BF16)
        pieces.append(jnp.concatenate([hi, lo], axis=0))
    ex = jnp.dot(wsum2, jnp.concatenate(pieces, axis=1), preferred_element_type=F32)
    fac = jnp.exp2(ex)
    fac_b = fac.astype(BF16)

    heads = []
    for g in range(n_g):
        cols = slice(g * LANES, (g + 1) * LANES)
        q_b = qs[g].astype(BF16)
        k_b = ks[g].astype(BF16)
        w = [jnp.where(uppers[l], q_b, k_b) * fac_b[(2 + l) * c:(3 + l) * c, cols]
             for l in range(N_LEVELS)]
        qf = q_b * fac_b[0:c, cols]
        kf = k_b * fac_b[c:2 * c, cols]
        st_b = sts[g].astype(BF16)
        qk = qs[g] * ks[g]
        for v, lm in zip(vss[g], lmss[g]):
            if lm is None:
                sel = lambda a: a
            else:
                sel = lambda a, lm=lm: jnp.where(lm, a, jnp.zeros_like(a))
            heads.append(dict(lhs=w, rhs=[sel(x) for x in w], qf=sel(qf), kf=sel(kf), st=st_b, v=v,
                              diag=jnp.sum(sel(qk), axis=-1, keepdims=True)))

    pair_outs = []
    state_update = lambda a: lax.dot_general(a["v"], a["kf"], _TN, preferred_element_type=F32)
    if pack_pairs:
        assert len(heads) % 2 == 0
        lane_lo = lax.broadcasted_iota(I32, (1, LANES), 1) < c
        updates = [state_update(a) for a in heads]
        grams, inters = [], []
        for a, b in zip(heads[0::2], heads[1::2]):
            grams.append([lax.dot_general(jnp.concatenate([a["lhs"][l], b["lhs"][l]], axis=1),
                                          _block_diag(a["rhs"][l], b["rhs"][l]), _NT,
                                          preferred_element_type=F32) for l in range(N_LEVELS)])
            inters.append(lax.dot_general(jnp.concatenate([a["qf"], b["qf"]], axis=1),
                                          _block_diag(a["st"], b["st"]), _NT,
                                          preferred_element_type=F32))
        for p, (a, b) in enumerate(zip(heads[0::2], heads[1::2])):
            scores = jnp.where(eye, jnp.where(lane_lo, a["diag"], b["diag"]), 0.0)
            for l in range(N_LEVELS):
                scores = jnp.where(pairs[l], grams[p][l], scores)
            o2 = jnp.dot(scores.astype(BF16), _block_diag(a["v"], b["v"]),
                         preferred_element_type=F32) + inters[p]
            pair_outs += [o2[:, :HEAD_DV], o2[:, HEAD_DV:]]
    else:
        grams = [[lax.dot_general(a["lhs"][l], a["rhs"][l], _NT, preferred_element_type=F32)
                  for l in range(N_LEVELS)] for a in heads]
        updates = [state_update(a) for a in heads]
        inters = [lax.dot_general(a["qf"], a["st"], _NT, preferred_element_type=F32)
                  for a in heads]
        for p, a in enumerate(heads):
            scores = jnp.where(eye1, a["diag"], 0.0)
            for l in range(N_LEVELS):
                scores = jnp.where(pairs1[l], grams[p][l], scores)
            pair_outs.append(jnp.dot(scores.astype(BF16), a["v"],
                                     preferred_element_type=F32) + inters[p])

    outs, new_sts = [], []
    h = 0
    for g in range(n_g):
        cols = slice(g * LANES, (g + 1) * LANES)
        new_st = sts[g] * fac[c - 1:c, cols]
        outs_g = []
        for _ in vss[g]:
            outs_g.append(pair_outs[h])
            new_st = new_st + updates[h]
            h += 1
        outs.append(outs_g)
        new_sts.append(new_st)
    return outs, new_sts


def _head_out(o, norm_g, gate):
    return (_rms(o, norm_g) * _silu(gate)).astype(BF16)


def _rope_body(pos_ref, invf_ref, cos_ref, sin_ref):
    ang = pos_ref[...].astype(F32) * invf_ref[...]
    lane = lax.broadcasted_iota(I32, (1, LANES), 1)
    cos_ref[...] = jnp.cos(ang)
    sin_ref[...] = jnp.sin(ang) * jnp.where(lane < LANES // 2, -1.0, 1.0)


def _rope_tables(pos_col, inv_freq2, *, tm=2048):
    n_tok = pos_col.shape[0]
    tm = min(tm, n_tok)
    table = jax.ShapeDtypeStruct((n_tok, LANES), F32)
    return pl.pallas_call(
        _rope_body,
        grid=(n_tok // tm,),
        in_specs=[pl.BlockSpec((tm, 1), lambda i: (i, 0)),
                  pl.BlockSpec((1, LANES), lambda i: (0, 0))],
        out_specs=[pl.BlockSpec((tm, LANES), lambda i: (i, 0))] * 2,
        out_shape=[table, table],
        compiler_params=_cparams(1),
        name="rope_tables",
    )(pos_col, inv_freq2)


def _retgla_body(cos_ref, sin_ref, rng_ref, wsum_ref, wa_ref, ba2_ref, gng_ref,
                aq_ref, ak_ref, av_ref, ag_ref, ba_ref, bq_ref, bk_ref, bv_ref, bg_ref, o_ref,
                rst_ref, gst_ref, lf_ref, *, ret_heads, gla_groups):
    c = CHUNK
    tb = aq_ref.shape[0]

    @pl.when(pl.program_id(1) == 0)
    def _():
        rst_ref[...] = jnp.zeros_like(rst_ref)
        gst_ref[...] = jnp.zeros_like(gst_ref)

    x = jnp.dot(ba_ref[...], wa_ref[...], preferred_element_type=F32) + ba2_ref[...]
    lf_ref[...] = (jnp.minimum(x, 0.0) - jnp.log1p(jnp.exp(-jnp.abs(x)))) * (1.0 / GLA_TAU)

    ri = lax.broadcasted_iota(I32, (c, c), 0)
    ci = lax.broadcasted_iota(I32, (c, c), 1)
    rel = (ri - ci).astype(F32)
    trow = lax.broadcasted_iota(I32, (c, LANES), 0).astype(F32)
    k_scale = float(LANES) ** -0.5
    log_gammas = [math.log1p(-(2.0 ** (-5 - h))) for h in range(ret_heads)]
    dmats = [jnp.where(rel >= 0, jnp.exp(rel * lg), 0.0) for lg in log_gammas]
    q_decays = [jnp.exp((trow + 1.0) * lg) for lg in log_gammas]
    k_decays = [jnp.exp((c - 1.0 - trow) * lg) * k_scale for lg in log_gammas]
    hcols = [slice(h * LANES, (h + 1) * LANES) for h in range(ret_heads)]
    ret_w = ret_heads * HEAD_DV

    masks = _level_masks() + _single_head_masks()
    lane = lax.broadcasted_iota(I32, (1, LANES), 1)
    lane_masks = [lane < LANES // 2, lane >= LANES // 2]
    wsum = wsum_ref[...]
    rng = rng_ref[...]
    gng = gng_ref[...]
    q_scale = float(LANES // 2) ** -0.5
    kcols = [slice(g * LANES, (g + 1) * LANES) for g in range(gla_groups)]
    vcols = [[slice((2 * g + j) * HEAD_DV, (2 * g + j + 1) * HEAD_DV) for j in range(2)]
             for g in range(gla_groups)]

    def chunk(ic, carry):
        r0 = pl.multiple_of(ic * c, c)
        rows = pl.ds(r0, c)
        cosv = cos_ref[rows, :]
        sinv = sin_ref[rows, :]
        scores, inters, updates, vs = [], [], [], []
        for h, cols in enumerate(hcols):
            q = aq_ref[rows, cols].astype(F32)
            k = ak_ref[rows, cols].astype(F32)
            v = av_ref[rows, cols]
            qr = q * cosv + pltpu.roll(q, LANES // 2, 1) * sinv
            kr = k * cosv + pltpu.roll(k, LANES // 2, 1) * sinv
            scores.append(_bdot_nt(qr, kr * k_scale))
            inters.append(_bdot_nt(qr * q_decays[h], rst_ref[h]))
            updates.append(lax.dot_general(v, (kr * k_decays[h]).astype(BF16), _TN,
                                           preferred_element_type=F32))
            vs.append(v)
        outs, new_sts = _gated_chunk(
            [bq_ref[rows, kc].astype(F32) * q_scale for kc in kcols],
            [bk_ref[rows, kc].astype(F32) for kc in kcols],
            [lf_ref[rows, kc] for kc in kcols],
            [[bv_ref[rows, vc] for vc in vcs] for vcs in vcols],
            [lane_masks] * gla_groups,
            [gst_ref[g] for g in range(gla_groups)],
            wsum, masks, pack_pairs=True)
        for g in range(gla_groups):
            gst_ref[g] = new_sts[g]
            for o, vc in zip(outs[g], vcols[g]):
                oc = slice(ret_w + vc.start, ret_w + vc.stop)
                o_ref[rows, oc] = _head_out(o, gng, bg_ref[rows, vc].astype(F32))
        for h, cols in enumerate(hcols):
            o = jnp.dot((scores[h] * dmats[h]).astype(BF16), vs[h],
                        preferred_element_type=F32) + inters[h]
            rst_ref[h] = math.exp(c * log_gammas[h]) * rst_ref[h] + updates[h]
            o_ref[rows, cols] = _head_out(o, rng, ag_ref[rows, cols].astype(F32))
        return carry

    lax.fori_loop(0, tb // c, chunk, 0, unroll=8)


def _retgla(z, cos_t, sin_t, ret_g, wsum, wa_pad, b_a2, gla_g, *, batch, seq, ret_heads,
            gla_heads, col_a, tb=512):
    n_tok = z.shape[0]
    wr = ret_heads * LANES
    wk = gla_heads * (LANES // 2)
    wv = gla_heads * HEAD_DV
    col0 = 4 * wr
    assert seq % tb == 0 and tb % CHUNK == 0
    nt = seq // tb
    row = lambda b, t: b * nt + t
    const = lambda shape: pl.BlockSpec(shape, lambda b, t: (0, 0))
    zblk = lambda w, j: pl.BlockSpec((tb, w), lambda b, t: (row(b, t), j))
    tspec = pl.BlockSpec((tb, LANES), lambda b, t: (row(b, t), 0))
    return pl.pallas_call(
        functools.partial(_retgla_body, ret_heads=ret_heads, gla_groups=gla_heads // 2),
        grid=(batch, nt),
        in_specs=[tspec, tspec, const((1, LANES)),
                  const(wsum.shape), const(wa_pad.shape), const((1, wk)), const((1, LANES)),
                  zblk(wr, 0), zblk(wr, 1), zblk(wr, 2), zblk(wr, 3),
                  zblk(LANES, col_a // LANES),
                  zblk(wk, col0 // wk), zblk(wk, col0 // wk + 1),
                  zblk(wv, (col0 + 2 * wk) // wv), zblk(wv, (col0 + 2 * wk) // wv + 1)],
        out_specs=pl.BlockSpec((tb, wr + wv), lambda b, t: (row(b, t), 0)),
        out_shape=jax.ShapeDtypeStruct((n_tok, wr + wv), BF16),
        scratch_shapes=[pltpu.VMEM((ret_heads, HEAD_DV, LANES), F32),
                        pltpu.VMEM((gla_heads // 2, HEAD_DV, LANES), F32),
                        pltpu.VMEM((tb, wk), F32)],
        compiler_params=_cparams(2),
        name="retention_gla",
    )(cos_t, sin_t, ret_g.reshape(1, LANES), wsum, wa_pad, b_a2.reshape(1, wk),
      gla_g.reshape(1, LANES), z, z, z, z, z, z, z, z, z)


def _hgrn_body(wsum_ref, lb_ref, ng_ref, q_ref, f_ref, i_ref, gate_ref, o_ref, st_ref, *, n_heads):
    c = CHUNK
    tb = q_ref.shape[0]

    @pl.when(pl.program_id(2) == 0)
    def _():
        st_ref[...] = jnp.zeros_like(st_ref)

    masks = _level_masks() + _single_head_masks()
    wsum = wsum_ref[...]
    ng = ng_ref[...]
    q_scale = float(LANES) ** -0.5

    def chunk(ic, carry):
        r0 = pl.multiple_of(ic * c, c)
        rows = pl.ds(r0, c)
        hcols = [slice(h * LANES, (h + 1) * LANES) for h in range(n_heads)]
        qs, ks, lfs = [], [], []
        for cols in hcols:
            lb = lb_ref[:, cols]
            f = f_ref[rows, cols].astype(F32)
            t = jnp.tanh(0.5 * f)
            b = 0.5 * (1.0 - lb)
            bt = b * t
            lfs.append(jnp.log((0.5 * (1.0 + lb)) + bt))
            ks.append(b - bt)
            qs.append(_silu(q_ref[rows, cols].astype(F32), q_scale))
        outs, new_sts = _gated_chunk(
            qs, ks, lfs, [[i_ref[rows, cols]] for cols in hcols], [[None]] * n_heads,
            [st_ref[h] for h in range(n_heads)], wsum, masks, pack_pairs=False)
        for h, cols in enumerate(hcols):
            st_ref[h] = new_sts[h]
            o_ref[rows, cols] = _head_out(outs[h][0], ng, gate_ref[rows, cols].astype(F32))
        return carry

    lax.fori_loop(0, tb // c, chunk, 0, unroll=2)


def _hgrn(z, wsum, lb, norm_g, *, batch, seq, n_heads, heads_per_step=8, tb=512):
    n_tok = z.shape[0]
    w = heads_per_step * LANES
    ng_ = n_heads // heads_per_step
    assert seq % tb == 0 and tb % CHUNK == 0
    nt = seq // tb
    zspec = lambda j: pl.BlockSpec((tb, w), lambda b, g, t, j=j: (b * nt + t, j * ng_ + g))
    return pl.pallas_call(
        functools.partial(_hgrn_body, n_heads=heads_per_step),
        grid=(batch, ng_, nt),
        in_specs=[pl.BlockSpec(wsum.shape, lambda b, g, t: (0, 0)),
                  pl.BlockSpec((1, w), lambda b, g, t: (0, g)),
                  pl.BlockSpec((1, LANES), lambda b, g, t: (0, 0)),
                  zspec(0), zspec(1), zspec(2), zspec(3)],
        out_specs=pl.BlockSpec((tb, w), lambda b, g, t: (b * nt + t, g)),
        out_shape=jax.ShapeDtypeStruct((n_tok, n_heads * LANES), BF16),
        scratch_shapes=[pltpu.VMEM((heads_per_step, HEAD_DV, LANES), F32)],
        compiler_params=_cparams(3),
        name="hgrn2",
    )(wsum, lb.reshape(1, n_heads * LANES), norm_g.reshape(1, LANES), z, z, z, z)


def _router_body(h_ref, x_ref, wo_ref, g_ref, w_ref, h1_ref, route_ref, cnt_ref, run_ref):
    tm = h_ref.shape[0]

    @pl.when(pl.program_id(0) == 0)
    def _():
        run_ref[...] = jnp.zeros_like(run_ref)

    ts = ROUTE_SUB
    w = w_ref[...]
    w_hi = w.astype(BF16)
    w_lo = (w - w_hi.astype(F32)).astype(BF16)
    lane = lax.broadcasted_iota(I32, (ts, LANES), 1)
    lane_f = lane.astype(F32)
    ri = lax.broadcasted_iota(I32, (ts, ts), 0)
    ci = lax.broadcasted_iota(I32, (ts, ts), 1)
    lstrict = jnp.where(ri > ci, 1.0, 0.0).astype(BF16)
    neg = -jnp.inf
    subs = [slice(r0, r0 + ts) for r0 in range(0, tm, ts)]
    h1s = [h_ref[rows, :] + jnp.dot(x_ref[rows, :], wo_ref[...], preferred_element_type=F32)
           for rows in subs]
    logits = []
    for rows, h1 in zip(subs, h1s):
        h1_ref[rows, :] = h1
        n = _rms(h1, g_ref[...])
        n_hi = n.astype(BF16)
        n_lo = (n - n_hi.astype(F32)).astype(BF16)
        logits.append(jnp.dot(n_hi, w_hi, preferred_element_type=F32)
                      + jnp.dot(n_hi, w_lo, preferred_element_type=F32)
                      + jnp.dot(n_lo, w_hi, preferred_element_type=F32))
    picks = []
    for lg in logits:
        lg1 = jnp.where(lane < N_EXPERTS, lg, neg)
        m1 = jnp.max(lg1, axis=-1, keepdims=True)
        i1 = jnp.min(jnp.where(lg1 == m1, lane_f, float(LANES)), axis=-1, keepdims=True)
        oh1 = lane_f == i1
        lg2 = jnp.where(oh1, neg, lg1)
        m2 = jnp.max(lg2, axis=-1, keepdims=True)
        i2 = jnp.min(jnp.where(lg2 == m2, lane_f, float(LANES)), axis=-1, keepdims=True)
        oh2 = lane_f == i2
        e2 = jnp.exp(m2 - m1)
        g1 = 1.0 / (1.0 + e2)
        both = jnp.where(oh1, 1.0, 0.0) + jnp.where(oh2, 1.0, 0.0)
        prefix = jnp.dot(lstrict, both.astype(BF16), preferred_element_type=F32)
        picks.append((i1, i2, g1, e2 * g1, oh1, oh2, both, prefix))
    run = run_ref[...]
    for rows, (i1, i2, g1, g2, oh1, oh2, both, prefix) in zip(subs, picks):
        before = prefix + run
        r1 = jnp.sum(jnp.where(oh1, before, 0.0), axis=-1, keepdims=True)
        r2 = jnp.sum(jnp.where(oh2, before, 0.0), axis=-1, keepdims=True)
        run = run + jnp.sum(both, axis=0, keepdims=True)
        out = jnp.where(lane == 0, i1, 0.0)
        out = jnp.where(lane == 1, i2, out)
        out = jnp.where(lane == 2, g1, out)
        out = jnp.where(lane == 3, g2, out)
        out = jnp.where(lane == 4, r1, out)
        out = jnp.where(lane == 5, r2, out)
        route_ref[rows, :] = out
    run_ref[...] = run
    cnt_ref[...] = run


def _router(h, x, w_out, g, w_pad, *, tm=512):
    n_tok, d = h.shape
    return pl.pallas_call(
        _router_body,
        grid=(n_tok // tm,),
        in_specs=[pl.BlockSpec((tm, d), lambda i: (i, 0)),
                  pl.BlockSpec((tm, x.shape[1]), lambda i: (i, 0)),
                  pl.BlockSpec(w_out.shape, lambda i: (0, 0), pipeline_mode=pl.Buffered(1)),
                  pl.BlockSpec((1, d), lambda i: (0, 0)),
                  pl.BlockSpec((d, LANES), lambda i: (0, 0))],
        out_specs=[pl.BlockSpec((tm, d), lambda i: (i, 0)),
                   pl.BlockSpec((tm, LANES), lambda i: (i, 0)),
                   pl.BlockSpec((1, LANES), lambda i: (0, 0))],
        out_shape=[jax.ShapeDtypeStruct((n_tok, d), F32),
                   jax.ShapeDtypeStruct((n_tok, LANES), F32),
                   jax.ShapeDtypeStruct((1, LANES), F32)],
        scratch_shapes=[pltpu.VMEM((1, LANES), F32)],
        compiler_params=_cparams(1),
        name="moe_router",
    )(h, x, w_out, g.reshape(1, d), w_pad)


def _dispatch_body(info_ref, h_ref, g_ref, pos_ref, xs_ref, nbuf, zbuf, idx, sems, isem, *, tm):
    i = pl.program_id(0)
    last = pl.num_programs(0) - 1
    slot = i % 2
    tg = zbuf.shape[0]

    def zero_copy(e):
        return pltpu.make_async_copy(
            zbuf, xs_ref.at[pl.ds(pl.multiple_of(info_ref[e], tg), tg)], sems.at[0, 0])

    @pl.when(i == 0)
    def _():
        zbuf[...] = jnp.zeros_like(zbuf)
        for e in range(2 * N_EXPERTS):
            @pl.when(info_ref[e] >= 0)
            def _():
                zero_copy(e).start()
        for e in range(2 * N_EXPERTS):
            @pl.when(info_ref[e] >= 0)
            def _():
                zero_copy(e).wait()

    def drain(s):
        for k in range(2):
            pltpu.make_async_copy(nbuf.at[s], xs_ref.at[pl.ds(0, tm)], sems.at[s, k]).wait()

    @pl.when(i >= 2)
    def _():
        drain(slot)

    idx_copy = pltpu.make_async_copy(pos_ref.at[i], idx, isem)
    idx_copy.start()
    nbuf[slot] = _rms(h_ref[...], g_ref[...])
    idx_copy.wait()

    def issue(j, carry):
        t0 = pl.multiple_of(j * ROW_UNROLL, ROW_UNROLL)
        for u in range(ROW_UNROLL):
            for k in range(2):
                pltpu.make_async_copy(nbuf.at[slot, pl.ds(t0 + u, 1)],
                                      xs_ref.at[pl.ds(idx[k * tm + t0 + u], 1)],
                                      sems.at[slot, k]).start()
        return carry

    lax.fori_loop(0, tm // ROW_UNROLL, issue, 0)

    @pl.when(i == last)
    def _():
        drain(slot)

        @pl.when(i >= 1)
        def _():
            drain(1 - slot)


def _dispatch(h, g, pos_tiles, info, rows_pad, *, tm, tg):
    n_tok, d = h.shape
    return pl.pallas_call(
        functools.partial(_dispatch_body, tm=tm),
        grid_spec=pltpu.PrefetchScalarGridSpec(
            num_scalar_prefetch=1,
            grid=(n_tok // tm,),
            in_specs=[pl.BlockSpec((tm, d), lambda i, info: (i, 0)),
                      pl.BlockSpec((1, d), lambda i, info: (0, 0)),
                      pl.BlockSpec(memory_space=pl.ANY)],
            out_specs=pl.BlockSpec(memory_space=pl.ANY),
            scratch_shapes=[pltpu.VMEM((2, tm, d), F32),
                            pltpu.VMEM((tg, d), F32),
                            pltpu.SMEM((2 * tm,), I32),
                            pltpu.SemaphoreType.DMA((2, 2)),
                            pltpu.SemaphoreType.DMA]),
        out_shape=jax.ShapeDtypeStruct((rows_pad, d), F32),
        compiler_params=_cparams(1),
        name="moe_dispatch",
    )(info, h, g.reshape(1, d), pos_tiles)


def _gmm_body(te_ref, nu_ref, x_ref, wg_ref, wu_ref, wd_ref, y_ref, *, ff_chunk):
    @pl.when(pl.program_id(0) >= nu_ref[0])
    def _():
        y_ref[...] = jnp.zeros_like(y_ref)

    @pl.when(pl.program_id(0) < nu_ref[0])
    def _():
        x = x_ref[...].astype(BF16)
        acc = None
        for c in range(0, wg_ref.shape[1], ff_chunk):
            a = jnp.dot(x, wg_ref[:, c:c + ff_chunk], preferred_element_type=F32)
            b = jnp.dot(x, wu_ref[:, c:c + ff_chunk], preferred_element_type=F32)
            hid = (_silu(a) * b).astype(BF16)
            part = jnp.dot(hid, wd_ref[c:c + ff_chunk, :], preferred_element_type=F32)
            acc = part if acc is None else acc + part
        y_ref[...] = acc


def _gmm(xs, tile_expert, n_used, wg, wu, wd, layer, *, tm, ff_chunk=7 * MXU_DIM):
    rows_pad, d = xs.shape
    ff = wg.shape[3]
    assert ff % ff_chunk == 0 and ff_chunk % MXU_DIM == 0
    tile = lambda i, te, nu: jnp.minimum(i, nu[0] - 1)
    wspec = lambda shape: pl.BlockSpec(
        (None, None) + shape, lambda i, te, nu: (layer, te[tile(i, te, nu)], 0, 0),
        pipeline_mode=pl.Buffered(1))
    return pl.pallas_call(
        functools.partial(_gmm_body, ff_chunk=ff_chunk),
        grid_spec=pltpu.PrefetchScalarGridSpec(
            num_scalar_prefetch=2,
            grid=(rows_pad // tm,),
            in_specs=[pl.BlockSpec((tm, d), lambda i, te, nu: (tile(i, te, nu), 0)),
                      wspec((d, ff)), wspec((d, ff)), wspec((ff, d))],
            out_specs=pl.BlockSpec((tm, d), lambda i, te, nu: (i, 0))),
        out_shape=jax.ShapeDtypeStruct((rows_pad, d), F32),
        compiler_params=_cparams(1),
        name="moe_experts",
    )(tile_expert, n_used, xs, wg, wu, wd)


def _combine_body(h_ref, route_ref, fg_ref, pos_ref, ys_ref, o_ref, gbuf, idx, sems, isem, *,
                  tm, final_norm):
    i = pl.program_id(0)
    slot = i % 2

    def gather(tile, s):
        idx_copy = pltpu.make_async_copy(pos_ref.at[tile], idx, isem)
        idx_copy.start()
        idx_copy.wait()

        def issue(j, carry):
            t0 = pl.multiple_of(j * ROW_UNROLL, ROW_UNROLL)
            for u in range(ROW_UNROLL):
                for k in range(2):
                    pltpu.make_async_copy(ys_ref.at[pl.ds(idx[k * tm + t0 + u], 1)],
                                          gbuf.at[s, k, pl.ds(t0 + u, 1)], sems.at[s, k]).start()
            return carry

        lax.fori_loop(0, tm // ROW_UNROLL, issue, 0)

    @pl.when(i == 0)
    def _():
        gather(0, 0)

    @pl.when(i + 1 < pl.num_programs(0))
    def _():
        gather(i + 1, 1 - slot)

    for k in range(2):
        pltpu.make_async_copy(ys_ref.at[pl.ds(0, tm)], gbuf.at[slot, k], sems.at[slot, k]).wait()

    route = route_ref[...]
    out = h_ref[...] + route[:, 2:3] * gbuf[slot, 0] + route[:, 3:4] * gbuf[slot, 1]
    if final_norm:
        out = _rms(out, fg_ref[...])
    o_ref[...] = out


def _combine(h, route, final_g, pos_tiles, ys, *, tm, final_norm):
    n_tok, d = h.shape
    return pl.pallas_call(
        functools.partial(_combine_body, tm=tm, final_norm=final_norm),
        grid=(n_tok // tm,),
        in_specs=[pl.BlockSpec((tm, d), lambda i: (i, 0)),
                  pl.BlockSpec((tm, LANES), lambda i: (i, 0)),
                  pl.BlockSpec((1, d), lambda i: (0, 0)),
                  pl.BlockSpec(memory_space=pl.ANY),
                  pl.BlockSpec(memory_space=pl.ANY)],
        out_specs=pl.BlockSpec((tm, d), lambda i: (i, 0)),
        out_shape=jax.ShapeDtypeStruct((n_tok, d), F32),
        scratch_shapes=[pltpu.VMEM((2, 2, tm, d), F32),
                        pltpu.SMEM((2 * tm,), I32),
                        pltpu.SemaphoreType.DMA((2, 2)),
                        pltpu.SemaphoreType.DMA],
        compiler_params=_cparams(1),
        name="moe_combine",
    )(h, route, final_g.reshape(1, d), pos_tiles, ys)


def _moe(h, x, w_out, norm_g, w_router, wg, wu, wd, layer, final_g, *, final_norm, tm_route=512,
         tm_gmm=512, tm_rows=1024):
    n_tok, d = h.shape
    w_pad = jnp.zeros((d, LANES), F32).at[:, :N_EXPERTS].set(w_router.astype(F32))
    h, route, counts = _router(h, x, w_out, norm_g, w_pad, tm=tm_route)

    cnt = counts[0, :N_EXPERTS].astype(I32)
    gsz = ((cnt + tm_gmm - 1) // tm_gmm) * tm_gmm
    ends = jnp.cumsum(gsz)
    offs = ends - gsz
    rows_pad = (n_tok * 2 // tm_gmm + N_EXPERTS) * tm_gmm
    n_tiles = rows_pad // tm_gmm
    tile_expert = jnp.minimum(
        jnp.searchsorted(ends, jnp.arange(n_tiles, dtype=I32) * tm_gmm, side="right"),
        N_EXPERTS - 1).astype(I32)
    n_used = jnp.maximum(ends[-1:] // tm_gmm, 1).astype(I32)
    tail = ends[-1] + jnp.arange(N_EXPERTS, dtype=I32) * tm_gmm
    info = jnp.concatenate([jnp.where(gsz > 0, ends - tm_gmm, -1),
                            jnp.where(tail < rows_pad, tail, -1)]).astype(I32)

    e01 = route[:, 0:2].astype(I32)
    pos = jnp.clip(offs[jnp.clip(e01, 0, N_EXPERTS - 1)] + route[:, 4:6].astype(I32),
                   0, rows_pad - 1)
    pos_tiles = pos.reshape(n_tok // tm_rows, tm_rows, 2).transpose(0, 2, 1).reshape(
        n_tok // tm_rows, 2 * tm_rows)

    xs = _dispatch(h, norm_g, pos_tiles, info, rows_pad, tm=tm_rows, tg=tm_gmm)
    ys = _gmm(xs, tile_expert, n_used, wg, wu, wd, layer, tm=tm_gmm)
    return _combine(h, route, final_g, pos_tiles, ys, tm=tm_rows, final_norm=final_norm)


def kernel(x, positions, norm_mix_g, norm_ffn_g, final_norm_g, ab_w_in, gla_w_a2, gla_b_a2,
           ret_norm_g, gla_norm_g, ab_w_out, ffn_w_gate, ffn_w_up, ffn_w_down,
           hgrn_lb_logits, c_w_in, hgrn_norm_g, c_w_out, moe_router, moe_w_gate,
           moe_w_up, moe_w_down):
    batch, seq, d = x.shape
    depth = norm_mix_g.shape[0]
    n_tok = batch * seq
    ret_heads = 4
    gla_heads = 4
    gla_rank = gla_w_a2.shape[1]
    hgrn_heads = d // LANES
    ab_cols = ab_w_in.shape[2] - gla_rank

    lb_cum = jnp.cumsum(jax.nn.softmax(hgrn_lb_logits.astype(F32), axis=0), axis=0)
    lower_bounds = lb_cum - lb_cum[0:1]

    wsum = jnp.asarray(_decay_sum_matrix(), BF16)
    inv_freq = ROPE_BASE ** (-jnp.arange(0, LANES, 2, dtype=F32) / LANES)
    inv_freq2 = jnp.concatenate([inv_freq, inv_freq]).reshape(1, LANES)
    cos_t, sin_t = _rope_tables(positions.reshape(n_tok, 1), inv_freq2)

    moe_wg, moe_wu, moe_wd = (w.astype(BF16) for w in (moe_w_gate, moe_w_up, moe_w_down))

    h = x.reshape(n_tok, d)
    for l in range(depth):
        e = l // 2
        if l % 2 == 0:
            w_in = jnp.pad(ab_w_in[e], ((0, 0), (0, LANES - gla_rank))).astype(BF16)
            z = _norm_proj(h, norm_mix_g[l], w_in)
            wa_pad = jnp.pad(gla_w_a2[e], ((0, LANES - gla_rank), (0, 0))).astype(BF16)
            o = _retgla(z, cos_t, sin_t, ret_norm_g[e], wsum, wa_pad, gla_b_a2[e], gla_norm_g[e],
                        batch=batch, seq=seq, ret_heads=ret_heads, gla_heads=gla_heads,
                        col_a=ab_cols)
            h = _ffn(h, [o], [ab_w_out[e].astype(BF16)], norm_ffn_g[l],
                     ffn_w_gate[e].astype(BF16), ffn_w_up[e].astype(BF16),
                     ffn_w_down[e].astype(BF16))
        else:
            z = _norm_proj(h, norm_mix_g[l], c_w_in[e].astype(BF16))
            o = _hgrn(z, wsum, lower_bounds[l], hgrn_norm_g[e], batch=batch, seq=seq,
                      n_heads=hgrn_heads)
            h = _moe(h, o, c_w_out[e].astype(BF16), norm_ffn_g[l], moe_router[e],
                     moe_wg, moe_wu, moe_wd, e, final_norm_g, final_norm=(l == depth - 1))
    if depth % 2 == 1:
        raise NotImplementedError("final norm is fused into the last (odd) layer")
    return h.reshape(batch, seq, d)
```

```python
import functools
import math

import numpy as np
import jax
import jax.numpy as jnp
from jax import lax
from jax.experimental import pallas as pl
from jax.experimental.pallas import tpu as pltpu

F32 = jnp.float32
BF16 = jnp.bfloat16
I32 = jnp.int32

EPS = 1e-6
LANES = 128
MXU_DIM = 256
CHUNK = 64
N_LEVELS = 6
HEAD_DV = 128
ROPE_BASE = 10000.0
GLA_TAU = 16.0
LOG2_E = 1.4426950408889634
N_EXPERTS = 8
ROUTE_SUB = 128
ROW_UNROLL = 8
VMEM_LIMIT = 56 * 1024 * 1024

_NT = (((1,), (1,)), ((), ()))
_TN = (((0,), (0,)), ((), ()))


def _cparams(n_axes):
    return pltpu.CompilerParams(dimension_semantics=("arbitrary",) * n_axes,
                                vmem_limit_bytes=VMEM_LIMIT)


def _rms(x, g):
    ms = jnp.mean(x * x, axis=-1, keepdims=True)
    return x * lax.rsqrt(ms + EPS) * g


def _silu(x, scale=1.0):
    t = jnp.tanh(0.5 * x)
    hx = x * (0.5 * scale)
    return hx + hx * t


def _bdot_nt(a, b):
    return lax.dot_general(a.astype(BF16), b.astype(BF16), _NT, preferred_element_type=F32)


def _bdot_tn(a, b):
    return lax.dot_general(a.astype(BF16), b.astype(BF16), _TN, preferred_element_type=F32)


def _norm_proj_body(h_ref, g_ref, w_ref, z_ref, *, col_chunk):
    n = _rms(h_ref[...], g_ref[...]).astype(BF16)
    m = w_ref.shape[1]
    for c in range(0, m, col_chunk):
        ce = min(c + col_chunk, m)
        z_ref[:, c:ce] = jnp.dot(n, w_ref[:, c:ce], preferred_element_type=F32).astype(z_ref.dtype)


def _norm_proj(h, g, w, *, tm=1024, col_chunk=512):
    n_tok, d = h.shape
    m = w.shape[1]
    assert n_tok % tm == 0
    return pl.pallas_call(
        functools.partial(_norm_proj_body, col_chunk=col_chunk),
        grid=(n_tok // tm,),
        in_specs=[pl.BlockSpec((tm, d), lambda i: (i, 0)),
                  pl.BlockSpec((1, d), lambda i: (0, 0)),
                  pl.BlockSpec((d, m), lambda i: (0, 0), pipeline_mode=pl.Buffered(1))],
        out_specs=pl.BlockSpec((tm, m), lambda i: (i, 0)),
        out_shape=jax.ShapeDtypeStruct((n_tok, m), BF16),
        compiler_params=_cparams(1),
        name="norm_proj",
    )(h, g.reshape(1, d), w)


def _ffn_body(*refs, n_in, ff_chunk):
    h_ref = refs[0]
    x_refs = refs[1:1 + n_in]
    w_refs = refs[1 + n_in:1 + 2 * n_in]
    g_ref, wg_ref, wu_ref, wd_ref, o_ref = refs[1 + 2 * n_in:]
    h = h_ref[...]
    for x_ref, w_ref in zip(x_refs, w_refs):
        h = h + jnp.dot(x_ref[...], w_ref[...], preferred_element_type=F32)
    o_ref[...] = h
    n = _rms(h, g_ref[...]).astype(BF16)
    acc = None
    ff = wg_ref.shape[1]
    for c in range(0, ff, ff_chunk):
        ce = min(c + ff_chunk, ff)
        a = jnp.dot(n, wg_ref[:, c:ce], preferred_element_type=F32)
        b = jnp.dot(n, wu_ref[:, c:ce], preferred_element_type=F32)
        hid = (_silu(a) * b).astype(BF16)
        part = jnp.dot(hid, wd_ref[c:ce, :], preferred_element_type=F32)
        acc = part if acc is None else acc + part
    o_ref[...] = o_ref[...] + acc


def _ffn(h, xs, ws, g, wg, wu, wd, *, tm=512, ff_chunk=6 * MXU_DIM):
    n_tok, d = h.shape
    ff = wg.shape[1]
    assert ff % MXU_DIM == 0
    const = lambda shape: pl.BlockSpec(shape, lambda i: (0, 0), pipeline_mode=pl.Buffered(1))
    in_specs = [pl.BlockSpec((tm, d), lambda i: (i, 0))]
    in_specs += [pl.BlockSpec((tm, x.shape[1]), lambda i: (i, 0)) for x in xs]
    in_specs += [const(w.shape) for w in ws]
    in_specs += [pl.BlockSpec((1, d), lambda i: (0, 0)), const((d, ff)), const((d, ff)), const((ff, d))]
    return pl.pallas_call(
        functools.partial(_ffn_body, n_in=len(xs), ff_chunk=ff_chunk),
        grid=(n_tok // tm,),
        in_specs=in_specs,
        out_specs=pl.BlockSpec((tm, d), lambda i: (i, 0)),
        out_shape=jax.ShapeDtypeStruct((n_tok, d), F32),
        compiler_params=_cparams(1),
        name="ffn_swiglu",
    )(h, *xs, *ws, g.reshape(1, d), wg, wu, wd)


def _decay_sum_matrix():
    c = CHUNK
    r = np.arange(c)[:, None]
    t = np.arange(c)[None, :]
    blocks = [(t <= r), (t > r)]
    for lvl in range(N_LEVELS):
        s = c >> (lvl + 1)
        m = (r // (2 * s)) * (2 * s) + s - 1
        upper = r > m
        blocks.append(np.where(upper, (t > m) & (t <= r), (t > r) & (t <= m)))
    w = np.concatenate(blocks, axis=0).astype(np.float32)
    return np.concatenate([w, w], axis=1)


def _level_masks():
    c = CHUNK
    assert 2 * c == LANES
    row = lax.broadcasted_iota(I32, (c, LANES), 0)
    col = lax.broadcasted_iota(I32, (c, LANES), 1) % c
    uppers, pairs = [], []
    for lvl in range(N_LEVELS):
        s = c >> (lvl + 1)
        uppers.append((row // s) % 2 == 1)
        pairs.append(((row // (2 * s)) == (col // (2 * s)))
                     & ((row // s) % 2 == 1) & ((col // s) % 2 == 0))
    return uppers, pairs, row == col


def _single_head_masks():
    c = CHUNK
    ri = lax.broadcasted_iota(I32, (c, c), 0)
    ci = lax.broadcasted_iota(I32, (c, c), 1)
    pairs = []
    for lvl in range(N_LEVELS):
        s = c >> (lvl + 1)
        pairs.append(((ri // (2 * s)) == (ci // (2 * s)))
                     & ((ri // s) % 2 == 1) & ((ci // s) % 2 == 0))
    return pairs, ri == ci


def _block_diag(a, b):
    z = jnp.zeros_like(a)
    return jnp.concatenate([jnp.concatenate([a, z], axis=1),
                            jnp.concatenate([z, b], axis=1)], axis=0)


def _gated_chunk(qs, ks, lfs, vss, lmss, sts, wsum2, masks, pack_pairs):
    c = CHUNK
    uppers, pairs, eye, pairs1, eye1 = masks
    n_g = len(qs)
    pieces = []
    for lf in lfs:
        lf2 = lf * LOG2_E
        hi = lf2.astype(BF16)
        lo = (lf2 - hi.astype(F32)).astype(BF16)
        pieces.append(jnp.concatenate([hi, lo], axis=0))
    ex = jnp.dot(wsum2, jnp.concatenate(pieces, axis=1), preferred_element_type=F32)
    fac = jnp.exp2(ex)
    fac_b = fac.astype(BF16)

    heads = []
    for g in range(n_g):
        cols = slice(g * LANES, (g + 1) * LANES)
        q_b = qs[g].astype(BF16)
        k_b = ks[g].astype(BF16)
        w = [jnp.where(uppers[l], q_b, k_b) * fac_b[(2 + l) * c:(3 + l) * c, cols]
             for l in range(N_LEVELS)]
        qf = q_b * fac_b[0:c, cols]
        kf = k_b * fac_b[c:2 * c, cols]
        st_b = sts[g].astype(BF16)
        qk = qs[g] * ks[g]
        for v, lm in zip(vss[g], lmss[g]):
            if lm is None:
                sel = lambda a: a
            else:
                sel = lambda a, lm=lm: jnp.where(lm, a, jnp.zeros_like(a))
            heads.append(dict(lhs=w, rhs=[sel(x) for x in w], qf=sel(qf), kf=sel(kf), st=st_b, v=v,
                              diag=jnp.sum(sel(qk), axis=-1, keepdims=True)))

    pair_outs = []
    state_update = lambda a: lax.dot_general(a["v"], a["kf"], _TN, preferred_element_type=F32)
    if pack_pairs:
        assert len(heads) % 2 == 0
        lane_lo = lax.broadcasted_iota(I32, (1, LANES), 1) < c
        updates = [state_update(a) for a in heads]
        grams, inters = [], []
        for a, b in zip(heads[0::2], heads[1::2]):
            grams.append([lax.dot_general(jnp.concatenate([a["lhs"][l], b["lhs"][l]], axis=1),
                                          _block_diag(a["rhs"][l], b["rhs"][l]), _NT,
                                          preferred_element_type=F32) for l in range(N_LEVELS)])
            inters.append(lax.dot_general(jnp.concatenate([a["qf"], b["qf"]], axis=1),
                                          _block_diag(a["st"], b["st"]), _NT,
                                          preferred_element_type=F32))
        for p, (a, b) in enumerate(zip(heads[0::2], heads[1::2])):
            scores = jnp.where(eye, jnp.where(lane_lo, a["diag"], b["diag"]), 0.0)
            for l in range(N_LEVELS):
                scores = jnp.where(pairs[l], grams[p][l], scores)
            o2 = jnp.dot(scores.astype(BF16), _block_diag(a["v"], b["v"]),
                         preferred_element_type=F32) + inters[p]
            pair_outs += [o2[:, :HEAD_DV], o2[:, HEAD_DV:]]
    else:
        grams = [[lax.dot_general(a["lhs"][l], a["rhs"][l], _NT, preferred_element_type=F32)
                  for l in range(N_LEVELS)] for a in heads]
        updates = [state_update(a) for a in heads]
        inters = [lax.dot_general(a["qf"], a["st"], _NT, preferred_element_type=F32)
                  for a in heads]
        for p, a in enumerate(heads):
            scores = jnp.where(eye1, a["diag"], 0.0)
            for l in range(N_LEVELS):
                scores = jnp.where(pairs1[l], grams[p][l], scores)
            pair_outs.append(jnp.dot(scores.astype(BF16), a["v"],
                                     preferred_element_type=F32) + inters[p])

    outs, new_sts = [], []
    h = 0
    for g in range(n_g):
        cols = slice(g * LANES, (g + 1) * LANES)
        new_st = sts[g] * fac[c - 1:c, cols]
        outs_g = []
        for _ in vss[g]:
            outs_g.append(pair_outs[h])
            new_st = new_st + updates[h]
            h += 1
        outs.append(outs_g)
        new_sts.append(new_st)
    return outs, new_sts


def _head_out(o, norm_g, gate):
    return (_rms(o, norm_g) * _silu(gate)).astype(BF16)


def _rope_body(pos_ref, invf_ref, cos_ref, sin_ref):
    ang = pos_ref[...].astype(F32) * invf_ref[...]
    lane = lax.broadcasted_iota(I32, (1, LANES), 1)
    cos_ref[...] = jnp.cos(ang)
    sin_ref[...] = jnp.sin(ang) * jnp.where(lane < LANES // 2, -1.0, 1.0)


def _rope_tables(pos_col, inv_freq2, *, tm=2048):
    n_tok = pos_col.shape[0]
    tm = min(tm, n_tok)
    table = jax.ShapeDtypeStruct((n_tok, LANES), F32)
    return pl.pallas_call(
        _rope_body,
        grid=(n_tok // tm,),
        in_specs=[pl.BlockSpec((tm, 1), lambda i: (i, 0)),
                  pl.BlockSpec((1, LANES), lambda i: (0, 0))],
        out_specs=[pl.BlockSpec((tm, LANES), lambda i: (i, 0))] * 2,
        out_shape=[table, table],
        compiler_params=_cparams(1),
        name="rope_tables",
    )(pos_col, inv_freq2)


def _retgla_body(cos_ref, sin_ref, rng_ref, wsum_ref, wa_ref, ba2_ref, gng_ref,
                aq_ref, ak_ref, av_ref, ag_ref, ba_ref, bq_ref, bk_ref, bv_ref, bg_ref, o_ref,
                rst_ref, gst_ref, lf_ref, *, ret_heads, gla_groups):
    c = CHUNK
    tb = aq_ref.shape[0]

    @pl.when(pl.program_id(1) == 0)
    def _():
        rst_ref[...] = jnp.zeros_like(rst_ref)
        gst_ref[...] = jnp.zeros_like(gst_ref)

    x = jnp.dot(ba_ref[...], wa_ref[...], preferred_element_type=F32) + ba2_ref[...]
    lf_ref[...] = (jnp.minimum(x, 0.0) - jnp.log1p(jnp.exp(-jnp.abs(x)))) * (1.0 / GLA_TAU)

    ri = lax.broadcasted_iota(I32, (c, c), 0)
    ci = lax.broadcasted_iota(I32, (c, c), 1)
    rel = (ri - ci).astype(F32)
    trow = lax.broadcasted_iota(I32, (c, LANES), 0).astype(F32)
    k_scale = float(LANES) ** -0.5
    log_gammas = [math.log1p(-(2.0 ** (-5 - h))) for h in range(ret_heads)]
    dmats = [jnp.where(rel >= 0, jnp.exp(rel * lg), 0.0) for lg in log_gammas]
    q_decays = [jnp.exp((trow + 1.0) * lg) for lg in log_gammas]
    k_decays = [jnp.exp((c - 1.0 - trow) * lg) * k_scale for lg in log_gammas]
    hcols = [slice(h * LANES, (h + 1) * LANES) for h in range(ret_heads)]
    ret_w = ret_heads * HEAD_DV

    masks = _level_masks() + _single_head_masks()
    lane = lax.broadcasted_iota(I32, (1, LANES), 1)
    lane_masks = [lane < LANES // 2, lane >= LANES // 2]
    wsum = wsum_ref[...]
    rng = rng_ref[...]
    gng = gng_ref[...]
    q_scale = float(LANES // 2) ** -0.5
    kcols = [slice(g * LANES, (g + 1) * LANES) for g in range(gla_groups)]
    vcols = [[slice((2 * g + j) * HEAD_DV, (2 * g + j + 1) * HEAD_DV) for j in range(2)]
             for g in range(gla_groups)]

    def chunk(ic, carry):
        r0 = pl.multiple_of(ic * c, c)
        rows = pl.ds(r0, c)
        cosv = cos_ref[rows, :]
        sinv = sin_ref[rows, :]
        scores, inters, updates, vs = [], [], [], []
        for h, cols in enumerate(hcols):
            q = aq_ref[rows, cols].astype(F32)
            k = ak_ref[rows, cols].astype(F32)
            v = av_ref[rows, cols]
            qr = q * cosv + pltpu.roll(q, LANES // 2, 1) * sinv
            kr = k * cosv + pltpu.roll(k, LANES // 2, 1) * sinv
            scores.append(_bdot_nt(qr, kr * k_scale))
            inters.append(_bdot_nt(qr * q_decays[h], rst_ref[h]))
            updates.append(lax.dot_general(v, (kr * k_decays[h]).astype(BF16), _TN,
                                           preferred_element_type=F32))
            vs.append(v)
        outs, new_sts = _gated_chunk(
            [bq_ref[rows, kc].astype(F32) * q_scale for kc in kcols],
            [bk_ref[rows, kc].astype(F32) for kc in kcols],
            [lf_ref[rows, kc] for kc in kcols],
            [[bv_ref[rows, vc] for vc in vcs] for vcs in vcols],
            [lane_masks] * gla_groups,
            [gst_ref[g] for g in range(gla_groups)],
            wsum, masks, pack_pairs=True)
        for g in range(gla_groups):
            gst_ref[g] = new_sts[g]
            for o, vc in zip(outs[g], vcols[g]):
                oc = slice(ret_w + vc.start, ret_w + vc.stop)
                o_ref[rows, oc] = _head_out(o, gng, bg_ref[rows, vc].astype(F32))
        for h, cols in enumerate(hcols):
            o = jnp.dot((scores[h] * dmats[h]).astype(BF16), vs[h],
                        preferred_element_type=F32) + inters[h]
            rst_ref[h] = math.exp(c * log_gammas[h]) * rst_ref[h] + updates[h]
            o_ref[rows, cols] = _head_out(o, rng, ag_ref[rows, cols].astype(F32))
        return carry

    lax.fori_loop(0, tb // c, chunk, 0, unroll=8)


def _retgla(z, cos_t, sin_t, ret_g, wsum, wa_pad, b_a2, gla_g, *, batch, seq, ret_heads,
            gla_heads, col_a, tb=512):
    n_tok = z.shape[0]
    wr = ret_heads * LANES
    wk = gla_heads * (LANES // 2)
    wv = gla_heads * HEAD_DV
    col0 = 4 * wr
    assert seq % tb == 0 and tb % CHUNK == 0
    nt = seq // tb
    row = lambda b, t: b * nt + t
    const = lambda shape: pl.BlockSpec(shape, lambda b, t: (0, 0))
    zblk = lambda w, j: pl.BlockSpec((tb, w), lambda b, t: (row(b, t), j))
    tspec = pl.BlockSpec((tb, LANES), lambda b, t: (row(b, t), 0))
    return pl.pallas_call(
        functools.partial(_retgla_body, ret_heads=ret_heads, gla_groups=gla_heads // 2),
        grid=(batch, nt),
        in_specs=[tspec, tspec, const((1, LANES)),
                  const(wsum.shape), const(wa_pad.shape), const((1, wk)), const((1, LANES)),
                  zblk(wr, 0), zblk(wr, 1), zblk(wr, 2), zblk(wr, 3),
                  zblk(LANES, col_a // LANES),
                  zblk(wk, col0 // wk), zblk(wk, col0 // wk + 1),
                  zblk(wv, (col0 + 2 * wk) // wv), zblk(wv, (col0 + 2 * wk) // wv + 1)],
        out_specs=pl.BlockSpec((tb, wr + wv), lambda b, t: (row(b, t), 0)),
        out_shape=jax.ShapeDtypeStruct((n_tok, wr + wv), BF16),
        scratch_shapes=[pltpu.VMEM((ret_heads, HEAD_DV, LANES), F32),
                        pltpu.VMEM((gla_heads // 2, HEAD_DV, LANES), F32),
                        pltpu.VMEM((tb, wk), F32)],
        compiler_params=_cparams(2),
        name="retention_gla",
    )(cos_t, sin_t, ret_g.reshape(1, LANES), wsum, wa_pad, b_a2.reshape(1, wk),
      gla_g.reshape(1, LANES), z, z, z, z, z, z, z, z, z)


def _hgrn_body(wsum_ref, lb_ref, ng_ref, q_ref, f_ref, i_ref, gate_ref, o_ref, st_ref, *, n_heads):
    c = CHUNK
    tb = q_ref.shape[0]

    @pl.when(pl.program_id(2) == 0)
    def _():
        st_ref[...] = jnp.zeros_like(st_ref)

    masks = _level_masks() + _single_head_masks()
    wsum = wsum_ref[...]
    ng = ng_ref[...]
    q_scale = float(LANES) ** -0.5

    def chunk(ic, carry):
        r0 = pl.multiple_of(ic * c, c)
        rows = pl.ds(r0, c)
        hcols = [slice(h * LANES, (h + 1) * LANES) for h in range(n_heads)]
        qs, ks, lfs = [], [], []
        for cols in hcols:
            lb = lb_ref[:, cols]
            f = f_ref[rows, cols].astype(F32)
            t = jnp.tanh(0.5 * f)
            b = 0.5 * (1.0 - lb)
            bt = b * t
            lfs.append(jnp.log((0.5 * (1.0 + lb)) + bt))
            ks.append(b - bt)
            qs.append(_silu(q_ref[rows, cols].astype(F32), q_scale))
        outs, new_sts = _gated_chunk(
            qs, ks, lfs, [[i_ref[rows, cols]] for cols in hcols], [[None]] * n_heads,
            [st_ref[h] for h in range(n_heads)], wsum, masks, pack_pairs=False)
        for h, cols in enumerate(hcols):
            st_ref[h] = new_sts[h]
            o_ref[rows, cols] = _head_out(outs[h][0], ng, gate_ref[rows, cols].astype(F32))
        return carry

    lax.fori_loop(0, tb // c, chunk, 0, unroll=2)


def _hgrn(z, wsum, lb, norm_g, *, batch, seq, n_heads, heads_per_step=8, tb=512):
    n_tok = z.shape[0]
    w = heads_per_step * LANES
    ng_ = n_heads // heads_per_step
    assert seq % tb == 0 and tb % CHUNK == 0
    nt = seq // tb
    zspec = lambda j: pl.BlockSpec((tb, w), lambda b, g, t, j=j: (b * nt + t, j * ng_ + g))
    return pl.pallas_call(
        functools.partial(_hgrn_body, n_heads=heads_per_step),
        grid=(batch, ng_, nt),
        in_specs=[pl.BlockSpec(wsum.shape, lambda b, g, t: (0, 0)),
                  pl.BlockSpec((1, w), lambda b, g, t: (0, g)),
                  pl.BlockSpec((1, LANES), lambda b, g, t: (0, 0)),
                  zspec(0), zspec(1), zspec(2), zspec(3)],
        out_specs=pl.BlockSpec((tb, w), lambda b, g, t: (b * nt + t, g)),
        out_shape=jax.ShapeDtypeStruct((n_tok, n_heads * LANES), BF16),
        scratch_shapes=[pltpu.VMEM((heads_per_step, HEAD_DV, LANES), F32)],
        compiler_params=_cparams(3),
        name="hgrn2",
    )(wsum, lb.reshape(1, n_heads * LANES), norm_g.reshape(1, LANES), z, z, z, z)


def _router_body(h_ref, x_ref, wo_ref, g_ref, w_ref, h1_ref, route_ref, cnt_ref, run_ref):
    tm = h_ref.shape[0]

    @pl.when(pl.program_id(0) == 0)
    def _():
        run_ref[...] = jnp.zeros_like(run_ref)

    ts = ROUTE_SUB
    w = w_ref[...]
    w_hi = w.astype(BF16)
    w_lo = (w - w_hi.astype(F32)).astype(BF16)
    lane = lax.broadcasted_iota(I32, (ts, LANES), 1)
    lane_f = lane.astype(F32)
    ri = lax.broadcasted_iota(I32, (ts, ts), 0)
    ci = lax.broadcasted_iota(I32, (ts, ts), 1)
    lstrict = jnp.where(ri > ci, 1.0, 0.0).astype(BF16)
    neg = -jnp.inf
    subs = [slice(r0, r0 + ts) for r0 in range(0, tm, ts)]
    h1s = [h_ref[rows, :] + jnp.dot(x_ref[rows, :], wo_ref[...], preferred_element_type=F32)
           for rows in subs]
    logits = []
    for rows, h1 in zip(subs, h1s):
        h1_ref[rows, :] = h1
        n = _rms(h1, g_ref[...])
        n_hi = n.astype(BF16)
        n_lo = (n - n_hi.astype(F32)).astype(BF16)
        logits.append(jnp.dot(n_hi, w_hi, preferred_element_type=F32)
                      + jnp.dot(n_hi, w_lo, preferred_element_type=F32)
                      + jnp.dot(n_lo, w_hi, preferred_element_type=F32))
    picks = []
    for lg in logits:
        lg1 = jnp.where(lane < N_EXPERTS, lg, neg)
        m1 = jnp.max(lg1, axis=-1, keepdims=True)
        i1 = jnp.min(jnp.where(lg1 == m1, lane_f, float(LANES)), axis=-1, keepdims=True)
        oh1 = lane_f == i1
        lg2 = jnp.where(oh1, neg, lg1)
        m2 = jnp.max(lg2, axis=-1, keepdims=True)
        i2 = jnp.min(jnp.where(lg2 == m2, lane_f, float(LANES)), axis=-1, keepdims=True)
        oh2 = lane_f == i2
        e2 = jnp.exp(m2 - m1)
        g1 = 1.0 / (1.0 + e2)
        both = jnp.where(oh1, 1.0, 0.0) + jnp.where(oh2, 1.0, 0.0)
        prefix = jnp.dot(lstrict, both.astype(BF16), preferred_element_type=F32)
        picks.append((i1, i2, g1, e2 * g1, oh1, oh2, both, prefix))
    run = run_ref[...]
    for rows, (i1, i2, g1, g2, oh1, oh2, both, prefix) in zip(subs, picks):
        before = prefix + run
        r1 = jnp.sum(jnp.where(oh1, before, 0.0), axis=-1, keepdims=True)
        r2 = jnp.sum(jnp.where(oh2, before, 0.0), axis=-1, keepdims=True)
        run = run + jnp.sum(both, axis=0, keepdims=True)
        out = jnp.where(lane == 0, i1, 0.0)
        out = jnp.where(lane == 1, i2, out)
        out = jnp.where(lane == 2, g1, out)
        out = jnp.where(lane == 3, g2, out)
        out = jnp.where(lane == 4, r1, out)
        out = jnp.where(lane == 5, r2, out)
        route_ref[rows, :] = out
    run_ref[...] = run
    cnt_ref[...] = run


def _router(h, x, w_out, g, w_pad, *, tm=512):
    n_tok, d = h.shape
    return pl.pallas_call(
        _router_body,
        grid=(n_tok // tm,),
        in_specs=[pl.BlockSpec((tm, d), lambda i: (i, 0)),
                  pl.BlockSpec((tm, x.shape[1]), lambda i: (i, 0)),
                  pl.BlockSpec(w_out.shape, lambda i: (0, 0), pipeline_mode=pl.Buffered(1)),
                  pl.BlockSpec((1, d), lambda i: (0, 0)),
                  pl.BlockSpec((d, LANES), lambda i: (0, 0))],
        out_specs=[pl.BlockSpec((tm, d), lambda i: (i, 0)),
                   pl.BlockSpec((tm, LANES), lambda i: (i, 0)),
                   pl.BlockSpec((1, LANES), lambda i: (0, 0))],
        out_shape=[jax.ShapeDtypeStruct((n_tok, d), F32),
                   jax.ShapeDtypeStruct((n_tok, LANES), F32),
                   jax.ShapeDtypeStruct((1, LANES), F32)],
        scratch_shapes=[pltpu.VMEM((1, LANES), F32)],
        compiler_params=_cparams(1),
        name="moe_router",
    )(h, x, w_out, g.reshape(1, d), w_pad)


def _dispatch_body(info_ref, h_ref, g_ref, pos_ref, xs_ref, nbuf, zbuf, idx, sems, isem, *, tm):
    i = pl.program_id(0)
    last = pl.num_programs(0) - 1
    slot = i % 2
    tg = zbuf.shape[0]

    def zero_copy(e):
        return pltpu.make_async_copy(
            zbuf, xs_ref.at[pl.ds(pl.multiple_of(info_ref[e], tg), tg)], sems.at[0, 0])

    @pl.when(i == 0)
    def _():
        zbuf[...] = jnp.zeros_like(zbuf)
        for e in range(2 * N_EXPERTS):
            @pl.when(info_ref[e] >= 0)
            def _():
                zero_copy(e).start()
        for e in range(2 * N_EXPERTS):
            @pl.when(info_ref[e] >= 0)
            def _():
                zero_copy(e).wait()

    def drain(s):
        for k in range(2):
            pltpu.make_async_copy(nbuf.at[s], xs_ref.at[pl.ds(0, tm)], sems.at[s, k]).wait()

    @pl.when(i >= 2)
    def _():
        drain(slot)

    idx_copy = pltpu.make_async_copy(pos_ref.at[i], idx, isem)
    idx_copy.start()
    nbuf[slot] = _rms(h_ref[...], g_ref[...])
    idx_copy.wait()

    def issue(j, carry):
        t0 = pl.multiple_of(j * ROW_UNROLL, ROW_UNROLL)
        for u in range(ROW_UNROLL):
            for k in range(2):
                pltpu.make_async_copy(nbuf.at[slot, pl.ds(t0 + u, 1)],
                                      xs_ref.at[pl.ds(idx[k * tm + t0 + u], 1)],
                                      sems.at[slot, k]).start()
        return carry

    lax.fori_loop(0, tm // ROW_UNROLL, issue, 0)

    @pl.when(i == last)
    def _():
        drain(slot)

        @pl.when(i >= 1)
        def _():
            drain(1 - slot)


def _dispatch(h, g, pos_tiles, info, rows_pad, *, tm, tg):
    n_tok, d = h.shape
    return pl.pallas_call(
        functools.partial(_dispatch_body, tm=tm),
        grid_spec=pltpu.PrefetchScalarGridSpec(
            num_scalar_prefetch=1,
            grid=(n_tok // tm,),
            in_specs=[pl.BlockSpec((tm, d), lambda i, info: (i, 0)),
                      pl.BlockSpec((1, d), lambda i, info: (0, 0)),
                      pl.BlockSpec(memory_space=pl.ANY)],
            out_specs=pl.BlockSpec(memory_space=pl.ANY),
            scratch_shapes=[pltpu.VMEM((2, tm, d), F32),
                            pltpu.VMEM((tg, d), F32),
                            pltpu.SMEM((2 * tm,), I32),
                            pltpu.SemaphoreType.DMA((2, 2)),
                            pltpu.SemaphoreType.DMA]),
        out_shape=jax.ShapeDtypeStruct((rows_pad, d), F32),
        compiler_params=_cparams(1),
        name="moe_dispatch",
    )(info, h, g.reshape(1, d), pos_tiles)


def _gmm_body(te_ref, nu_ref, x_ref, wg_ref, wu_ref, wd_ref, y_ref, *, ff_chunk):
    @pl.when(pl.program_id(0) >= nu_ref[0])
    def _():
        y_ref[...] = jnp.zeros_like(y_ref)

    @pl.when(pl.program_id(0) < nu_ref[0])
    def _():
        x = x_ref[...].astype(BF16)
        acc = None
        for c in range(0, wg_ref.shape[1], ff_chunk):
            a = jnp.dot(x, wg_ref[:, c:c + ff_chunk], preferred_element_type=F32)
            b = jnp.dot(x, wu_ref[:, c:c + ff_chunk], preferred_element_type=F32)
            hid = (_silu(a) * b).astype(BF16)
            part = jnp.dot(hid, wd_ref[c:c + ff_chunk, :], preferred_element_type=F32)
            acc = part if acc is None else acc + part
        y_ref[...] = acc


def _gmm(xs, tile_expert, n_used, wg, wu, wd, layer, *, tm, ff_chunk=7 * MXU_DIM):
    rows_pad, d = xs.shape
    ff = wg.shape[3]
    assert ff % ff_chunk == 0 and ff_chunk % MXU_DIM == 0
    tile = lambda i, te, nu: jnp.minimum(i, nu[0] - 1)
    wspec = lambda shape: pl.BlockSpec(
        (None, None) + shape, lambda i, te, nu: (layer, te[tile(i, te, nu)], 0, 0),
        pipeline_mode=pl.Buffered(1))
    return pl.pallas_call(
        functools.partial(_gmm_body, ff_chunk=ff_chunk),
        grid_spec=pltpu.PrefetchScalarGridSpec(
            num_scalar_prefetch=2,
            grid=(rows_pad // tm,),
            in_specs=[pl.BlockSpec((tm, d), lambda i, te, nu: (tile(i, te, nu), 0)),
                      wspec((d, ff)), wspec((d, ff)), wspec((ff, d))],
            out_specs=pl.BlockSpec((tm, d), lambda i, te, nu: (i, 0))),
        out_shape=jax.ShapeDtypeStruct((rows_pad, d), F32),
        compiler_params=_cparams(1),
        name="moe_experts",
    )(tile_expert, n_used, xs, wg, wu, wd)


def _combine_body(h_ref, route_ref, fg_ref, pos_ref, ys_ref, o_ref, gbuf, idx, sems, isem, *,
                  tm, final_norm):
    i = pl.program_id(0)
    slot = i % 2

    def gather(tile, s):
        idx_copy = pltpu.make_async_copy(pos_ref.at[tile], idx, isem)
        idx_copy.start()
        idx_copy.wait()

        def issue(j, carry):
            t0 = pl.multiple_of(j * ROW_UNROLL, ROW_UNROLL)
            for u in range(ROW_UNROLL):
                for k in range(2):
                    pltpu.make_async_copy(ys_ref.at[pl.ds(idx[k * tm + t0 + u], 1)],
                                          gbuf.at[s, k, pl.ds(t0 + u, 1)], sems.at[s, k]).start()
            return carry

        lax.fori_loop(0, tm // ROW_UNROLL, issue, 0)

    @pl.when(i == 0)
    def _():
        gather(0, 0)

    @pl.when(i + 1 < pl.num_programs(0))
    def _():
        gather(i + 1, 1 - slot)

    for k in range(2):
        pltpu.make_async_copy(ys_ref.at[pl.ds(0, tm)], gbuf.at[slot, k], sems.at[slot, k]).wait()

    route = route_ref[...]
    out = h_ref[...] + route[:, 2:3] * gbuf[slot, 0] + route[:, 3:4] * gbuf[slot, 1]
    if final_norm:
        out = _rms(out, fg_ref[...])
    o_ref[...] = out


def _combine(h, route, final_g, pos_tiles, ys, *, tm, final_norm):
    n_tok, d = h.shape
    return pl.pallas_call(
        functools.partial(_combine_body, tm=tm, final_norm=final_norm),
        grid=(n_tok // tm,),
        in_specs=[pl.BlockSpec((tm, d), lambda i: (i, 0)),
                  pl.BlockSpec((tm, LANES), lambda i: (i, 0)),
                  pl.BlockSpec((1, d), lambda i: (0, 0)),
                  pl.BlockSpec(memory_space=pl.ANY),
                  pl.BlockSpec(memory_space=pl.ANY)],
        out_specs=pl.BlockSpec((tm, d), lambda i: (i, 0)),
        out_shape=jax.ShapeDtypeStruct((n_tok, d), F32),
        scratch_shapes=[pltpu.VMEM((2, 2, tm, d), F32),
                        pltpu.SMEM((2 * tm,), I32),
                        pltpu.SemaphoreType.DMA((2, 2)),
                        pltpu.SemaphoreType.DMA],
        compiler_params=_cparams(1),
        name="moe_combine",
    )(h, route, final_g.reshape(1, d), pos_tiles, ys)


def _moe(h, x, w_out, norm_g, w_router, wg, wu, wd, layer, final_g, *, final_norm, tm_route=1024,
         tm_gmm=512, tm_rows=1024):
    n_tok, d = h.shape
    w_pad = jnp.zeros((d, LANES), F32).at[:, :N_EXPERTS].set(w_router.astype(F32))
    h, route, counts = _router(h, x, w_out, norm_g, w_pad, tm=tm_route)

    cnt = counts[0, :N_EXPERTS].astype(I32)
    gsz = ((cnt + tm_gmm - 1) // tm_gmm) * tm_gmm
    ends = jnp.cumsum(gsz)
    offs = ends - gsz
    rows_pad = (n_tok * 2 // tm_gmm + N_EXPERTS) * tm_gmm
    n_tiles = rows_pad // tm_gmm
    tile_expert = jnp.minimum(
        jnp.searchsorted(ends, jnp.arange(n_tiles, dtype=I32) * tm_gmm, side="right"),
        N_EXPERTS - 1).astype(I32)
    n_used = jnp.maximum(ends[-1:] // tm_gmm, 1).astype(I32)
    tail = ends[-1] + jnp.arange(N_EXPERTS, dtype=I32) * tm_gmm
    info = jnp.concatenate([jnp.where(gsz > 0, ends - tm_gmm, -1),
                            jnp.where(tail < rows_pad, tail, -1)]).astype(I32)

    e01 = route[:, 0:2].astype(I32)
    pos = jnp.clip(offs[jnp.clip(e01, 0, N_EXPERTS - 1)] + route[:, 4:6].astype(I32),
                   0, rows_pad - 1)
    pos_tiles = pos.reshape(n_tok // tm_rows, tm_rows, 2).transpose(0, 2, 1).reshape(
        n_tok // tm_rows, 2 * tm_rows)

    xs = _dispatch(h, norm_g, pos_tiles, info, rows_pad, tm=tm_rows, tg=tm_gmm)
    ys = _gmm(xs, tile_expert, n_used, wg, wu, wd, layer, tm=tm_gmm)
    return _combine(h, route, final_g, pos_tiles, ys, tm=tm_rows, final_norm=final_norm)


def kernel(x, positions, norm_mix_g, norm_ffn_g, final_norm_g, ab_w_in, gla_w_a2, gla_b_a2,
           ret_norm_g, gla_norm_g, ab_w_out, ffn_w_gate, ffn_w_up, ffn_w_down,
           hgrn_lb_logits, c_w_in, hgrn_norm_g, c_w_out, moe_router, moe_w_gate,
           moe_w_up, moe_w_down):
    batch, seq, d = x.shape
    depth = norm_mix_g.shape[0]
    n_tok = batch * seq
    ret_heads = 4
    gla_heads = 4
    gla_rank = gla_w_a2.shape[1]
    hgrn_heads = d // LANES
    ab_cols = ab_w_in.shape[2] - gla_rank

    lb_cum = jnp.cumsum(jax.nn.softmax(hgrn_lb_logits.astype(F32), axis=0), axis=0)
    lower_bounds = lb_cum - lb_cum[0:1]

    wsum = jnp.asarray(_decay_sum_matrix(), BF16)
    inv_freq = ROPE_BASE ** (-jnp.arange(0, LANES, 2, dtype=F32) / LANES)
    inv_freq2 = jnp.concatenate([inv_freq, inv_freq]).reshape(1, LANES)
    cos_t, sin_t = _rope_tables(positions.reshape(n_tok, 1), inv_freq2)

    moe_wg, moe_wu, moe_wd = (w.astype(BF16) for w in (moe_w_gate, moe_w_up, moe_w_down))

    h = x.reshape(n_tok, d)
    for l in range(depth):
        e = l // 2
        if l % 2 == 0:
            w_in = jnp.pad(ab_w_in[e], ((0, 0), (0, LANES - gla_rank))).astype(BF16)
            z = _norm_proj(h, norm_mix_g[l], w_in)
            wa_pad = jnp.pad(gla_w_a2[e], ((0, LANES - gla_rank), (0, 0))).astype(BF16)
            o = _retgla(z, cos_t, sin_t, ret_norm_g[e], wsum, wa_pad, gla_b_a2[e], gla_norm_g[e],
                        batch=batch, seq=seq, ret_heads=ret_heads, gla_heads=gla_heads,
                        col_a=ab_cols)
            h = _ffn(h, [o], [ab_w_out[e].astype(BF16)], norm_ffn_g[l],
                     ffn_w_gate[e].astype(BF16), ffn_w_up[e].astype(BF16),
                     ffn_w_down[e].astype(BF16))
        else:
            z = _norm_proj(h, norm_mix_g[l], c_w_in[e].astype(BF16))
            o = _hgrn(z, wsum, lower_bounds[l], hgrn_norm_g[e], batch=batch, seq=seq,
                      n_heads=hgrn_heads)
            h = _moe(h, o, c_w_out[e].astype(BF16), norm_ffn_g[l], moe_router[e],
                     moe_wg, moe_wu, moe_wd, e, final_norm_g, final_norm=(l == depth - 1))
    if depth % 2 == 1:
        raise NotImplementedError("final norm is fused into the last (odd) layer")
    return h.reshape(batch, seq, d)
```

```python
import functools
import math

import numpy as np
import jax
import jax.numpy as jnp
from jax import lax
from jax.experimental import pallas as pl
from jax.experimental.pallas import tpu as pltpu

F32 = jnp.float32
BF16 = jnp.bfloat16
I32 = jnp.int32

EPS = 1e-6
LANES = 128
MXU_DIM = 256
CHUNK = 64
N_LEVELS = CHUNK.bit_length() - 1
HEAD_DV = 128
ROPE_BASE = 10000.0
GLA_TAU = 16.0
LOG2_E = 1.4426950408889634
N_EXPERTS = 8
ROUTE_SUB = 128
ROW_UNROLL = 8
VMEM_LIMIT = 56 * 1024 * 1024

_NT = (((1,), (1,)), ((), ()))
_TN = (((0,), (0,)), ((), ()))


def _cparams(n_axes):
    return pltpu.CompilerParams(dimension_semantics=("arbitrary",) * n_axes,
                                vmem_limit_bytes=VMEM_LIMIT)


def _rms(x, g):
    ms = jnp.mean(x * x, axis=-1, keepdims=True)
    return x * lax.rsqrt(ms + EPS) * g


def _silu(x, scale=1.0):
    t = jnp.tanh(0.5 * x)
    hx = x * (0.5 * scale)
    return hx + hx * t


def _bdot_nt(a, b):
    return lax.dot_general(a.astype(BF16), b.astype(BF16), _NT, preferred_element_type=F32)


def _norm_proj_body(h_ref, g_ref, w_ref, z_ref, *, col_chunk):
    n = _rms(h_ref[...], g_ref[...]).astype(BF16)
    m = w_ref.shape[1]
    for c in range(0, m, col_chunk):
        ce = min(c + col_chunk, m)
        z_ref[:, c:ce] = jnp.dot(n, w_ref[:, c:ce], preferred_element_type=F32).astype(z_ref.dtype)


def _norm_proj(h, g, w, *, tm=1024, col_chunk=512):
    n_tok, d = h.shape
    m = w.shape[1]
    assert n_tok % tm == 0
    return pl.pallas_call(
        functools.partial(_norm_proj_body, col_chunk=col_chunk),
        grid=(n_tok // tm,),
        in_specs=[pl.BlockSpec((tm, d), lambda i: (i, 0)),
                  pl.BlockSpec((1, d), lambda i: (0, 0)),
                  pl.BlockSpec((d, m), lambda i: (0, 0), pipeline_mode=pl.Buffered(1))],
        out_specs=pl.BlockSpec((tm, m), lambda i: (i, 0)),
        out_shape=jax.ShapeDtypeStruct((n_tok, m), BF16),
        compiler_params=_cparams(1),
        name="norm_proj",
    )(h, g.reshape(1, d), w)


def _ffn_body(*refs, n_in, ff_chunk):
    h_ref = refs[0]
    x_refs = refs[1:1 + n_in]
    w_refs = refs[1 + n_in:1 + 2 * n_in]
    g_ref, wg_ref, wu_ref, wd_ref, o_ref = refs[1 + 2 * n_in:]
    h = h_ref[...]
    for x_ref, w_ref in zip(x_refs, w_refs):
        h = h + jnp.dot(x_ref[...], w_ref[...], preferred_element_type=F32)
    o_ref[...] = h
    n = _rms(h, g_ref[...]).astype(BF16)
    acc = None
    ff = wg_ref.shape[1]
    for c in range(0, ff, ff_chunk):
        ce = min(c + ff_chunk, ff)
        a = jnp.dot(n, wg_ref[:, c:ce], preferred_element_type=F32)
        b = jnp.dot(n, wu_ref[:, c:ce], preferred_element_type=F32)
        hid = (_silu(a) * b).astype(BF16)
        part = jnp.dot(hid, wd_ref[c:ce, :], preferred_element_type=F32)
        acc = part if acc is None else acc + part
    o_ref[...] = o_ref[...] + acc


def _ffn(h, xs, ws, g, wg, wu, wd, *, tm=512, ff_chunk=6 * MXU_DIM):
    n_tok, d = h.shape
    ff = wg.shape[1]
    assert ff % MXU_DIM == 0
    const = lambda shape: pl.BlockSpec(shape, lambda i: (0, 0), pipeline_mode=pl.Buffered(1))
    in_specs = [pl.BlockSpec((tm, d), lambda i: (i, 0))]
    in_specs += [pl.BlockSpec((tm, x.shape[1]), lambda i: (i, 0)) for x in xs]
    in_specs += [const(w.shape) for w in ws]
    in_specs += [pl.BlockSpec((1, d), lambda i: (0, 0)), const((d, ff)), const((d, ff)), const((ff, d))]
    return pl.pallas_call(
        functools.partial(_ffn_body, n_in=len(xs), ff_chunk=ff_chunk),
        grid=(n_tok // tm,),
        in_specs=in_specs,
        out_specs=pl.BlockSpec((tm, d), lambda i: (i, 0)),
        out_shape=jax.ShapeDtypeStruct((n_tok, d), F32),
        compiler_params=_cparams(1),
        name="ffn_swiglu",
    )(h, *xs, *ws, g.reshape(1, d), wg, wu, wd)


def _decay_sum_matrix():
    c = CHUNK
    r = np.arange(c)[:, None]
    t = np.arange(c)[None, :]
    blocks = [(t <= r), (t > r)]
    for lvl in range(N_LEVELS):
        s = c >> (lvl + 1)
        m = (r // (2 * s)) * (2 * s) + s - 1
        upper = r > m
        blocks.append(np.where(upper, (t > m) & (t <= r), (t > r) & (t <= m)))
    w = np.concatenate(blocks, axis=0).astype(np.float32)
    return np.concatenate([w, w], axis=1)


def _level_masks():
    c = CHUNK
    assert 2 * c == LANES
    row = lax.broadcasted_iota(I32, (c, LANES), 0)
    col = lax.broadcasted_iota(I32, (c, LANES), 1) % c
    uppers, pairs = [], []
    for lvl in range(N_LEVELS):
        s = c >> (lvl + 1)
        uppers.append((row // s) % 2 == 1)
        pairs.append(((row // (2 * s)) == (col // (2 * s)))
                     & ((row // s) % 2 == 1) & ((col // s) % 2 == 0))
    return uppers, pairs, row == col


def _single_head_masks():
    c = CHUNK
    ri = lax.broadcasted_iota(I32, (c, c), 0)
    ci = lax.broadcasted_iota(I32, (c, c), 1)
    pairs = []
    for lvl in range(N_LEVELS):
        s = c >> (lvl + 1)
        pairs.append(((ri // (2 * s)) == (ci // (2 * s)))
                     & ((ri // s) % 2 == 1) & ((ci // s) % 2 == 0))
    return pairs, ri == ci


def _block_diag(a, b):
    z = jnp.zeros_like(a)
    return jnp.concatenate([jnp.concatenate([a, z], axis=1),
                            jnp.concatenate([z, b], axis=1)], axis=0)


def _gated_chunk(qs, ks, lfs, vss, lmss, sts, wsum2, masks, pack_pairs):
    c = CHUNK
    uppers, pairs, eye, pairs1, eye1 = masks
    n_g = len(qs)
    pieces = []
    for lf in lfs:
        lf2 = lf * LOG2_E
        hi = lf2.astype(BF16)
        lo = (lf2 - hi.astype(F32)).astype(BF16)
        pieces.append(jnp.concatenate([hi, lo], axis=0))
    ex = jnp.dot(wsum2, jnp.concatenate(pieces, axis=1), preferred_element_type=F32)
    fac = jnp.exp2(ex)
    fac_b = fac.astype(BF16)

    heads = []
    for g in range(n_g):
        cols = slice(g * LANES, (g + 1) * LANES)
        q_b = qs[g].astype(BF16)
        k_b = ks[g].astype(BF16)
        w = [jnp.where(uppers[l], q_b, k_b) * fac_b[(2 + l) * c:(3 + l) * c, cols]
             for l in range(N_LEVELS)]
        qf = q_b * fac_b[0:c, cols]
        kf = k_b * fac_b[c:2 * c, cols]
        st_b = sts[g].astype(BF16)
        qk = qs[g] * ks[g]
        for v, lm in zip(vss[g], lmss[g]):
            if lm is None:
                sel = lambda a: a
            else:
                sel = lambda a, lm=lm: jnp.where(lm, a, jnp.zeros_like(a))
            heads.append(dict(lhs=w, rhs=[sel(x) for x in w], qf=sel(qf), kf=sel(kf), st=st_b, v=v,
                              diag=jnp.sum(sel(qk), axis=-1, keepdims=True)))

    pair_outs = []
    state_update = lambda a: lax.dot_general(a["v"], a["kf"], _TN, preferred_element_type=F32)
    if pack_pairs:
        assert len(heads) % 2 == 0
        lane_lo = lax.broadcasted_iota(I32, (1, LANES), 1) < c
        updates = [state_update(a) for a in heads]
        grams, inters = [], []
        for a, b in zip(heads[0::2], heads[1::2]):
            grams.append([lax.dot_general(jnp.concatenate([a["lhs"][l], b["lhs"][l]], axis=1),
                                          _block_diag(a["rhs"][l], b["rhs"][l]), _NT,
                                          preferred_element_type=F32) for l in range(N_LEVELS)])
            inters.append(lax.dot_general(jnp.concatenate([a["qf"], b["qf"]], axis=1),
                                          _block_diag(a["st"], b["st"]), _NT,
                                          preferred_element_type=F32))
        for p, (a, b) in enumerate(zip(heads[0::2], heads[1::2])):
            scores = jnp.where(eye, jnp.where(lane_lo, a["diag"], b["diag"]), 0.0)
            for l in range(N_LEVELS):
                scores = jnp.where(pairs[l], grams[p][l], scores)
            o2 = jnp.dot(scores.astype(BF16), _block_diag(a["v"], b["v"]),
                         preferred_element_type=F32) + inters[p]
            pair_outs += [o2[:, :HEAD_DV], o2[:, HEAD_DV:]]
    else:
        grams = [[lax.dot_general(a["lhs"][l], a["rhs"][l], _NT, preferred_element_type=F32)
                  for l in range(N_LEVELS)] for a in heads]
        updates = [state_update(a) for a in heads]
        inters = [lax.dot_general(a["qf"], a["st"], _NT, preferred_element_type=F32)
                  for a in heads]
        for p, a in enumerate(heads):
            scores = jnp.where(eye1, a["diag"], 0.0)
            for l in range(N_LEVELS):
                scores = jnp.where(pairs1[l], grams[p][l], scores)
            pair_outs.append(jnp.dot(scores.astype(BF16), a["v"],
                                     preferred_element_type=F32) + inters[p])

    outs, new_sts = [], []
    h = 0
    for g in range(n_g):
        cols = slice(g * LANES, (g + 1) * LANES)
        new_st = sts[g] * fac[c - 1:c, cols]
        outs_g = []
        for _ in vss[g]:
            outs_g.append(pair_outs[h])
            new_st = new_st + updates[h]
            h += 1
        outs.append(outs_g)
        new_sts.append(new_st)
    return outs, new_sts


def _head_out(o, norm_g, gate):
    return (_rms(o, norm_g) * _silu(gate)).astype(BF16)


def _rope_body(pos_ref, invf_ref, cos_ref, sin_ref):
    ang = pos_ref[...].astype(F32) * invf_ref[...]
    lane = lax.broadcasted_iota(I32, (1, LANES), 1)
    cos_ref[...] = jnp.cos(ang)
    sin_ref[...] = jnp.sin(ang) * jnp.where(lane < LANES // 2, -1.0, 1.0)


def _rope_tables(pos_col, inv_freq2, *, tm=2048):
    n_tok = pos_col.shape[0]
    tm = min(tm, n_tok)
    table = jax.ShapeDtypeStruct((n_tok, LANES), F32)
    return pl.pallas_call(
        _rope_body,
        grid=(n_tok // tm,),
        in_specs=[pl.BlockSpec((tm, 1), lambda i: (i, 0)),
                  pl.BlockSpec((1, LANES), lambda i: (0, 0))],
        out_specs=[pl.BlockSpec((tm, LANES), lambda i: (i, 0))] * 2,
        out_shape=[table, table],
        compiler_params=_cparams(1),
        name="rope_tables",
    )(pos_col, inv_freq2)


def _retgla_body(cos_ref, sin_ref, rng_ref, wsum_ref, wa_ref, ba2_ref, gng_ref,
                aq_ref, ak_ref, av_ref, ag_ref, ba_ref, bq_ref, bk_ref, bv_ref, bg_ref, o_ref,
                rst_ref, gst_ref, lf_ref, *, ret_heads, gla_groups):
    c = CHUNK
    tb = aq_ref.shape[0]

    @pl.when(pl.program_id(1) == 0)
    def _():
        rst_ref[...] = jnp.zeros_like(rst_ref)
        gst_ref[...] = jnp.zeros_like(gst_ref)

    x = jnp.dot(ba_ref[...], wa_ref[...], preferred_element_type=F32) + ba2_ref[...]
    lf_ref[...] = (jnp.minimum(x, 0.0) - jnp.log1p(jnp.exp(-jnp.abs(x)))) * (1.0 / GLA_TAU)

    ri = lax.broadcasted_iota(I32, (c, c), 0)
    ci = lax.broadcasted_iota(I32, (c, c), 1)
    rel = (ri - ci).astype(F32)
    trow = lax.broadcasted_iota(I32, (c, LANES), 0).astype(F32)
    k_scale = float(LANES) ** -0.5
    log_gammas = [math.log1p(-(2.0 ** (-5 - h))) for h in range(ret_heads)]
    dmats = [jnp.where(rel >= 0, jnp.exp(rel * lg), 0.0) for lg in log_gammas]
    q_decays = [jnp.exp((trow + 1.0) * lg) for lg in log_gammas]
    k_decays = [jnp.exp((c - 1.0 - trow) * lg) * k_scale for lg in log_gammas]
    hcols = [slice(h * LANES, (h + 1) * LANES) for h in range(ret_heads)]
    ret_w = ret_heads * HEAD_DV

    masks = _level_masks() + _single_head_masks()
    lane = lax.broadcasted_iota(I32, (1, LANES), 1)
    lane_masks = [lane < LANES // 2, lane >= LANES // 2]
    wsum = wsum_ref[...]
    rng = rng_ref[...]
    gng = gng_ref[...]
    q_scale = float(LANES // 2) ** -0.5
    kcols = [slice(g * LANES, (g + 1) * LANES) for g in range(gla_groups)]
    vcols = [[slice((2 * g + j) * HEAD_DV, (2 * g + j + 1) * HEAD_DV) for j in range(2)]
             for g in range(gla_groups)]

    def chunk(ic, carry):
        r0 = pl.multiple_of(ic * c, c)
        rows = pl.ds(r0, c)
        cosv = cos_ref[rows, :]
        sinv = sin_ref[rows, :]
        scores, inters, updates, vs = [], [], [], []
        for h, cols in enumerate(hcols):
            q = aq_ref[rows, cols].astype(F32)
            k = ak_ref[rows, cols].astype(F32)
            v = av_ref[rows, cols]
            qr = q * cosv + pltpu.roll(q, LANES // 2, 1) * sinv
            kr = k * cosv + pltpu.roll(k, LANES // 2, 1) * sinv
            scores.append(_bdot_nt(qr, kr * k_scale))
            inters.append(_bdot_nt(qr * q_decays[h], rst_ref[h]))
            updates.append(lax.dot_general(v, (kr * k_decays[h]).astype(BF16), _TN,
                                           preferred_element_type=F32))
            vs.append(v)
        outs, new_sts = _gated_chunk(
            [bq_ref[rows, kc].astype(F32) * q_scale for kc in kcols],
            [bk_ref[rows, kc].astype(F32) for kc in kcols],
            [lf_ref[rows, kc] for kc in kcols],
            [[bv_ref[rows, vc] for vc in vcs] for vcs in vcols],
            [lane_masks] * gla_groups,
            [gst_ref[g] for g in range(gla_groups)],
            wsum, masks, pack_pairs=True)
        for g in range(gla_groups):
            gst_ref[g] = new_sts[g]
            for o, vc in zip(outs[g], vcols[g]):
                oc = slice(ret_w + vc.start, ret_w + vc.stop)
                o_ref[rows, oc] = _head_out(o, gng, bg_ref[rows, vc].astype(F32))
        for h, cols in enumerate(hcols):
            o = jnp.dot((scores[h] * dmats[h]).astype(BF16), vs[h],
                        preferred_element_type=F32) + inters[h]
            rst_ref[h] = math.exp(c * log_gammas[h]) * rst_ref[h] + updates[h]
            o_ref[rows, cols] = _head_out(o, rng, ag_ref[rows, cols].astype(F32))
        return carry

    lax.fori_loop(0, tb // c, chunk, 0, unroll=8)


def _retgla(z, cos_t, sin_t, ret_g, wsum, wa_pad, b_a2, gla_g, *, batch, seq, ret_heads,
            gla_heads, col_a, tb=512):
    n_tok = z.shape[0]
    wr = ret_heads * LANES
    wk = gla_heads * (LANES // 2)
    wv = gla_heads * HEAD_DV
    col0 = 4 * wr
    assert seq % tb == 0 and tb % CHUNK == 0
    nt = seq // tb
    row = lambda b, t: b * nt + t
    const = lambda shape: pl.BlockSpec(shape, lambda b, t: (0, 0))
    zblk = lambda w, j: pl.BlockSpec((tb, w), lambda b, t: (row(b, t), j))
    tspec = pl.BlockSpec((tb, LANES), lambda b, t: (row(b, t), 0))
    return pl.pallas_call(
        functools.partial(_retgla_body, ret_heads=ret_heads, gla_groups=gla_heads // 2),
        grid=(batch, nt),
        in_specs=[tspec, tspec, const((1, LANES)),
                  const(wsum.shape), const(wa_pad.shape), const((1, wk)), const((1, LANES)),
                  zblk(wr, 0), zblk(wr, 1), zblk(wr, 2), zblk(wr, 3),
                  zblk(LANES, col_a // LANES),
                  zblk(wk, col0 // wk), zblk(wk, col0 // wk + 1),
                  zblk(wv, (col0 + 2 * wk) // wv), zblk(wv, (col0 + 2 * wk) // wv + 1)],
        out_specs=pl.BlockSpec((tb, wr + wv), lambda b, t: (row(b, t), 0)),
        out_shape=jax.ShapeDtypeStruct((n_tok, wr + wv), BF16),
        scratch_shapes=[pltpu.VMEM((ret_heads, HEAD_DV, LANES), F32),
                        pltpu.VMEM((gla_heads // 2, HEAD_DV, LANES), F32),
                        pltpu.VMEM((tb, wk), F32)],
        compiler_params=_cparams(2),
        name="retention_gla",
    )(cos_t, sin_t, ret_g.reshape(1, LANES), wsum, wa_pad, b_a2.reshape(1, wk),
      gla_g.reshape(1, LANES), z, z, z, z, z, z, z, z, z)


def _hgrn_body(wsum_ref, lb_ref, ng_ref, q_ref, f_ref, i_ref, gate_ref, o_ref, st_ref, *, n_heads):
    c = CHUNK
    tb = q_ref.shape[0]

    @pl.when(pl.program_id(2) == 0)
    def _():
        st_ref[...] = jnp.zeros_like(st_ref)

    masks = _level_masks() + _single_head_masks()
    wsum = wsum_ref[...]
    ng = ng_ref[...]
    q_scale = float(LANES) ** -0.5

    def chunk(ic, carry):
        r0 = pl.multiple_of(ic * c, c)
        rows = pl.ds(r0, c)
        hcols = [slice(h * LANES, (h + 1) * LANES) for h in range(n_heads)]
        qs, ks, lfs = [], [], []
        for cols in hcols:
            lb = lb_ref[:, cols]
            f = f_ref[rows, cols].astype(F32)
            t = jnp.tanh(0.5 * f)
            b = 0.5 * (1.0 - lb)
            bt = b * t
            lfs.append(jnp.log((0.5 * (1.0 + lb)) + bt))
            ks.append(b - bt)
            qs.append(_silu(q_ref[rows, cols].astype(F32), q_scale))
        outs, new_sts = _gated_chunk(
            qs, ks, lfs, [[i_ref[rows, cols]] for cols in hcols], [[None]] * n_heads,
            [st_ref[h] for h in range(n_heads)], wsum, masks, pack_pairs=False)
        for h, cols in enumerate(hcols):
            st_ref[h] = new_sts[h]
            o_ref[rows, cols] = _head_out(outs[h][0], ng, gate_ref[rows, cols].astype(F32))
        return carry

    lax.fori_loop(0, tb // c, chunk, 0, unroll=2)


def _hgrn(z, wsum, lb, norm_g, *, batch, seq, n_heads, heads_per_step=8, tb=512):
    n_tok = z.shape[0]
    w = heads_per_step * LANES
    ng_ = n_heads // heads_per_step
    assert seq % tb == 0 and tb % CHUNK == 0
    nt = seq // tb
    zspec = lambda j: pl.BlockSpec((tb, w), lambda b, g, t, j=j: (b * nt + t, j * ng_ + g))
    return pl.pallas_call(
        functools.partial(_hgrn_body, n_heads=heads_per_step),
        grid=(batch, ng_, nt),
        in_specs=[pl.BlockSpec(wsum.shape, lambda b, g, t: (0, 0)),
                  pl.BlockSpec((1, w), lambda b, g, t: (0, g)),
                  pl.BlockSpec((1, LANES), lambda b, g, t: (0, 0)),
                  zspec(0), zspec(1), zspec(2), zspec(3)],
        out_specs=pl.BlockSpec((tb, w), lambda b, g, t: (b * nt + t, g)),
        out_shape=jax.ShapeDtypeStruct((n_tok, n_heads * LANES), BF16),
        scratch_shapes=[pltpu.VMEM((heads_per_step, HEAD_DV, LANES), F32)],
        compiler_params=_cparams(3),
        name="hgrn2",
    )(wsum, lb.reshape(1, n_heads * LANES), norm_g.reshape(1, LANES), z, z, z, z)


def _router_body(h_ref, x_ref, wo_ref, g_ref, w_ref, h1_ref, route_ref, cnt_ref, run_ref):
    tm = h_ref.shape[0]

    @pl.when(pl.program_id(0) == 0)
    def _():
        run_ref[...] = jnp.zeros_like(run_ref)

    ts = ROUTE_SUB
    w = w_ref[...]
    w_hi = w.astype(BF16)
    w_lo = (w - w_hi.astype(F32)).astype(BF16)
    lane = lax.broadcasted_iota(I32, (ts, LANES), 1)
    lane_f = lane.astype(F32)
    ri = lax.broadcasted_iota(I32, (ts, ts), 0)
    ci = lax.broadcasted_iota(I32, (ts, ts), 1)
    lstrict = jnp.where(ri > ci, 1.0, 0.0).astype(BF16)
    neg = -jnp.inf
    subs = [slice(r0, r0 + ts) for r0 in range(0, tm, ts)]
    h1s = [h_ref[rows, :] + jnp.dot(x_ref[rows, :], wo_ref[...], preferred_element_type=F32)
           for rows in subs]
    logits = []
    for rows, h1 in zip(subs, h1s):
        h1_ref[rows, :] = h1
        n = _rms(h1, g_ref[...])
        n_hi = n.astype(BF16)
        n_lo = (n - n_hi.astype(F32)).astype(BF16)
        logits.append(jnp.dot(n_hi, w_hi, preferred_element_type=F32)
                      + jnp.dot(n_hi, w_lo, preferred_element_type=F32)
                      + jnp.dot(n_lo, w_hi, preferred_element_type=F32))
    picks = []
    for lg in logits:
        lg1 = jnp.where(lane < N_EXPERTS, lg, neg)
        m1 = jnp.max(lg1, axis=-1, keepdims=True)
        i1 = jnp.min(jnp.where(lg1 == m1, lane_f, float(LANES)), axis=-1, keepdims=True)
        oh1 = lane_f == i1
        lg2 = jnp.where(oh1, neg, lg1)
        m2 = jnp.max(lg2, axis=-1, keepdims=True)
        i2 = jnp.min(jnp.where(lg2 == m2, lane_f, float(LANES)), axis=-1, keepdims=True)
        oh2 = lane_f == i2
        e2 = jnp.exp(m2 - m1)
        g1 = 1.0 / (1.0 + e2)
        both = jnp.where(oh1, 1.0, 0.0) + jnp.where(oh2, 1.0, 0.0)
        prefix = jnp.dot(lstrict, both.astype(BF16), preferred_element_type=F32)
        picks.append((i1, i2, g1, e2 * g1, oh1, oh2, both, prefix))
    run = run_ref[...]
    for rows, (i1, i2, g1, g2, oh1, oh2, both, prefix) in zip(subs, picks):
        before = prefix + run
        r1 = jnp.sum(jnp.where(oh1, before, 0.0), axis=-1, keepdims=True)
        r2 = jnp.sum(jnp.where(oh2, before, 0.0), axis=-1, keepdims=True)
        run = run + jnp.sum(both, axis=0, keepdims=True)
        out = jnp.where(lane == 0, i1, 0.0)
        out = jnp.where(lane == 1, i2, out)
        out = jnp.where(lane == 2, g1, out)
        out = jnp.where(lane == 3, g2, out)
        out = jnp.where(lane == 4, r1, out)
        out = jnp.where(lane == 5, r2, out)
        route_ref[rows, :] = out
    run_ref[...] = run
    cnt_ref[...] = run


def _router(h, x, w_out, g, w_pad, *, tm=512):
    n_tok, d = h.shape
    return pl.pallas_call(
        _router_body,
        grid=(n_tok // tm,),
        in_specs=[pl.BlockSpec((tm, d), lambda i: (i, 0)),
                  pl.BlockSpec((tm, x.shape[1]), lambda i: (i, 0)),
                  pl.BlockSpec(w_out.shape, lambda i: (0, 0), pipeline_mode=pl.Buffered(1)),
                  pl.BlockSpec((1, d), lambda i: (0, 0)),
                  pl.BlockSpec((d, LANES), lambda i: (0, 0))],
        out_specs=[pl.BlockSpec((tm, d), lambda i: (i, 0)),
                   pl.BlockSpec((tm, LANES), lambda i: (i, 0)),
                   pl.BlockSpec((1, LANES), lambda i: (0, 0))],
        out_shape=[jax.ShapeDtypeStruct((n_tok, d), F32),
                   jax.ShapeDtypeStruct((n_tok, LANES), F32),
                   jax.ShapeDtypeStruct((1, LANES), F32)],
        scratch_shapes=[pltpu.VMEM((1, LANES), F32)],
        compiler_params=_cparams(1),
        name="moe_router",
    )(h, x, w_out, g.reshape(1, d), w_pad)


def _dispatch_body(info_ref, h_ref, g_ref, pos_ref, xs_ref, nbuf, zbuf, idx, sems, isem, *, tm):
    i = pl.program_id(0)
    last = pl.num_programs(0) - 1
    slot = i % 2
    tg = zbuf.shape[0]

    def zero_copy(e):
        return pltpu.make_async_copy(
            zbuf, xs_ref.at[pl.ds(pl.multiple_of(info_ref[e], tg), tg)], sems.at[0, 0])

    @pl.when(i == 0)
    def _():
        zbuf[...] = jnp.zeros_like(zbuf)
        for e in range(2 * N_EXPERTS):
            @pl.when(info_ref[e] >= 0)
            def _():
                zero_copy(e).start()
        for e in range(2 * N_EXPERTS):
            @pl.when(info_ref[e] >= 0)
            def _():
                zero_copy(e).wait()

    def drain(s):
        for k in range(2):
            pltpu.make_async_copy(nbuf.at[s], xs_ref.at[pl.ds(0, tm)], sems.at[s, k]).wait()

    @pl.when(i >= 2)
    def _():
        drain(slot)

    idx_copy = pltpu.make_async_copy(pos_ref.at[i], idx, isem)
    idx_copy.start()
    nbuf[slot] = _rms(h_ref[...], g_ref[...])
    idx_copy.wait()

    def issue(j, carry):
        t0 = pl.multiple_of(j * ROW_UNROLL, ROW_UNROLL)
        for u in range(ROW_UNROLL):
            for k in range(2):
                pltpu.make_async_copy(nbuf.at[slot, pl.ds(t0 + u, 1)],
                                      xs_ref.at[pl.ds(idx[k * tm + t0 + u], 1)],
                                      sems.at[slot, k]).start()
        return carry

    lax.fori_loop(0, tm // ROW_UNROLL, issue, 0)

    @pl.when(i == last)
    def _():
        drain(slot)

        @pl.when(i >= 1)
        def _():
            drain(1 - slot)


def _dispatch(h, g, pos_tiles, info, rows_pad, *, tm, tg):
    n_tok, d = h.shape
    return pl.pallas_call(
        functools.partial(_dispatch_body, tm=tm),
        grid_spec=pltpu.PrefetchScalarGridSpec(
            num_scalar_prefetch=1,
            grid=(n_tok // tm,),
            in_specs=[pl.BlockSpec((tm, d), lambda i, info: (i, 0)),
                      pl.BlockSpec((1, d), lambda i, info: (0, 0)),
                      pl.BlockSpec(memory_space=pl.ANY)],
            out_specs=pl.BlockSpec(memory_space=pl.ANY),
            scratch_shapes=[pltpu.VMEM((2, tm, d), F32),
                            pltpu.VMEM((tg, d), F32),
                            pltpu.SMEM((2 * tm,), I32),
                            pltpu.SemaphoreType.DMA((2, 2)),
                            pltpu.SemaphoreType.DMA]),
        out_shape=jax.ShapeDtypeStruct((rows_pad, d), F32),
        compiler_params=_cparams(1),
        name="moe_dispatch",
    )(info, h, g.reshape(1, d), pos_tiles)


def _gmm_body(te_ref, nu_ref, x_ref, wg_ref, wu_ref, wd_ref, y_ref, *, ff_chunk):
    @pl.when(pl.program_id(0) >= nu_ref[0])
    def _():
        y_ref[...] = jnp.zeros_like(y_ref)

    @pl.when(pl.program_id(0) < nu_ref[0])
    def _():
        x = x_ref[...].astype(BF16)
        acc = None
        for c in range(0, wg_ref.shape[1], ff_chunk):
            a = jnp.dot(x, wg_ref[:, c:c + ff_chunk], preferred_element_type=F32)
            b = jnp.dot(x, wu_ref[:, c:c + ff_chunk], preferred_element_type=F32)
            hid = (_silu(a) * b).astype(BF16)
            part = jnp.dot(hid, wd_ref[c:c + ff_chunk, :], preferred_element_type=F32)
            acc = part if acc is None else acc + part
        y_ref[...] = acc


def _gmm(xs, tile_expert, n_used, wg, wu, wd, layer, *, tm, ff_chunk=7 * MXU_DIM):
    rows_pad, d = xs.shape
    ff = wg.shape[3]
    assert ff % ff_chunk == 0 and ff_chunk % MXU_DIM == 0
    tile = lambda i, te, nu: jnp.minimum(i, nu[0] - 1)
    wspec = lambda shape: pl.BlockSpec(
        (None, None) + shape, lambda i, te, nu: (layer, te[tile(i, te, nu)], 0, 0),
        pipeline_mode=pl.Buffered(1))
    return pl.pallas_call(
        functools.partial(_gmm_body, ff_chunk=ff_chunk),
        grid_spec=pltpu.PrefetchScalarGridSpec(
            num_scalar_prefetch=2,
            grid=(rows_pad // tm,),
            in_specs=[pl.BlockSpec((tm, d), lambda i, te, nu: (tile(i, te, nu), 0)),
                      wspec((d, ff)), wspec((d, ff)), wspec((ff, d))],
            out_specs=pl.BlockSpec((tm, d), lambda i, te, nu: (i, 0))),
        out_shape=jax.ShapeDtypeStruct((rows_pad, d), F32),
        compiler_params=_cparams(1),
        name="moe_experts",
    )(tile_expert, n_used, xs, wg, wu, wd)


def _combine_body(h_ref, route_ref, fg_ref, pos_ref, ys_ref, o_ref, gbuf, idx, sems, isem, *,
                  tm, final_norm):
    i = pl.program_id(0)
    slot = i % 2

    def gather(tile, s):
        idx_copy = pltpu.make_async_copy(pos_ref.at[tile], idx, isem)
        idx_copy.start()
        idx_copy.wait()

        def issue(j, carry):
            t0 = pl.multiple_of(j * ROW_UNROLL, ROW_UNROLL)
            for u in range(ROW_UNROLL):
                for k in range(2):
                    pltpu.make_async_copy(ys_ref.at[pl.ds(idx[k * tm + t0 + u], 1)],
                                          gbuf.at[s, k, pl.ds(t0 + u, 1)], sems.at[s, k]).start()
            return carry

        lax.fori_loop(0, tm // ROW_UNROLL, issue, 0)

    @pl.when(i == 0)
    def _():
        gather(0, 0)

    @pl.when(i + 1 < pl.num_programs(0))
    def _():
        gather(i + 1, 1 - slot)

    for k in range(2):
        pltpu.make_async_copy(ys_ref.at[pl.ds(0, tm)], gbuf.at[slot, k], sems.at[slot, k]).wait()

    route = route_ref[...]
    out = h_ref[...] + route[:, 2:3] * gbuf[slot, 0] + route[:, 3:4] * gbuf[slot, 1]
    if final_norm:
        out = _rms(out, fg_ref[...])
    o_ref[...] = out


def _combine(h, route, final_g, pos_tiles, ys, *, tm, final_norm):
    n_tok, d = h.shape
    return pl.pallas_call(
        functools.partial(_combine_body, tm=tm, final_norm=final_norm),
        grid=(n_tok // tm,),
        in_specs=[pl.BlockSpec((tm, d), lambda i: (i, 0)),
                  pl.BlockSpec((tm, LANES), lambda i: (i, 0)),
                  pl.BlockSpec((1, d), lambda i: (0, 0)),
                  pl.BlockSpec(memory_space=pl.ANY),
                  pl.BlockSpec(memory_space=pl.ANY)],
        out_specs=pl.BlockSpec((tm, d), lambda i: (i, 0)),
        out_shape=jax.ShapeDtypeStruct((n_tok, d), F32),
        scratch_shapes=[pltpu.VMEM((2, 2, tm, d), F32),
                        pltpu.SMEM((2 * tm,), I32),
                        pltpu.SemaphoreType.DMA((2, 2)),
                        pltpu.SemaphoreType.DMA],
        compiler_params=_cparams(1),
        name="moe_combine",
    )(h, route, final_g.reshape(1, d), pos_tiles, ys)


def _moe(h, x, w_out, norm_g, w_router, wg, wu, wd, layer, final_g, *, final_norm, tm_route=1024,
         tm_gmm=512, tm_rows=1024):
    n_tok, d = h.shape
    w_pad = jnp.zeros((d, LANES), F32).at[:, :N_EXPERTS].set(w_router.astype(F32))
    h, route, counts = _router(h, x, w_out, norm_g, w_pad, tm=tm_route)

    cnt = counts[0, :N_EXPERTS].astype(I32)
    gsz = ((cnt + tm_gmm - 1) // tm_gmm) * tm_gmm
    ends = jnp.cumsum(gsz)
    offs = ends - gsz
    rows_pad = (n_tok * 2 // tm_gmm + N_EXPERTS) * tm_gmm
    n_tiles = rows_pad // tm_gmm
    tile_expert = jnp.minimum(
        jnp.searchsorted(ends, jnp.arange(n_tiles, dtype=I32) * tm_gmm, side="right"),
        N_EXPERTS - 1).astype(I32)
    n_used = jnp.maximum(ends[-1:] // tm_gmm, 1).astype(I32)
    tail = ends[-1] + jnp.arange(N_EXPERTS, dtype=I32) * tm_gmm
    info = jnp.concatenate([jnp.where(gsz > 0, ends - tm_gmm, -1),
                            jnp.where(tail < rows_pad, tail, -1)]).astype(I32)

    e01 = route[:, 0:2].astype(I32)
    pos = jnp.clip(offs[jnp.clip(e01, 0, N_EXPERTS - 1)] + route[:, 4:6].astype(I32),
                   0, rows_pad - 1)
    pos_tiles = pos.reshape(n_tok // tm_rows, tm_rows, 2).transpose(0, 2, 1).reshape(
        n_tok // tm_rows, 2 * tm_rows)

    xs = _dispatch(h, norm_g, pos_tiles, info, rows_pad, tm=tm_rows, tg=tm_gmm)
    ys = _gmm(xs, tile_expert, n_used, wg, wu, wd, layer, tm=tm_gmm)
    return _combine(h, route, final_g, pos_tiles, ys, tm=tm_rows, final_norm=final_norm)


def kernel(x, positions, norm_mix_g, norm_ffn_g, final_norm_g, ab_w_in, gla_w_a2, gla_b_a2,
           ret_norm_g, gla_norm_g, ab_w_out, ffn_w_gate, ffn_w_up, ffn_w_down,
           hgrn_lb_logits, c_w_in, hgrn_norm_g, c_w_out, moe_router, moe_w_gate,
           moe_w_up, moe_w_down):
    batch, seq, d = x.shape
    depth = norm_mix_g.shape[0]
    n_tok = batch * seq
    ret_heads = 4
    gla_heads = 4
    gla_rank = gla_w_a2.shape[1]
    hgrn_heads = d // LANES
    ab_cols = ab_w_in.shape[2] - gla_rank

    lb_cum = jnp.cumsum(jax.nn.softmax(hgrn_lb_logits.astype(F32), axis=0), axis=0)
    lower_bounds = lb_cum - lb_cum[0:1]

    wsum = jnp.asarray(_decay_sum_matrix(), BF16)
    inv_freq = ROPE_BASE ** (-jnp.arange(0, LANES, 2, dtype=F32) / LANES)
    inv_freq2 = jnp.concatenate([inv_freq, inv_freq]).reshape(1, LANES)
    cos_t, sin_t = _rope_tables(positions.reshape(n_tok, 1), inv_freq2)

    moe_wg, moe_wu, moe_wd = (w.astype(BF16) for w in (moe_w_gate, moe_w_up, moe_w_down))

    h = x.reshape(n_tok, d)
    for l in range(depth):
        e = l // 2
        if l % 2 == 0:
            w_in = jnp.pad(ab_w_in[e], ((0, 0), (0, LANES - gla_rank))).astype(BF16)
            z = _norm_proj(h, norm_mix_g[l], w_in)
            wa_pad = jnp.pad(gla_w_a2[e], ((0, LANES - gla_rank), (0, 0))).astype(BF16)
            o = _retgla(z, cos_t, sin_t, ret_norm_g[e], wsum, wa_pad, gla_b_a2[e], gla_norm_g[e],
                        batch=batch, seq=seq, ret_heads=ret_heads, gla_heads=gla_heads,
                        col_a=ab_cols)
            h = _ffn(h, [o], [ab_w_out[e].astype(BF16)], norm_ffn_g[l],
                     ffn_w_gate[e].astype(BF16), ffn_w_up[e].astype(BF16),
                     ffn_w_down[e].astype(BF16))
        else:
            z = _norm_proj(h, norm_mix_g[l], c_w_in[e].astype(BF16))
            o = _hgrn(z, wsum, lower_bounds[l], hgrn_norm_g[e], batch=batch, seq=seq,
                      n_heads=hgrn_heads)
            h = _moe(h, o, c_w_out[e].astype(BF16), norm_ffn_g[l], moe_router[e],
                     moe_wg, moe_wu, moe_wd, e, final_norm_g, final_norm=(l == depth - 1))
    if depth % 2 == 1:
        raise NotImplementedError("final norm is fused into the last (odd) layer")
    return h.reshape(batch, seq, d)
```

```python
import functools
import math

import numpy as np
import jax
import jax.numpy as jnp
from jax import lax
from jax.experimental import pallas as pl
from jax.experimental.pallas import tpu as pltpu

F32 = jnp.float32
BF16 = jnp.bfloat16
I32 = jnp.int32

EPS = 1e-6
LANES = 128
MXU_DIM = 256
CHUNK = 64
N_LEVELS = CHUNK.bit_length() - 1
HEAD_DV = 128
ROPE_BASE = 10000.0
GLA_TAU = 16.0
LOG2_E = 1.4426950408889634
N_EXPERTS = 8
ROUTE_SUB = 128
ROW_UNROLL = 8
VMEM_LIMIT = 56 * 1024 * 1024

_NT = (((1,), (1,)), ((), ()))
_TN = (((0,), (0,)), ((), ()))


def _cparams(n_axes):
    return pltpu.CompilerParams(dimension_semantics=("arbitrary",) * n_axes,
                                vmem_limit_bytes=VMEM_LIMIT)


def _rms(x, g):
    ms = jnp.mean(x * x, axis=-1, keepdims=True)
    return x * lax.rsqrt(ms + EPS) * g


def _silu(x, scale=1.0):
    t = jnp.tanh(0.5 * x)
    hx = x * (0.5 * scale)
    return hx + hx * t


def _bdot_nt(a, b):
    return lax.dot_general(a.astype(BF16), b.astype(BF16), _NT, preferred_element_type=F32)


def _norm_proj_body(h_ref, g_ref, w_ref, z_ref, *, col_chunk):
    n = _rms(h_ref[...], g_ref[...]).astype(BF16)
    m = w_ref.shape[1]
    for c in range(0, m, col_chunk):
        ce = min(c + col_chunk, m)
        z_ref[:, c:ce] = jnp.dot(n, w_ref[:, c:ce], preferred_element_type=F32).astype(z_ref.dtype)


def _norm_proj(h, g, w, *, tm=1024, col_chunk=512):
    n_tok, d = h.shape
    m = w.shape[1]
    assert n_tok % tm == 0
    return pl.pallas_call(
        functools.partial(_norm_proj_body, col_chunk=col_chunk),
        grid=(n_tok // tm,),
        in_specs=[pl.BlockSpec((tm, d), lambda i: (i, 0)),
                  pl.BlockSpec((1, d), lambda i: (0, 0)),
                  pl.BlockSpec((d, m), lambda i: (0, 0), pipeline_mode=pl.Buffered(1))],
        out_specs=pl.BlockSpec((tm, m), lambda i: (i, 0)),
        out_shape=jax.ShapeDtypeStruct((n_tok, m), BF16),
        compiler_params=_cparams(1),
        name="norm_proj",
    )(h, g.reshape(1, d), w)


def _ffn_body(*refs, n_in, ff_chunk):
    h_ref = refs[0]
    x_refs = refs[1:1 + n_in]
    w_refs = refs[1 + n_in:1 + 2 * n_in]
    g_ref, wg_ref, wu_ref, wd_ref, o_ref = refs[1 + 2 * n_in:]
    h = h_ref[...]
    for x_ref, w_ref in zip(x_refs, w_refs):
        h = h + jnp.dot(x_ref[...], w_ref[...], preferred_element_type=F32)
    o_ref[...] = h
    n = _rms(h, g_ref[...]).astype(BF16)
    acc = None
    ff = wg_ref.shape[1]
    for c in range(0, ff, ff_chunk):
        ce = min(c + ff_chunk, ff)
        a = jnp.dot(n, wg_ref[:, c:ce], preferred_element_type=F32)
        b = jnp.dot(n, wu_ref[:, c:ce], preferred_element_type=F32)
        hid = (_silu(a) * b).astype(BF16)
        part = jnp.dot(hid, wd_ref[c:ce, :], preferred_element_type=F32)
        acc = part if acc is None else acc + part
    o_ref[...] = o_ref[...] + acc


def _ffn(h, xs, ws, g, wg, wu, wd, *, tm=512, ff_chunk=6 * MXU_DIM):
    n_tok, d = h.shape
    ff = wg.shape[1]
    assert ff % MXU_DIM == 0
    const = lambda shape: pl.BlockSpec(shape, lambda i: (0, 0), pipeline_mode=pl.Buffered(1))
    in_specs = [pl.BlockSpec((tm, d), lambda i: (i, 0))]
    in_specs += [pl.BlockSpec((tm, x.shape[1]), lambda i: (i, 0)) for x in xs]
    in_specs += [const(w.shape) for w in ws]
    in_specs += [pl.BlockSpec((1, d), lambda i: (0, 0)), const((d, ff)), const((d, ff)), const((ff, d))]
    return pl.pallas_call(
        functools.partial(_ffn_body, n_in=len(xs), ff_chunk=ff_chunk),
        grid=(n_tok // tm,),
        in_specs=in_specs,
        out_specs=pl.BlockSpec((tm, d), lambda i: (i, 0)),
        out_shape=jax.ShapeDtypeStruct((n_tok, d), F32),
        compiler_params=_cparams(1),
        name="ffn_swiglu",
    )(h, *xs, *ws, g.reshape(1, d), wg, wu, wd)


def _decay_sum_matrix():
    c = CHUNK
    r = np.arange(c)[:, None]
    t = np.arange(c)[None, :]
    blocks = [(t <= r), (t > r)]
    for lvl in range(N_LEVELS):
        s = c >> (lvl + 1)
        m = (r // (2 * s)) * (2 * s) + s - 1
        upper = r > m
        blocks.append(np.where(upper, (t > m) & (t <= r), (t > r) & (t <= m)))
    w = np.concatenate(blocks, axis=0).astype(np.float32)
    return np.concatenate([w, w], axis=1)


def _level_masks():
    c = CHUNK
    assert 2 * c == LANES
    row = lax.broadcasted_iota(I32, (c, LANES), 0)
    col = lax.broadcasted_iota(I32, (c, LANES), 1) % c
    uppers, pairs = [], []
    for lvl in range(N_LEVELS):
        s = c >> (lvl + 1)
        uppers.append((row // s) % 2 == 1)
        pairs.append(((row // (2 * s)) == (col // (2 * s)))
                     & ((row // s) % 2 == 1) & ((col // s) % 2 == 0))
    return uppers, pairs, row == col


def _single_head_masks():
    c = CHUNK
    ri = lax.broadcasted_iota(I32, (c, c), 0)
    ci = lax.broadcasted_iota(I32, (c, c), 1)
    pairs = []
    for lvl in range(N_LEVELS):
        s = c >> (lvl + 1)
        pairs.append(((ri // (2 * s)) == (ci // (2 * s)))
                     & ((ri // s) % 2 == 1) & ((ci // s) % 2 == 0))
    return pairs, ri == ci


def _block_diag(a, b):
    z = jnp.zeros_like(a)
    return jnp.concatenate([jnp.concatenate([a, z], axis=1),
                            jnp.concatenate([z, b], axis=1)], axis=0)


def _gated_chunk(qs, ks, lfs, vss, lmss, sts, wsum2, masks, pack_pairs):
    c = CHUNK
    uppers, pairs, eye, pairs1, eye1 = masks
    n_g = len(qs)
    pieces = []
    for lf in lfs:
        lf2 = lf * LOG2_E
        hi = lf2.astype(BF16)
        lo = (lf2 - hi.astype(F32)).astype(BF16)
        pieces.append(jnp.concatenate([hi, lo], axis=0))
    ex = jnp.dot(wsum2, jnp.concatenate(pieces, axis=1), preferred_element_type=F32)
    fac = jnp.exp2(ex)
    fac_b = fac.astype(BF16)

    heads = []
    for g in range(n_g):
        cols = slice(g * LANES, (g + 1) * LANES)
        q_b = qs[g].astype(BF16)
        k_b = ks[g].astype(BF16)
        w = [jnp.where(uppers[l], q_b, k_b) * fac_b[(2 + l) * c:(3 + l) * c, cols]
             for l in range(N_LEVELS)]
        qf = q_b * fac_b[0:c, cols]
        kf = k_b * fac_b[c:2 * c, cols]
        st_b = sts[g].astype(BF16)
        qk = qs[g] * ks[g]
        for v, lm in zip(vss[g], lmss[g]):
            if lm is None:
                sel = lambda a: a
            else:
                sel = lambda a, lm=lm: jnp.where(lm, a, jnp.zeros_like(a))
            heads.append(dict(lhs=w, rhs=[sel(x) for x in w], qf=sel(qf), kf=sel(kf), st=st_b, v=v,
                              diag=jnp.sum(sel(qk), axis=-1, keepdims=True)))

    pair_outs = []
    state_update = lambda a: lax.dot_general(a["v"], a["kf"], _TN, preferred_element_type=F32)
    if pack_pairs:
        assert len(heads) % 2 == 0
        lane_lo = lax.broadcasted_iota(I32, (1, LANES), 1) < c
        updates = [state_update(a) for a in heads]
        grams, inters = [], []
        for a, b in zip(heads[0::2], heads[1::2]):
            grams.append([lax.dot_general(jnp.concatenate([a["lhs"][l], b["lhs"][l]], axis=1),
                                          _block_diag(a["rhs"][l], b["rhs"][l]), _NT,
                                          preferred_element_type=F32) for l in range(N_LEVELS)])
            inters.append(lax.dot_general(jnp.concatenate([a["qf"], b["qf"]], axis=1),
                                          _block_diag(a["st"], b["st"]), _NT,
                                          preferred_element_type=F32))
        for p, (a, b) in enumerate(zip(heads[0::2], heads[1::2])):
            scores = jnp.where(eye, jnp.where(lane_lo, a["diag"], b["diag"]), 0.0)
            for l in range(N_LEVELS):
                scores = jnp.where(pairs[l], grams[p][l], scores)
            o2 = jnp.dot(scores.astype(BF16), _block_diag(a["v"], b["v"]),
                         preferred_element_type=F32) + inters[p]
            pair_outs += [o2[:, :HEAD_DV], o2[:, HEAD_DV:]]
    else:
        grams = [[lax.dot_general(a["lhs"][l], a["rhs"][l], _NT, preferred_element_type=F32)
                  for l in range(N_LEVELS)] for a in heads]
        updates = [state_update(a) for a in heads]
        inters = [lax.dot_general(a["qf"], a["st"], _NT, preferred_element_type=F32)
                  for a in heads]
        for p, a in enumerate(heads):
            scores = jnp.where(eye1, a["diag"], 0.0)
            for l in range(N_LEVELS):
                scores = jnp.where(pairs1[l], grams[p][l], scores)
            pair_outs.append(jnp.dot(scores.astype(BF16), a["v"],
                                     preferred_element_type=F32) + inters[p])

    outs, new_sts = [], []
    h = 0
    for g in range(n_g):
        cols = slice(g * LANES, (g + 1) * LANES)
        new_st = sts[g] * fac[c - 1:c, cols]
        outs_g = []
        for _ in vss[g]:
            outs_g.append(pair_outs[h])
            new_st = new_st + updates[h]
            h += 1
        outs.append(outs_g)
        new_sts.append(new_st)
    return outs, new_sts


def _head_out(o, norm_g, gate):
    return (_rms(o, norm_g) * _silu(gate)).astype(BF16)


def _rope_body(pos_ref, invf_ref, cos_ref, sin_ref):
    ang = pos_ref[...].astype(F32) * invf_ref[...]
    lane = lax.broadcasted_iota(I32, (1, LANES), 1)
    cos_ref[...] = jnp.cos(ang)
    sin_ref[...] = jnp.sin(ang) * jnp.where(lane < LANES // 2, -1.0, 1.0)


def _rope_tables(pos_col, inv_freq2, *, tm=2048):
    n_tok = pos_col.shape[0]
    tm = min(tm, n_tok)
    table = jax.ShapeDtypeStruct((n_tok, LANES), F32)
    return pl.pallas_call(
        _rope_body,
        grid=(n_tok // tm,),
        in_specs=[pl.BlockSpec((tm, 1), lambda i: (i, 0)),
                  pl.BlockSpec((1, LANES), lambda i: (0, 0))],
        out_specs=[pl.BlockSpec((tm, LANES), lambda i: (i, 0))] * 2,
        out_shape=[table, table],
        compiler_params=_cparams(1),
        name="rope_tables",
    )(pos_col, inv_freq2)


def _retgla_body(cos_ref, sin_ref, rng_ref, wsum_ref, wa_ref, ba2_ref, gng_ref,
                aq_ref, ak_ref, av_ref, ag_ref, ba_ref, bq_ref, bk_ref, bv_ref, bg_ref, o_ref,
                rst_ref, gst_ref, lf_ref, *, ret_heads, gla_groups):
    c = CHUNK
    tb = aq_ref.shape[0]

    @pl.when(pl.program_id(1) == 0)
    def _():
        rst_ref[...] = jnp.zeros_like(rst_ref)
        gst_ref[...] = jnp.zeros_like(gst_ref)

    x = jnp.dot(ba_ref[...], wa_ref[...], preferred_element_type=F32) + ba2_ref[...]
    lf_ref[...] = (jnp.minimum(x, 0.0) - jnp.log1p(jnp.exp(-jnp.abs(x)))) * (1.0 / GLA_TAU)

    ri = lax.broadcasted_iota(I32, (c, c), 0)
    ci = lax.broadcasted_iota(I32, (c, c), 1)
    rel = (ri - ci).astype(F32)
    trow = lax.broadcasted_iota(I32, (c, LANES), 0).astype(F32)
    k_scale = float(LANES) ** -0.5
    log_gammas = [math.log1p(-(2.0 ** (-5 - h))) for h in range(ret_heads)]
    dmats = [jnp.where(rel >= 0, jnp.exp(rel * lg), 0.0) for lg in log_gammas]
    q_decays = [jnp.exp((trow + 1.0) * lg) for lg in log_gammas]
    k_decays = [jnp.exp((c - 1.0 - trow) * lg) * k_scale for lg in log_gammas]
    hcols = [slice(h * LANES, (h + 1) * LANES) for h in range(ret_heads)]
    ret_w = ret_heads * HEAD_DV

    masks = _level_masks() + _single_head_masks()
    lane = lax.broadcasted_iota(I32, (1, LANES), 1)
    lane_masks = [lane < LANES // 2, lane >= LANES // 2]
    wsum = wsum_ref[...]
    rng = rng_ref[...]
    gng = gng_ref[...]
    q_scale = float(LANES // 2) ** -0.5
    kcols = [slice(g * LANES, (g + 1) * LANES) for g in range(gla_groups)]
    vcols = [[slice((2 * g + j) * HEAD_DV, (2 * g + j + 1) * HEAD_DV) for j in range(2)]
             for g in range(gla_groups)]

    def chunk(ic, carry):
        r0 = pl.multiple_of(ic * c, c)
        rows = pl.ds(r0, c)
        cosv = cos_ref[rows, :]
        sinv = sin_ref[rows, :]
        scores, inters, updates, vs = [], [], [], []
        for h, cols in enumerate(hcols):
            q = aq_ref[rows, cols].astype(F32)
            k = ak_ref[rows, cols].astype(F32)
            v = av_ref[rows, cols]
            qr = q * cosv + pltpu.roll(q, LANES // 2, 1) * sinv
            kr = k * cosv + pltpu.roll(k, LANES // 2, 1) * sinv
            scores.append(_bdot_nt(qr, kr * k_scale))
            inters.append(_bdot_nt(qr * q_decays[h], rst_ref[h]))
            updates.append(lax.dot_general(v, (kr * k_decays[h]).astype(BF16), _TN,
                                           preferred_element_type=F32))
            vs.append(v)
        outs, new_sts = _gated_chunk(
            [bq_ref[rows, kc].astype(F32) * q_scale for kc in kcols],
            [bk_ref[rows, kc].astype(F32) for kc in kcols],
            [lf_ref[rows, kc] for kc in kcols],
            [[bv_ref[rows, vc] for vc in vcs] for vcs in vcols],
            [lane_masks] * gla_groups,
            [gst_ref[g] for g in range(gla_groups)],
            wsum, masks, pack_pairs=True)
        for g in range(gla_groups):
            gst_ref[g] = new_sts[g]
            for o, vc in zip(outs[g], vcols[g]):
                oc = slice(ret_w + vc.start, ret_w + vc.stop)
                o_ref[rows, oc] = _head_out(o, gng, bg_ref[rows, vc].astype(F32))
        for h, cols in enumerate(hcols):
            o = jnp.dot((scores[h] * dmats[h]).astype(BF16), vs[h],
                        preferred_element_type=F32) + inters[h]
            rst_ref[h] = math.exp(c * log_gammas[h]) * rst_ref[h] + updates[h]
            o_ref[rows, cols] = _head_out(o, rng, ag_ref[rows, cols].astype(F32))
        return carry

    lax.fori_loop(0, tb // c, chunk, 0, unroll=8)


def _retgla(z, cos_t, sin_t, ret_g, wsum, wa_pad, b_a2, gla_g, *, batch, seq, ret_heads,
            gla_heads, col_a, tb=512):
    n_tok = z.shape[0]
    wr = ret_heads * LANES
    wk = gla_heads * (LANES // 2)
    wv = gla_heads * HEAD_DV
    col0 = 4 * wr
    assert seq % tb == 0 and tb % CHUNK == 0
    nt = seq // tb
    row = lambda b, t: b * nt + t
    const = lambda shape: pl.BlockSpec(shape, lambda b, t: (0, 0))
    zblk = lambda w, j: pl.BlockSpec((tb, w), lambda b, t: (row(b, t), j))
    tspec = pl.BlockSpec((tb, LANES), lambda b, t: (row(b, t), 0))
    return pl.pallas_call(
        functools.partial(_retgla_body, ret_heads=ret_heads, gla_groups=gla_heads // 2),
        grid=(batch, nt),
        in_specs=[tspec, tspec, const((1, LANES)),
                  const(wsum.shape), const(wa_pad.shape), const((1, wk)), const((1, LANES)),
                  zblk(wr, 0), zblk(wr, 1), zblk(wr, 2), zblk(wr, 3),
                  zblk(LANES, col_a // LANES),
                  zblk(wk, col0 // wk), zblk(wk, col0 // wk + 1),
                  zblk(wv, (col0 + 2 * wk) // wv), zblk(wv, (col0 + 2 * wk) // wv + 1)],
        out_specs=pl.BlockSpec((tb, wr + wv), lambda b, t: (row(b, t), 0)),
        out_shape=jax.ShapeDtypeStruct((n_tok, wr + wv), BF16),
        scratch_shapes=[pltpu.VMEM((ret_heads, HEAD_DV, LANES), F32),
                        pltpu.VMEM((gla_heads // 2, HEAD_DV, LANES), F32),
                        pltpu.VMEM((tb, wk), F32)],
        compiler_params=_cparams(2),
        name="retention_gla",
    )(cos_t, sin_t, ret_g.reshape(1, LANES), wsum, wa_pad, b_a2.reshape(1, wk),
      gla_g.reshape(1, LANES), z, z, z, z, z, z, z, z, z)


def _hgrn_body(wsum_ref, lb_ref, ng_ref, q_ref, f_ref, i_ref, gate_ref, o_ref, st_ref, *, n_heads):
    c = CHUNK
    tb = q_ref.shape[0]

    @pl.when(pl.program_id(2) == 0)
    def _():
        st_ref[...] = jnp.zeros_like(st_ref)

    masks = _level_masks() + _single_head_masks()
    wsum = wsum_ref[...]
    ng = ng_ref[...]
    q_scale = float(LANES) ** -0.5

    def chunk(ic, carry):
        r0 = pl.multiple_of(ic * c, c)
        rows = pl.ds(r0, c)
        hcols = [slice(h * LANES, (h + 1) * LANES) for h in range(n_heads)]
        qs, ks, lfs = [], [], []
        for cols in hcols:
            lb = lb_ref[:, cols]
            f = f_ref[rows, cols].astype(F32)
            t = jnp.tanh(0.5 * f)
            b = 0.5 * (1.0 - lb)
            bt = b * t
            lfs.append(jnp.log((0.5 * (1.0 + lb)) + bt))
            ks.append(b - bt)
            qs.append(_silu(q_ref[rows, cols].astype(F32), q_scale))
        outs, new_sts = _gated_chunk(
            qs, ks, lfs, [[i_ref[rows, cols]] for cols in hcols], [[None]] * n_heads,
            [st_ref[h] for h in range(n_heads)], wsum, masks, pack_pairs=False)
        for h, cols in enumerate(hcols):
            st_ref[h] = new_sts[h]
            o_ref[rows, cols] = _head_out(outs[h][0], ng, gate_ref[rows, cols].astype(F32))
        return carry

    lax.fori_loop(0, tb // c, chunk, 0, unroll=2)


def _hgrn(z, wsum, lb, norm_g, *, batch, seq, n_heads, heads_per_step=8, tb=512):
    n_tok = z.shape[0]
    w = heads_per_step * LANES
    ng_ = n_heads // heads_per_step
    assert seq % tb == 0 and tb % CHUNK == 0
    nt = seq // tb
    zspec = lambda j: pl.BlockSpec((tb, w), lambda b, g, t, j=j: (b * nt + t, j * ng_ + g))
    return pl.pallas_call(
        functools.partial(_hgrn_body, n_heads=heads_per_step),
        grid=(batch, ng_, nt),
        in_specs=[pl.BlockSpec(wsum.shape, lambda b, g, t: (0, 0)),
                  pl.BlockSpec((1, w), lambda b, g, t: (0, g)),
                  pl.BlockSpec((1, LANES), lambda b, g, t: (0, 0)),
                  zspec(0), zspec(1), zspec(2), zspec(3)],
        out_specs=pl.BlockSpec((tb, w), lambda b, g, t: (b * nt + t, g)),
        out_shape=jax.ShapeDtypeStruct((n_tok, n_heads * LANES), BF16),
        scratch_shapes=[pltpu.VMEM((heads_per_step, HEAD_DV, LANES), F32)],
        compiler_params=_cparams(3),
        name="hgrn2",
    )(wsum, lb.reshape(1, n_heads * LANES), norm_g.reshape(1, LANES), z, z, z, z)


def _router_body(h_ref, x_ref, wo_ref, g_ref, w_ref, h1_ref, route_ref, cnt_ref, run_ref):
    tm = h_ref.shape[0]

    @pl.when(pl.program_id(0) == 0)
    def _():
        run_ref[...] = jnp.zeros_like(run_ref)

    ts = ROUTE_SUB
    w = w_ref[...]
    w_hi = w.astype(BF16)
    w_lo = (w - w_hi.astype(F32)).astype(BF16)
    lane = lax.broadcasted_iota(I32, (ts, LANES), 1)
    lane_f = lane.astype(F32)
    ri = lax.broadcasted_iota(I32, (ts, ts), 0)
    ci = lax.broadcasted_iota(I32, (ts, ts), 1)
    lstrict = jnp.where(ri > ci, 1.0, 0.0).astype(BF16)
    neg = -jnp.inf
    subs = [slice(r0, r0 + ts) for r0 in range(0, tm, ts)]
    h1s = [h_ref[rows, :] + jnp.dot(x_ref[rows, :], wo_ref[...], preferred_element_type=F32)
           for rows in subs]
    logits = []
    for rows, h1 in zip(subs, h1s):
        h1_ref[rows, :] = h1
        n = _rms(h1, g_ref[...])
        n_hi = n.astype(BF16)
        n_lo = (n - n_hi.astype(F32)).astype(BF16)
        logits.append(jnp.dot(n_hi, w_hi, preferred_element_type=F32)
                      + jnp.dot(n_hi, w_lo, preferred_element_type=F32)
                      + jnp.dot(n_lo, w_hi, preferred_element_type=F32))
    picks = []
    for lg in logits:
        lg1 = jnp.where(lane < N_EXPERTS, lg, neg)
        m1 = jnp.max(lg1, axis=-1, keepdims=True)
        i1 = jnp.min(jnp.where(lg1 == m1, lane_f, float(LANES)), axis=-1, keepdims=True)
        oh1 = lane_f == i1
        lg2 = jnp.where(oh1, neg, lg1)
        m2 = jnp.max(lg2, axis=-1, keepdims=True)
        i2 = jnp.min(jnp.where(lg2 == m2, lane_f, float(LANES)), axis=-1, keepdims=True)
        oh2 = lane_f == i2
        e2 = jnp.exp(m2 - m1)
        g1 = 1.0 / (1.0 + e2)
        both = jnp.where(oh1, 1.0, 0.0) + jnp.where(oh2, 1.0, 0.0)
        prefix = jnp.dot(lstrict, both.astype(BF16), preferred_element_type=F32)
        picks.append((i1, i2, g1, e2 * g1, oh1, oh2, both, prefix))
    run = run_ref[...]
    for rows, (i1, i2, g1, g2, oh1, oh2, both, prefix) in zip(subs, picks):
        before = prefix + run
        r1 = jnp.sum(jnp.where(oh1, before, 0.0), axis=-1, keepdims=True)
        r2 = jnp.sum(jnp.where(oh2, before, 0.0), axis=-1, keepdims=True)
        run = run + jnp.sum(both, axis=0, keepdims=True)
        out = jnp.where(lane == 0, i1, 0.0)
        out = jnp.where(lane == 1, i2, out)
        out = jnp.where(lane == 2, g1, out)
        out = jnp.where(lane == 3, g2, out)
        out = jnp.where(lane == 4, r1, out)
        out = jnp.where(lane == 5, r2, out)
        route_ref[rows, :] = out
    run_ref[...] = run
    cnt_ref[...] = run


def _router(h, x, w_out, g, w_pad, *, tm=512):
    n_tok, d = h.shape
    return pl.pallas_call(
        _router_body,
        grid=(n_tok // tm,),
        in_specs=[pl.BlockSpec((tm, d), lambda i: (i, 0)),
                  pl.BlockSpec((tm, x.shape[1]), lambda i: (i, 0)),
                  pl.BlockSpec(w_out.shape, lambda i: (0, 0), pipeline_mode=pl.Buffered(1)),
                  pl.BlockSpec((1, d), lambda i: (0, 0)),
                  pl.BlockSpec((d, LANES), lambda i: (0, 0))],
        out_specs=[pl.BlockSpec((tm, d), lambda i: (i, 0)),
                   pl.BlockSpec((tm, LANES), lambda i: (i, 0)),
                   pl.BlockSpec((1, LANES), lambda i: (0, 0))],
        out_shape=[jax.ShapeDtypeStruct((n_tok, d), F32),
                   jax.ShapeDtypeStruct((n_tok, LANES), F32),
                   jax.ShapeDtypeStruct((1, LANES), F32)],
        scratch_shapes=[pltpu.VMEM((1, LANES), F32)],
        compiler_params=_cparams(1),
        name="moe_router",
    )(h, x, w_out, g.reshape(1, d), w_pad)


def _dispatch_body(info_ref, h_ref, g_ref, pos_ref, xs_ref, nbuf, zbuf, idx, sems, isem, *, tm):
    i = pl.program_id(0)
    last = pl.num_programs(0) - 1
    slot = i % 2
    tg = zbuf.shape[0]

    def zero_copy(e):
        return pltpu.make_async_copy(
            zbuf, xs_ref.at[pl.ds(pl.multiple_of(info_ref[e], tg), tg)], sems.at[0, 0])

    @pl.when(i == 0)
    def _():
        zbuf[...] = jnp.zeros_like(zbuf)
        for e in range(2 * N_EXPERTS):
            @pl.when(info_ref[e] >= 0)
            def _():
                zero_copy(e).start()
        for e in range(2 * N_EXPERTS):
            @pl.when(info_ref[e] >= 0)
            def _():
                zero_copy(e).wait()

    def drain(s):
        for k in range(2):
            pltpu.make_async_copy(nbuf.at[s], nbuf.at[s], sems.at[s, k]).wait()

    @pl.when(i >= 2)
    def _():
        drain(slot)

    idx_copy = pltpu.make_async_copy(pos_ref.at[i], idx, isem)
    idx_copy.start()
    nbuf[slot] = _rms(h_ref[...], g_ref[...]).reshape(nbuf.shape[1:])
    idx_copy.wait()

    def issue(j, carry):
        for u in range(ROW_UNROLL):
            for k in range(2):
                pltpu.make_async_copy(nbuf.at[slot, j, pl.ds(u, 1)],
                                      xs_ref.at[pl.ds(idx[k * tm + j * ROW_UNROLL + u], 1)],
                                      sems.at[slot, k]).start()
        return carry

    lax.fori_loop(0, tm // ROW_UNROLL, issue, 0)

    @pl.when(i == last)
    def _():
        drain(slot)

        @pl.when(i >= 1)
        def _():
            drain(1 - slot)


def _dispatch(h, g, pos_tiles, info, rows_pad, *, tm, tg):
    n_tok, d = h.shape
    return pl.pallas_call(
        functools.partial(_dispatch_body, tm=tm),
        grid_spec=pltpu.PrefetchScalarGridSpec(
            num_scalar_prefetch=1,
            grid=(n_tok // tm,),
            in_specs=[pl.BlockSpec((tm, d), lambda i, info: (i, 0)),
                      pl.BlockSpec((1, d), lambda i, info: (0, 0)),
                      pl.BlockSpec(memory_space=pl.ANY)],
            out_specs=pl.BlockSpec(memory_space=pl.ANY),
            scratch_shapes=[pltpu.VMEM((2, tm // ROW_UNROLL, ROW_UNROLL, d), F32),
                            pltpu.VMEM((tg, d), F32),
                            pltpu.SMEM((2 * tm,), I32),
                            pltpu.SemaphoreType.DMA((2, 2)),
                            pltpu.SemaphoreType.DMA]),
        out_shape=jax.ShapeDtypeStruct((rows_pad, d), F32),
        compiler_params=_cparams(1),
        name="moe_dispatch",
    )(info, h, g.reshape(1, d), pos_tiles)


def _gmm_body(te_ref, nu_ref, x_ref, wg_ref, wu_ref, wd_ref, y_ref, *, ff_chunk):
    @pl.when(pl.program_id(0) >= nu_ref[0])
    def _():
        y_ref[...] = jnp.zeros_like(y_ref)

    @pl.when(pl.program_id(0) < nu_ref[0])
    def _():
        x = x_ref[...].astype(BF16)
        acc = None
        for c in range(0, wg_ref.shape[1], ff_chunk):
            a = jnp.dot(x, wg_ref[:, c:c + ff_chunk], preferred_element_type=F32)
            b = jnp.dot(x, wu_ref[:, c:c + ff_chunk], preferred_element_type=F32)
            hid = (_silu(a) * b).astype(BF16)
            part = jnp.dot(hid, wd_ref[c:c + ff_chunk, :], preferred_element_type=F32)
            acc = part if acc is None else acc + part
        y_ref[...] = acc


def _gmm(xs, tile_expert, n_used, wg, wu, wd, layer, *, tm, ff_chunk=7 * MXU_DIM):
    rows_pad, d = xs.shape
    ff = wg.shape[3]
    assert ff % ff_chunk == 0 and ff_chunk % MXU_DIM == 0
    tile = lambda i, te, nu: jnp.minimum(i, nu[0] - 1)
    wspec = lambda shape: pl.BlockSpec(
        (None, None) + shape, lambda i, te, nu: (layer, te[tile(i, te, nu)], 0, 0),
        pipeline_mode=pl.Buffered(1))
    return pl.pallas_call(
        functools.partial(_gmm_body, ff_chunk=ff_chunk),
        grid_spec=pltpu.PrefetchScalarGridSpec(
            num_scalar_prefetch=2,
            grid=(rows_pad // tm,),
            in_specs=[pl.BlockSpec((tm, d), lambda i, te, nu: (tile(i, te, nu), 0)),
                      wspec((d, ff)), wspec((d, ff)), wspec((ff, d))],
            out_specs=pl.BlockSpec((tm, d), lambda i, te, nu: (i, 0))),
        out_shape=jax.ShapeDtypeStruct((rows_pad, d), F32),
        compiler_params=_cparams(1),
        name="moe_experts",
    )(tile_expert, n_used, xs, wg, wu, wd)


def _combine_body(h_ref, route_ref, fg_ref, pos_ref, ys_ref, o_ref, gbuf, idx, sems, isem, *,
                  tm, final_norm):
    i = pl.program_id(0)
    slot = i % 2

    def gather(tile, s):
        idx_copy = pltpu.make_async_copy(pos_ref.at[tile], idx, isem)
        idx_copy.start()
        idx_copy.wait()

        def issue(j, carry):
            for u in range(ROW_UNROLL):
                for k in range(2):
                    pltpu.make_async_copy(ys_ref.at[pl.ds(idx[k * tm + j * ROW_UNROLL + u], 1)],
                                          gbuf.at[s, k, j, pl.ds(u, 1)], sems.at[s, k]).start()
            return carry

        lax.fori_loop(0, tm // ROW_UNROLL, issue, 0)

    @pl.when(i == 0)
    def _():
        gather(0, 0)

    @pl.when(i + 1 < pl.num_programs(0))
    def _():
        gather(i + 1, 1 - slot)

    for k in range(2):
        pltpu.make_async_copy(gbuf.at[slot, k], gbuf.at[slot, k], sems.at[slot, k]).wait()

    route = route_ref[...]
    y0 = gbuf[slot, 0].reshape(h_ref.shape)
    y1 = gbuf[slot, 1].reshape(h_ref.shape)
    out = h_ref[...] + route[:, 2:3] * y0 + route[:, 3:4] * y1
    if final_norm:
        out = _rms(out, fg_ref[...])
    o_ref[...] = out


def _combine(h, route, final_g, pos_tiles, ys, *, tm, final_norm):
    n_tok, d = h.shape
    return pl.pallas_call(
        functools.partial(_combine_body, tm=tm, final_norm=final_norm),
        grid=(n_tok // tm,),
        in_specs=[pl.BlockSpec((tm, d), lambda i: (i, 0)),
                  pl.BlockSpec((tm, LANES), lambda i: (i, 0)),
                  pl.BlockSpec((1, d), lambda i: (0, 0)),
                  pl.BlockSpec(memory_space=pl.ANY),
                  pl.BlockSpec(memory_space=pl.ANY)],
        out_specs=pl.BlockSpec((tm, d), lambda i: (i, 0)),
        out_shape=jax.ShapeDtypeStruct((n_tok, d), F32),
        scratch_shapes=[pltpu.VMEM((2, 2, tm // ROW_UNROLL, ROW_UNROLL, d), F32),
                        pltpu.SMEM((2 * tm,), I32),
                        pltpu.SemaphoreType.DMA((2, 2)),
                        pltpu.SemaphoreType.DMA],
        compiler_params=_cparams(1),
        name="moe_combine",
    )(h, route, final_g.reshape(1, d), pos_tiles, ys)


def _moe(h, x, w_out, norm_g, w_router, wg, wu, wd, layer, final_g, *, final_norm, tm_route=1024,
         tm_gmm=512, tm_rows=1024):
    n_tok, d = h.shape
    w_pad = jnp.zeros((d, LANES), F32).at[:, :N_EXPERTS].set(w_router.astype(F32))
    h, route, counts = _router(h, x, w_out, norm_g, w_pad, tm=tm_route)

    cnt = counts[0, :N_EXPERTS].astype(I32)
    gsz = ((cnt + tm_gmm - 1) // tm_gmm) * tm_gmm
    ends = jnp.cumsum(gsz)
    offs = ends - gsz
    rows_pad = (n_tok * 2 // tm_gmm + N_EXPERTS) * tm_gmm
    n_tiles = rows_pad // tm_gmm
    tile_expert = jnp.minimum(
        jnp.searchsorted(ends, jnp.arange(n_tiles, dtype=I32) * tm_gmm, side="right"),
        N_EXPERTS - 1).astype(I32)
    n_used = jnp.maximum(ends[-1:] // tm_gmm, 1).astype(I32)
    tail = ends[-1] + jnp.arange(N_EXPERTS, dtype=I32) * tm_gmm
    info = jnp.concatenate([jnp.where(gsz > 0, ends - tm_gmm, -1),
                            jnp.where(tail < rows_pad, tail, -1)]).astype(I32)

    e01 = route[:, 0:2].astype(I32)
    pos = jnp.clip(offs[jnp.clip(e01, 0, N_EXPERTS - 1)] + route[:, 4:6].astype(I32),
                   0, rows_pad - 1)
    pos_tiles = pos.reshape(n_tok // tm_rows, tm_rows, 2).transpose(0, 2, 1).reshape(
        n_tok // tm_rows, 2 * tm_rows)

    xs = _dispatch(h, norm_g, pos_tiles, info, rows_pad, tm=tm_rows, tg=tm_gmm)
    ys = _gmm(xs, tile_expert, n_used, wg, wu, wd, layer, tm=tm_gmm)
    return _combine(h, route, final_g, pos_tiles, ys, tm=tm_rows, final_norm=final_norm)


def kernel(x, positions, norm_mix_g, norm_ffn_g, final_norm_g, ab_w_in, gla_w_a2, gla_b_a2,
           ret_norm_g, gla_norm_g, ab_w_out, ffn_w_gate, ffn_w_up, ffn_w_down,
           hgrn_lb_logits, c_w_in, hgrn_norm_g, c_w_out, moe_router, moe_w_gate,
           moe_w_up, moe_w_down):
    batch, seq, d = x.shape
    depth = norm_mix_g.shape[0]
    n_tok = batch * seq
    ret_heads = 4
    gla_heads = 4
    gla_rank = gla_w_a2.shape[1]
    hgrn_heads = d // LANES
    ab_cols = ab_w_in.shape[2] - gla_rank

    lb_cum = jnp.cumsum(jax.nn.softmax(hgrn_lb_logits.astype(F32), axis=0), axis=0)
    lower_bounds = lb_cum - lb_cum[0:1]

    wsum = jnp.asarray(_decay_sum_matrix(), BF16)
    inv_freq = ROPE_BASE ** (-jnp.arange(0, LANES, 2, dtype=F32) / LANES)
    inv_freq2 = jnp.concatenate([inv_freq, inv_freq]).reshape(1, LANES)
    cos_t, sin_t = _rope_tables(positions.reshape(n_tok, 1), inv_freq2)

    moe_wg, moe_wu, moe_wd = (w.astype(BF16) for w in (moe_w_gate, moe_w_up, moe_w_down))

    h = x.reshape(n_tok, d)
    for l in range(depth):
        e = l // 2
        if l % 2 == 0:
            w_in = jnp.pad(ab_w_in[e], ((0, 0), (0, LANES - gla_rank))).astype(BF16)
            z = _norm_proj(h, norm_mix_g[l], w_in)
            wa_pad = jnp.pad(gla_w_a2[e], ((0, LANES - gla_rank), (0, 0))).astype(BF16)
            o = _retgla(z, cos_t, sin_t, ret_norm_g[e], wsum, wa_pad, gla_b_a2[e], gla_norm_g[e],
                        batch=batch, seq=seq, ret_heads=ret_heads, gla_heads=gla_heads,
                        col_a=ab_cols)
            h = _ffn(h, [o], [ab_w_out[e].astype(BF16)], norm_ffn_g[l],
                     ffn_w_gate[e].astype(BF16), ffn_w_up[e].astype(BF16),
                     ffn_w_down[e].astype(BF16))
        else:
            z = _norm_proj(h, norm_mix_g[l], c_w_in[e].astype(BF16))
            o = _hgrn(z, wsum, lower_bounds[l], hgrn_norm_g[e], batch=batch, seq=seq,
                      n_heads=hgrn_heads)
            h = _moe(h, o, c_w_out[e].astype(BF16), norm_ffn_g[l], moe_router[e],
                     moe_wg, moe_wu, moe_wd, e, final_norm_g, final_norm=(l == depth - 1))
    if depth % 2 == 1:
        raise NotImplementedError("final norm is fused into the last (odd) layer")
    return h.reshape(batch, seq, d)
```
